```python
import math
import jax
import jax.numpy as jnp
from jax import lax
import numpy as np

D_MODEL = 1024
BATCH = 8
SEQ = 2048
DEPTH = 1

GRID_W = 64
CTX_LEN = 256
N_DIR = 2
CONV_K = 5
CHUNK = 64
EPS = 1e-6
FFN_RES = 0.5
D_FF = 2816

GDN_HEADS = 8
GDN_DK = 128
GDN_DV = 128
GDN_QK = GDN_HEADS * GDN_DK
GDN_V = GDN_HEADS * GDN_DV

MB_INNER = 2 * D_MODEL
MB_HEADDIM = 64
MB_HEADS = MB_INNER // MB_HEADDIM
MB_GROUPS = 2
MB_HPG = MB_HEADS // MB_GROUPS
MB_STATE = 128
MB_XBC = MB_INNER + 2 * MB_GROUPS * MB_STATE

IN_SIZES = (2 * GDN_QK + GDN_V, GDN_V, N_DIR * GDN_HEADS, N_DIR * GDN_HEADS, MB_INNER, MB_XBC, N_DIR * MB_HEADS, 2 * D_MODEL)
D_IN = 2 * GDN_QK + 2 * GDN_V + 2 * N_DIR * GDN_HEADS + MB_INNER + MB_XBC + N_DIR * MB_HEADS + 2 * D_MODEL

kernel_name = "hybrid_gdn_ssd_macaron_prefix"


def rmsnorm(x, g):
    xf = x.astype(jnp.float32)
    y = xf * lax.rsqrt(jnp.mean(xf * xf, axis=-1, keepdims=True) + EPS)
    return (y * g.astype(jnp.float32)).astype(x.dtype)


def l2norm(t):
    tf = t.astype(jnp.float32)
    return tf * lax.rsqrt(jnp.sum(tf * tf, axis=-1, keepdims=True) + EPS)


def swiglu(h, w_gu, w_down):
    gt, up = jnp.split(h @ w_gu, 2, axis=-1)
    return (jax.nn.silu(gt) * up) @ w_down


def ffn_sublayer(x, g, shift, scale, gate, w_gu, w_down):
    h = rmsnorm(x, g) * (1.0 + scale) + shift
    return x + FFN_RES * gate * swiglu(h, w_gu, w_down)


def dwconv(u, w):
    return lax.conv_general_dilated(
        u, w[:, None, :].astype(u.dtype), window_strides=(1,),
        padding=[(CONV_K // 2, CONV_K // 2)],
        dimension_numbers=('NWC', 'WIO', 'NWC'), feature_group_count=u.shape[-1])


def grid_conv(u, w):
    b, n, ch = u.shape
    rows = n // GRID_W
    return dwconv(u.reshape(b * rows, GRID_W, ch), w).reshape(b, n, ch)


def gated_delta_chunked(q, k, v, g, beta, s0):
    f32 = jnp.float32
    b, n_tok, nh, _ = q.shape
    dv = v.shape[-1]
    nc = n_tok // CHUNK

    def blocks(t):
        t = t.astype(f32).reshape(b, nc, CHUNK, nh, *t.shape[3:])
        return jnp.moveaxis(t, 2, 3)

    q, k, v, beta = blocks(q), blocks(k), blocks(v), blocks(beta)
    g = jnp.cumsum(blocks(g), axis=-1)
    idx = jnp.arange(CHUNK)
    causal = idx[:, None] >= idx[None, :]
    strict = idx[:, None] > idx[None, :]
    decay = jnp.exp(jnp.where(causal, g[..., :, None] - g[..., None, :], -jnp.inf))
    kb = k * beta[..., None]
    a_mat = jnp.where(strict, jnp.einsum('bnhid,bnhjd->bnhij', kb, k) * decay, 0.0) + jnp.eye(CHUNK, dtype=f32)
    rhs = jnp.concatenate([v * beta[..., None], kb * jnp.exp(g)[..., None]], axis=-1)
    sol = lax.linalg.triangular_solve(a_mat, rhs, left_side=True, lower=True, unit_diagonal=True)
    u, w = sol[..., :dv], sol[..., dv:]
    qk = jnp.einsum('bnhid,bnhjd->bnhij', q, k) * decay
    q_in = q * jnp.exp(g)[..., None]
    k_out = k * jnp.exp(g[..., -1:] - g)[..., None]
    g_tot = jnp.exp(g[..., -1])

    def step(s, inp):
        q_c, k_c, u_c, w_c, qk_c, gt_c = inp
        v_new = u_c - jnp.einsum('bhcd,bhde->bhce', w_c, s)
        o_c = jnp.einsum('bhcd,bhde->bhce', q_c, s) + jnp.einsum('bhij,bhje->bhie', qk_c, v_new)
        s = s * gt_c[..., None, None] + jnp.einsum('bhcd,bhce->bhde', k_c, v_new)
        return s, o_c

    seq = tuple(jnp.moveaxis(t, 1, 0) for t in (q_in, k_out, u, w, qk, g_tot))
    s_fin, o = lax.scan(step, s0.astype(f32), seq)
    o = jnp.moveaxis(jnp.moveaxis(o, 0, 1), 2, 3).reshape(b, n_tok, nh, dv)
    return o, s_fin


def ssd_chunked(xs, dt, a_neg, bm, cm, s0):
    f32 = jnp.float32
    b, n_tok = xs.shape[:2]
    nc = n_tok // CHUNK
    a = (dt * a_neg).reshape(b, nc, CHUNK, MB_GROUPS, MB_HPG)
    a_cs = jnp.cumsum(jnp.moveaxis(a, 2, -1), axis=-1)
    xdt = (xs.astype(f32) * dt[..., None]).reshape(b, nc, CHUNK, MB_GROUPS, MB_HPG, MB_HEADDIM)
    bc = bm.astype(f32).reshape(b, nc, CHUNK, MB_GROUPS, MB_STATE)
    cc = cm.astype(f32).reshape(b, nc, CHUNK, MB_GROUPS, MB_STATE)
    idx = jnp.arange(CHUNK)
    causal = idx[:, None] >= idx[None, :]
    seg = jnp.exp(jnp.where(causal, a_cs[..., :, None] - a_cs[..., None, :], -jnp.inf))
    cb = jnp.einsum('bclgd,bcsgd->bcgls', cc, bc)
    y_diag = jnp.einsum('bcghls,bcsghp->bclghp', cb[:, :, :, None] * seg, xdt)

    def step(s, inp):
        c_c, b_c, x_c, acs = inp
        y_c = jnp.einsum('blgd,bghpd,bghl->blghp', c_c, s, jnp.exp(acs))
        w_end = jnp.exp(acs[..., -1:] - acs)
        s = s * jnp.exp(acs[..., -1])[..., None, None] + jnp.einsum('bsgd,bghs,bsghp->bghpd', b_c, w_end, x_c)
        return s, y_c

    seq = tuple(jnp.moveaxis(t, 1, 0) for t in (cc, bc, xdt, a_cs))
    s_fin, y_off = lax.scan(step, s0.astype(f32), seq)
    y = y_diag + jnp.moveaxis(y_off, 0, 1)
    return y.reshape(b, n_tok, MB_GROUPS, MB_HPG, MB_HEADDIM), s_fin


def mixer_features(h, w_in, gdn_conv_w, mb_conv_w, mb_conv_b, conv):
    b, n, _ = h.shape
    u = h @ w_in
    qkv, za, a, beta, zb, xbc, dt, gates = jnp.split(u, np.cumsum(IN_SIZES)[:-1].tolist(), axis=-1)
    qkv = jax.nn.silu(conv(qkv, gdn_conv_w))
    q, k, v = jnp.split(qkv, [GDN_QK, 2 * GDN_QK], axis=-1)
    q = l2norm(q.reshape(b, n, GDN_HEADS, GDN_DK)) * GDN_DK ** -0.5
    k = l2norm(k.reshape(b, n, GDN_HEADS, GDN_DK))
    xbc = jax.nn.silu(conv(xbc, mb_conv_w) + mb_conv_b)
    xs, bm, cm = jnp.split(xbc, [MB_INNER, MB_INNER + MB_GROUPS * MB_STATE], axis=-1)
    return {
        'q': q, 'k': k, 'v': v.reshape(b, n, GDN_HEADS, GDN_DV), 'zA': za,
        'a': a.reshape(b, n, N_DIR, GDN_HEADS), 'beta': beta.reshape(b, n, N_DIR, GDN_HEADS),
        'zB': zb, 'xs': xs.reshape(b, n, MB_GROUPS, MB_HPG, MB_HEADDIM),
        'Bm': bm.reshape(b, n, MB_GROUPS, MB_STATE), 'Cm': cm.reshape(b, n, MB_GROUPS, MB_STATE),
        'dt': dt.reshape(b, n, N_DIR, MB_HEADS), 'gates': gates,
    }


def rev(t):
    return jnp.flip(t, axis=1)


def gdn_mixer(f, a_log, dt_bias, s0):
    f32 = jnp.float32
    g = -jnp.exp(a_log.astype(f32)) * jax.nn.softplus((f['a'] + dt_bias).astype(f32))
    beta = jax.nn.sigmoid(f['beta'].astype(f32))
    q, k, v = f['q'], f['k'], f['v']
    o_f, s_f = gated_delta_chunked(q, k, v, g[:, :, 0], beta[:, :, 0], s0[0])
    o_b, s_b = gated_delta_chunked(rev(q), rev(k), rev(v), rev(g[:, :, 1]), rev(beta[:, :, 1]), s0[1])
    return o_f + rev(o_b), jnp.stack([s_f, s_b])


def ssd_mixer(f, a_log, dt_bias, s0):
    f32 = jnp.float32
    b, n = f['dt'].shape[:2]
    dt = jax.nn.softplus((f['dt'] + dt_bias).astype(f32)).reshape(b, n, N_DIR, MB_GROUPS, MB_HPG)
    a_neg = -jnp.exp(a_log.astype(f32)).reshape(N_DIR, MB_GROUPS, MB_HPG)
    xs, bm, cm = f['xs'], f['Bm'], f['Cm']
    y_f, s_f = ssd_chunked(xs, dt[:, :, 0], a_neg[0], bm, cm, s0[0])
    y_b, s_b = ssd_chunked(rev(xs), rev(dt[:, :, 1]), a_neg[1], rev(bm), rev(cm), s0[1])
    return y_f + rev(y_b), jnp.stack([s_f, s_b])


def mixer_out(f, o, y, gdn_norm_g, mb_d, mb_norm_g, w_branch_gdn, w_branch_mb, w_out):
    b, n = o.shape[:2]
    dt = f['gates'].dtype
    za = f['zA'].reshape(b, n, GDN_HEADS, GDN_DV)
    ya = (rmsnorm(o, gdn_norm_g) * jax.nn.silu(za)).astype(dt).reshape(b, n, GDN_V)
    yb = y + mb_d.reshape(MB_GROUPS, MB_HPG)[..., None] * f['xs']
    yb = yb.reshape(b, n, MB_GROUPS, MB_HPG * MB_HEADDIM) * jax.nn.silu(f['zB'].reshape(b, n, MB_GROUPS, MB_HPG * MB_HEADDIM))
    yb = rmsnorm(yb, mb_norm_g.reshape(MB_GROUPS, MB_HPG * MB_HEADDIM)).astype(dt).reshape(b, n, MB_INNER)
    ga, gb = jnp.split(f['gates'], 2, axis=-1)
    merged = jax.nn.sigmoid(ga) * (ya @ w_branch_gdn) + jax.nn.sigmoid(gb) * (yb @ w_branch_mb)
    return merged @ w_out


def setup_inputs(seed: int = 0) -> dict:
    key = jax.random.key(seed)
    ks = jax.random.split(key, 24)
    f32 = jnp.float32
    L, D = DEPTH, D_MODEL

    def nrm(k, shape, scale):
        return jax.random.normal(k, shape, f32) * scale

    def dt_bias_init(k, shape):
        dt = jnp.exp(jax.random.uniform(k, shape, f32, math.log(1e-3), math.log(1e-1)))
        return dt + jnp.log(-jnp.expm1(-dt))

    def a_log_init(k, shape):
        return jnp.log(jax.random.uniform(k, shape, f32, 1.0, 16.0))

    return {
        'x': nrm(ks[0], (BATCH, SEQ, D), 1.0),
        'c': nrm(ks[1], (BATCH, D), 1.0),
        'ctx': nrm(ks[2], (BATCH, CTX_LEN, D), 1.0),
        'c_ctx': nrm(ks[3], (D,), 1.0),
        'w_ada': nrm(ks[4], (L, D, 9 * D), D ** -0.5),
        'b_ada': nrm(ks[5], (L, 9 * D), 0.02),
        'norm_g': 1.0 + nrm(ks[6], (L, 3, D), 0.02),
        'ffn_w_gu': nrm(ks[7], (L, 2, D, 2 * D_FF), D ** -0.5),
        'ffn_w_down': nrm(ks[8], (L, 2, D_FF, D), D_FF ** -0.5),
        'w_in': nrm(ks[9], (L, D, D_IN), D ** -0.5),
        'gdn_conv_w': nrm(ks[10], (L, CONV_K, 2 * GDN_QK + GDN_V), CONV_K ** -0.5),
        'gdn_A_log': a_log_init(ks[11], (L, N_DIR, GDN_HEADS)),
        'gdn_dt_bias': dt_bias_init(ks[12], (L, N_DIR, GDN_HEADS)),
        'gdn_norm_g': 1.0 + nrm(ks[13], (L, GDN_DV), 0.02),
        'mb_conv_w': nrm(ks[14], (L, CONV_K, MB_XBC), CONV_K ** -0.5),
        'mb_conv_b': nrm(ks[15], (L, MB_XBC), 0.02),
        'mb_A_log': a_log_init(ks[16], (L, N_DIR, MB_HEADS)),
        'mb_dt_bias': dt_bias_init(ks[17], (L, N_DIR, MB_HEADS)),
        'mb_D': 1.0 + nrm(ks[18], (L, MB_HEADS), 0.02),
        'mb_norm_g': 1.0 + nrm(ks[19], (L, MB_INNER), 0.02),
        'w_branch_gdn': nrm(ks[20], (L, GDN_V, D), GDN_V ** -0.5),
        'w_branch_mb': nrm(ks[21], (L, MB_INNER, D), MB_INNER ** -0.5),
        'w_out': nrm(ks[22], (L, D, D), D ** -0.5),
        'final_g': 1.0 + nrm(ks[23], (D,), 0.02),
    }


def reference(x, c, ctx, c_ctx, w_ada, b_ada, norm_g, ffn_w_gu, ffn_w_down, w_in, gdn_conv_w, gdn_A_log,
              gdn_dt_bias, gdn_norm_g, mb_conv_w, mb_conv_b, mb_A_log, mb_dt_bias, mb_D, mb_norm_g,
              w_branch_gdn, w_branch_mb, w_out, final_g):
    b = x.shape[0]
    xc = ctx
    for i in range(DEPTH):
        last = i == DEPTH - 1
        m = jnp.split((jax.nn.silu(c) @ w_ada[i] + b_ada[i])[:, None, :], 9, axis=-1)
        mc = jnp.split(jax.nn.silu(c_ctx) @ w_ada[i] + b_ada[i], 9, axis=-1)

        x = ffn_sublayer(x, norm_g[i, 0], m[0], m[1], m[2], ffn_w_gu[i, 0], ffn_w_down[i, 0])
        xc = ffn_sublayer(xc, norm_g[i, 0], mc[0], mc[1], mc[2], ffn_w_gu[i, 0], ffn_w_down[i, 0])

        hc = rmsnorm(xc, norm_g[i, 1]) * (1.0 + mc[4]) + mc[3]
        h = rmsnorm(x, norm_g[i, 1]) * (1.0 + m[4]) + m[3]
        fc = mixer_features(hc, w_in[i], gdn_conv_w[i], mb_conv_w[i], mb_conv_b[i], dwconv)
        f = mixer_features(h, w_in[i], gdn_conv_w[i], mb_conv_w[i], mb_conv_b[i], grid_conv)
        s_gdn0 = jnp.zeros((N_DIR, b, GDN_HEADS, GDN_DK, GDN_DV), jnp.float32)
        s_mb0 = jnp.zeros((N_DIR, b, MB_GROUPS, MB_HPG, MB_HEADDIM, MB_STATE), jnp.float32)
        oc, s_gdn = gdn_mixer(fc, gdn_A_log[i], gdn_dt_bias[i], s_gdn0)
        yc, s_mb = ssd_mixer(fc, mb_A_log[i], mb_dt_bias[i], s_mb0)
        o, _ = gdn_mixer(f, gdn_A_log[i], gdn_dt_bias[i], s_gdn)
        y, _ = ssd_mixer(f, mb_A_log[i], mb_dt_bias[i], s_mb)
        x = x + m[5] * mixer_out(f, o, y, gdn_norm_g[i], mb_D[i], mb_norm_g[i],
                                 w_branch_gdn[i], w_branch_mb[i], w_out[i])

        x = ffn_sublayer(x, norm_g[i, 2], m[6], m[7], m[8], ffn_w_gu[i, 1], ffn_w_down[i, 1])
        if not last:
            xc = xc + mc[5] * mixer_out(fc, oc, yc, gdn_norm_g[i], mb_D[i], mb_norm_g[i],
                                        w_branch_gdn[i], w_branch_mb[i], w_out[i])
            xc = ffn_sublayer(xc, norm_g[i, 2], mc[6], mc[7], mc[8], ffn_w_gu[i, 1], ffn_w_down[i, 1])
    return rmsnorm(x, final_g)
```

```python
import functools

import jax
import jax.numpy as jnp
from jax import lax
from jax.experimental import pallas as pl
from jax.experimental.pallas import tpu as pltpu

F32 = jnp.float32
BF16 = jnp.bfloat16
HIGHEST = lax.Precision.HIGHEST

EPS = 1e-6
FFN_RES = 0.5
GRID_W = 64
CONV_K = 5
GDN_HEADS = 8
GDN_D = 128
MB_HEADDIM = 64
MB_GROUPS = 2
MB_STATE = 128
N_DIR = 2

LANE = 128
SCAN_CHUNK = 128
VMEM_LIMIT = 56 * 1024 * 1024


def _const_spec(shape):
    nd = len(shape)
    return pl.BlockSpec(shape, lambda *_: (0,) * nd, pipeline_mode=pl.Buffered(1))


def _silu(x):
    return x * jax.nn.sigmoid(x)


def _softplus(x):
    return jnp.maximum(x, 0.0) + jnp.log(1.0 + jnp.exp(-jnp.abs(x)))


def _bdot(a, b):
    return jnp.dot(a.astype(BF16), b.astype(BF16), preferred_element_type=F32)


def _bdot_nt(a, b):
    return lax.dot_general(a.astype(BF16), b.astype(BF16), (((1,), (1,)), ((), ())),
                           preferred_element_type=F32)


def _bdot_tn(a, b):
    return lax.dot_general(a.astype(BF16), b.astype(BF16), (((0,), (0,)), ((), ())),
                           preferred_element_type=F32)


def _modulated_norm(x, g, shift, scale):
    ms = jnp.mean(x * x, axis=-1, keepdims=True)
    return x * lax.rsqrt(ms + EPS) * g * (1.0 + scale) + shift


def _adaln_kernel(c_ref, w_ref, b_ref, o_ref):
    s = _silu(c_ref[...])
    o_ref[...] = jnp.dot(s, w_ref[...], preferred_element_type=F32, precision=HIGHEST) + b_ref[...]


def _adaln(cc, w, b, tn=1152):
    rows, d = cc.shape
    n = w.shape[1]
    return pl.pallas_call(
        _adaln_kernel,
        out_shape=jax.ShapeDtypeStruct((rows, n), F32),
        grid=(n // tn,),
        in_specs=[pl.BlockSpec((rows, d), lambda j: (0, 0)),
                  pl.BlockSpec((d, tn), lambda j: (0, j)),
                  pl.BlockSpec((1, tn), lambda j: (0, j))],
        out_specs=pl.BlockSpec((rows, tn), lambda j: (0, j)),
        compiler_params=pltpu.CompilerParams(vmem_limit_bytes=VMEM_LIMIT),
        name="adaln",
    )(cc, w, b)


def _ffn_kernel(x_ref, mod_ref, g_ref, wgu_ref, wd_ref, fg_ref, o_ref, *, k, d_ff, final):
    x = x_ref[...]
    shift = mod_ref[0, 3 * k:3 * k + 1, :]
    scale = mod_ref[0, 3 * k + 1:3 * k + 2, :]
    gate = mod_ref[0, 3 * k + 2:3 * k + 3, :]
    hb = _modulated_norm(x, g_ref[...], shift, scale).astype(BF16)
    gt = jnp.dot(hb, wgu_ref[:, :d_ff], preferred_element_type=F32)
    up = jnp.dot(hb, wgu_ref[:, d_ff:], preferred_element_type=F32)
    act = (_silu(gt) * up).astype(BF16)
    y = x + FFN_RES * gate * jnp.dot(act, wd_ref[...], preferred_element_type=F32)
    if final:
        ms = jnp.mean(y * y, axis=-1, keepdims=True)
        y = y * lax.rsqrt(ms + EPS) * fg_ref[...]
    o_ref[...] = y


def _ffn(xa, mod3, g, wgu, wd, fg, *, k, row_of_tile, tm, final, name):
    t, d = xa.shape
    d_ff = wd.shape[0]
    return pl.pallas_call(
        functools.partial(_ffn_kernel, k=k, d_ff=d_ff, final=final),
        out_shape=jax.ShapeDtypeStruct((t, d), F32),
        grid=(t // tm,),
        in_specs=[pl.BlockSpec((tm, d), lambda i: (i, 0)),
                  pl.BlockSpec((1,) + mod3.shape[1:], lambda i: (row_of_tile(i), 0, 0)),
                  _const_spec(g.shape), _const_spec(wgu.shape), _const_spec(wd.shape),
                  _const_spec(fg.shape)],
        out_specs=pl.BlockSpec((tm, d), lambda i: (i, 0)),
        compiler_params=pltpu.CompilerParams(vmem_limit_bytes=VMEM_LIMIT),
        name=name,
    )(xa, mod3, g, wgu, wd, fg)


def _inproj_kernel(x_ref, mod_ref, g_ref, w_ref, ws_ref, cp_ref, cg_ref, sg_ref, *,
                   tm, n_ctx_tiles, ctx_len, blocks):
    i = pl.program_id(0)
    x = x_ref[...]
    h = _modulated_norm(x, g_ref[...], mod_ref[0, 3:4, :], mod_ref[0, 4:5, :])
    hb = h.astype(BF16)
    sg_ref[...] = jnp.dot(h, ws_ref[...], preferred_element_type=F32, precision=HIGHEST)

    period_mask = jnp.where(i < n_ctx_tiles, ctx_len - 1, GRID_W - 1)
    period = period_mask + 1
    pos = lax.broadcasted_iota(jnp.int32, (tm, 1), 0) & period_mask
    half = CONV_K // 2
    valid = {}
    for dlt in range(-half, half + 1):
        if dlt:
            valid[dlt] = ((pos + dlt >= 0) & (pos + dlt < period)).astype(F32)

    for c0, c1, kind in blocks:
        u = jnp.dot(hb, w_ref[:, c0:c1], preferred_element_type=F32)
        acc = u * cp_ref[half:half + 1, c0:c1]
        for dlt in range(-half, half + 1):
            if dlt:
                sh = pltpu.roll(u, (-dlt) % tm, 0)
                acc = acc + (sh * valid[dlt]) * cp_ref[half + dlt:half + dlt + 1, c0:c1]
        if kind == "x":
            acc = acc + cp_ref[CONV_K:CONV_K + 1, c0:c1]
        y = _silu(acc)
        if kind in ("q", "k"):
            parts = []
            for hh in range((c1 - c0) // GDN_D):
                yh = y[:, hh * GDN_D:(hh + 1) * GDN_D]
                inv = lax.rsqrt(jnp.sum(yh * yh, axis=-1, keepdims=True) + EPS)
                if kind == "q":
                    inv = inv * (GDN_D ** -0.5)
                parts.append(yh * inv)
            y = jnp.concatenate(parts, axis=1)
        cg_ref[:, c0:c1] = y


def _inproj(x1, mod3, g, w_cg, w_small, conv_p, *, tm, n_ctx_tiles, ctx_len, seq, blocks):
    t, d = x1.shape
    ncg = w_cg.shape[1]

    def row_of_tile(i):
        return jnp.where(i < n_ctx_tiles, 0, 1 + ((i - n_ctx_tiles) * tm) // seq)

    return pl.pallas_call(
        functools.partial(_inproj_kernel, tm=tm, n_ctx_tiles=n_ctx_tiles, ctx_len=ctx_len, blocks=blocks),
        out_shape=(jax.ShapeDtypeStruct((t, ncg), F32), jax.ShapeDtypeStruct((t, LANE), F32)),
        grid=(t // tm,),
        in_specs=[pl.BlockSpec((tm, d), lambda i: (i, 0)),
                  pl.BlockSpec((1,) + mod3.shape[1:], lambda i: (row_of_tile(i), 0, 0)),
                  _const_spec(g.shape), _const_spec(w_cg.shape), _const_spec(w_small.shape),
                  _const_spec(conv_p.shape)],
        out_specs=(pl.BlockSpec((tm, ncg), lambda i: (i, 0)), pl.BlockSpec((tm, LANE), lambda i: (i, 0))),
        compiler_params=pltpu.CompilerParams(vmem_limit_bytes=VMEM_LIMIT),
        name="inproj",
    )(x1, mod3, g, w_cg, w_small, conv_p)


def _tri_masks(n, rev):
    ii = lax.broadcasted_iota(jnp.int32, (n, n), 0)
    jj = lax.broadcasted_iota(jnp.int32, (n, n), 1)
    if rev:
        return ii, jj, ii <= jj, ii < jj
    return ii, jj, ii >= jj, ii > jj


def _unit_tri_inverse(a, ii, jj):
    n = a.shape[0]
    lg = 3
    eye = (ii == jj).astype(F32)
    a8 = jnp.where((ii >> lg) == (jj >> lg), a, 0.0)
    x2 = _bdot(a8, a8)
    x4 = _bdot(x2, x2)
    t = eye - a8
    t = t + _bdot(t, x2)
    t = t + _bdot(t, x4)
    while (1 << lg) < n:
        off = ((ii >> (lg + 1)) == (jj >> (lg + 1))) & ((ii >> lg) != (jj >> lg))
        bs = jnp.where(off, a, 0.0)
        t = t - _bdot(t, _bdot(bs, t))
        lg += 1
    return t


def _scan_block_index(b, s, rev, n_batch, ncc, nlc):
    lc = jnp.maximum(s - ncc, 0)
    if rev:
        ctx_blk = b * ncc + (ncc - 1 - jnp.minimum(s, ncc - 1))
        lat_blk = n_batch * ncc + b * nlc + (nlc - 1 - lc)
    else:
        ctx_blk = b * ncc + jnp.minimum(s, ncc - 1)
        lat_blk = n_batch * ncc + b * nlc + lc
    return jnp.where(s < ncc, ctx_blk, lat_blk)


def _scan_out_index(b, s, rev, ncc, nlc):
    lc = jnp.maximum(s - ncc, 0)
    return b * nlc + ((nlc - 1 - lc) if rev else lc)


def _gdn_dir(q, k, v, sg, sgt, prow, pcol, state_ref, o_ref, d, rev):
    n = SCAN_CHUNK
    ii, jj, incl, strict = _tri_masks(n, rev)
    mtri = incl.astype(F32)
    mtri_t = (ii <= jj if not rev else ii >= jj).astype(F32)
    g_col = -jnp.exp(prow[0:1, :]) * _softplus(sg + prow[1:2, :])
    g_row = -jnp.exp(pcol[:, 0:1]) * _softplus(sgt + pcol[:, 1:2])
    gc_col = jnp.dot(mtri, g_col, preferred_element_type=F32, precision=HIGHEST)
    gc_row = jnp.dot(g_row, mtri_t, preferred_element_type=F32, precision=HIGHEST)
    beta_col = jax.nn.sigmoid(sg)
    last = 0 if rev else n - 1
    for hh in range(GDN_HEADS):
        ci = d * GDN_HEADS + hh
        lo, hi = hh * GDN_D, (hh + 1) * GDN_D
        qh, kh, vh = q[:, lo:hi], k[:, lo:hi], v[:, lo:hi]
        gcol = gc_col[:, ci:ci + 1]
        grow = gc_row[ci:ci + 1, :]
        bcol = beta_col[:, N_DIR * GDN_HEADS + ci:N_DIR * GDN_HEADS + ci + 1]
        g_last = gcol[last:last + 1, :]
        eg = jnp.exp(gcol)
        decay = jnp.where(incl, jnp.exp(jnp.where(incl, gcol - grow, 0.0)), 0.0)
        kb = kh * bcol
        kk = _bdot_nt(jnp.concatenate([kb, qh], axis=0), kh)
        a = jnp.where(strict, kk[:n] * decay, 0.0)
        qk = kk[n:] * decay
        t = _unit_tri_inverse(a, ii, jj)
        sol = _bdot(t, jnp.concatenate([vh * bcol, kb * eg], axis=1))
        u, w = sol[:, :GDN_D], sol[:, GDN_D:]
        st = state_ref[d, hh]
        ws = _bdot(jnp.concatenate([w, qh * eg], axis=0), st)
        v_new = u - ws[:n]
        o_ref[:, lo:hi] = ws[n:] + _bdot(qk, v_new)
        k_out = kh * jnp.exp(g_last - gcol)
        state_ref[d, hh] = st * jnp.exp(g_last) + _bdot_tn(k_out, v_new)


def _gdn_kernel(qf, kf, vf, sgf, sgtf, qb, kb, vb, sgb, sgtb, prow_ref, pcol_ref, of_ref, ob_ref, state_ref):
    @pl.when(pl.program_id(1) == 0)
    def _():
        state_ref[...] = jnp.zeros_like(state_ref)

    prow, pcol = prow_ref[...], pcol_ref[...]
    _gdn_dir(qf[...], kf[...], vf[...], sgf[...], sgtf[...], prow, pcol, state_ref, of_ref, 0, False)
    _gdn_dir(qb[...], kb[...], vb[...], sgb[...], sgtb[...], prow, pcol, state_ref, ob_ref, 1, True)


def _gdn_scan(cg, sg, sgt, prow, pcol, *, n_batch, ncc, nlc, col_q):
    n = SCAN_CHUNK
    dq = GDN_HEADS * GDN_D
    cq = col_q // dq

    def tok(rev, col):
        return pl.BlockSpec((n, dq), lambda b, s: (_scan_block_index(b, s, rev, n_batch, ncc, nlc), col))

    def small(rev):
        return pl.BlockSpec((n, LANE), lambda b, s: (_scan_block_index(b, s, rev, n_batch, ncc, nlc), 0))

    def small_t(rev):
        return pl.BlockSpec((LANE, n), lambda b, s: (0, _scan_block_index(b, s, rev, n_batch, ncc, nlc)))

    def out(rev):
        return pl.BlockSpec((n, dq), lambda b, s: (_scan_out_index(b, s, rev, ncc, nlc), 0))

    in_specs = []
    for rev in (False, True):
        in_specs += [tok(rev, cq), tok(rev, cq + 1), tok(rev, cq + 2), small(rev), small_t(rev)]
    in_specs += [_const_spec(prow.shape), _const_spec(pcol.shape)]
    t_lat = n_batch * nlc * n
    return pl.pallas_call(
        _gdn_kernel,
        out_shape=(jax.ShapeDtypeStruct((t_lat, dq), F32),) * 2,
        grid=(n_batch, ncc + nlc),
        in_specs=in_specs,
        out_specs=(out(False), out(True)),
        scratch_shapes=[pltpu.VMEM((N_DIR, GDN_HEADS, GDN_D, GDN_D), F32)],
        compiler_params=pltpu.CompilerParams(vmem_limit_bytes=VMEM_LIMIT,
                                             dimension_semantics=("arbitrary", "arbitrary")),
        name="gdn_scan",
    )(cg, cg, cg, sg, sgt, cg, cg, cg, sg, sgt, prow, pcol)


def _expand(x, e):
    x1 = x.astype(BF16)
    r1 = x - x1.astype(F32)
    x2 = r1.astype(BF16)
    x3 = (r1 - x2.astype(F32)).astype(BF16)
    dot = functools.partial(jnp.dot, preferred_element_type=F32)
    return dot(x1, e) + dot(x2, e) + dot(x3, e)


def _ssd_dir(xs, bm, cm, sg, sgt, prow, pcol, ex, state_ref, y_ref, d, rev):
    n = SCAN_CHUNK
    n_heads = xs.shape[1] // MB_HEADDIM
    hpg = n_heads // MB_GROUPS
    gw = hpg * MB_HEADDIM
    ii, jj, incl, _ = _tri_masks(n, rev)
    mtri = incl.astype(F32)
    mtri_t = (ii <= jj if not rev else ii >= jj).astype(F32)
    dt_col = _softplus(sg + prow[3:4, :])
    dt_row = _softplus(sgt + pcol[:, 3:4])
    acs_col = jnp.dot(mtri, dt_col * -jnp.exp(prow[2:3, :]), preferred_element_type=F32, precision=HIGHEST)
    acs_row = jnp.dot(dt_row * -jnp.exp(pcol[:, 2:3]), mtri_t, preferred_element_type=F32, precision=HIGHEST)
    last = 0 if rev else n - 1
    tot = acs_col[last:last + 1, :]
    e_in = _expand(dt_col * jnp.exp(tot - acs_col), ex)
    e_out = _expand(jnp.exp(acs_col), ex)
    e_tot = _expand(jnp.broadcast_to(jnp.exp(tot), (8, LANE)), ex)[0:1, :]
    xw = xs * e_in
    lane = lax.broadcasted_iota(jnp.int32, (n, 2 * MB_HEADDIM), 1)
    left = lane < MB_HEADDIM
    for g in range(MB_GROUPS):
        bg = bm[:, g * MB_STATE:(g + 1) * MB_STATE]
        cgm = cm[:, g * MB_STATE:(g + 1) * MB_STATE]
        cb = _bdot_nt(cgm, bg)
        st = state_ref[d, g]
        y_off = _bdot(cgm, st) * e_out[:, g * gw:(g + 1) * gw]
        state_ref[d, g] = st * e_tot[:, g * gw:(g + 1) * gw] + _bdot_tn(bg, xw[:, g * gw:(g + 1) * gw])
        for pr in range(hpg // 2):
            h0 = g * hpg + 2 * pr
            c0 = N_DIR * GDN_HEADS * 2 + d * n_heads + h0
            ms = []
            for hh in (0, 1):
                ccol = acs_col[:, c0 + hh:c0 + hh + 1]
                crow = acs_row[c0 + hh:c0 + hh + 1, :]
                seg = jnp.where(incl, jnp.exp(jnp.where(incl, ccol - crow, 0.0)), 0.0)
                ms.append(cb * seg * dt_row[c0 + hh:c0 + hh + 1, :])
            m2 = jnp.concatenate(ms, axis=1)
            lo = h0 * MB_HEADDIM
            xp = xs[:, lo:lo + 2 * MB_HEADDIM]
            xbd = jnp.concatenate([jnp.where(left, xp, 0.0), jnp.where(left, 0.0, xp)], axis=0)
            y_ref[:, lo:lo + 2 * MB_HEADDIM] = _bdot(m2, xbd) + y_off[:, lo - g * gw:lo - g * gw + 2 * MB_HEADDIM]


def _ssd_kernel(xf, bf, cf, sgf, sgtf, xb, bb, cb, sgb, sgtb, prow_ref, pcol_ref, ex_ref, yf_ref, yb_ref, state_ref):
    @pl.when(pl.program_id(1) == 0)
    def _():
        state_ref[...] = jnp.zeros_like(state_ref)

    prow, pcol = prow_ref[...], pcol_ref[...]
    _ssd_dir(xf[...], bf[...], cf[...], sgf[...], sgtf[...], prow, pcol, ex_ref[0], state_ref, yf_ref, 0, False)
    _ssd_dir(xb[...], bb[...], cb[...], sgb[...], sgtb[...], prow, pcol, ex_ref[1], state_ref, yb_ref, 1, True)


def _ssd_scan(cg, sg, sgt, prow, pcol, ex, *, n_batch, ncc, nlc, d_inner, col_b):
    n = SCAN_CHUNK
    bw = MB_GROUPS * MB_STATE
    cb_ = col_b // bw

    def blk(rev, width, col):
        return pl.BlockSpec((n, width), lambda b, s: (_scan_block_index(b, s, rev, n_batch, ncc, nlc), col))

    def small_t(rev):
        return pl.BlockSpec((LANE, n), lambda b, s: (0, _scan_block_index(b, s, rev, n_batch, ncc, nlc)))

    def out(rev):
        return pl.BlockSpec((n, d_inner), lambda b, s: (_scan_out_index(b, s, rev, ncc, nlc), 0))

    in_specs = []
    for rev in (False, True):
        in_specs += [blk(rev, d_inner, 0), blk(rev, bw, cb_), blk(rev, bw, cb_ + 1), blk(rev, LANE, 0), small_t(rev)]
    in_specs += [_const_spec(prow.shape), _const_spec(pcol.shape), _const_spec(ex.shape)]
    t_lat = n_batch * nlc * n
    return pl.pallas_call(
        _ssd_kernel,
        out_shape=(jax.ShapeDtypeStruct((t_lat, d_inner), F32),) * 2,
        grid=(n_batch, ncc + nlc),
        in_specs=in_specs,
        out_specs=(out(False), out(True)),
        scratch_shapes=[pltpu.VMEM((N_DIR, MB_GROUPS, MB_STATE, d_inner // MB_GROUPS), F32)],
        compiler_params=pltpu.CompilerParams(vmem_limit_bytes=VMEM_LIMIT,
                                             dimension_semantics=("arbitrary", "arbitrary")),
        name="ssd_scan",
    )(cg, cg, cg, sg, sgt, cg, cg, cg, sg, sgt, prow, pcol, ex)


def _mixout_kernel(x_ref, mod_ref, g_ref, wp_ref, of_ref, ob_ref, yf_ref, yb_ref, xs_ref,
                   rows_ref, wbg_ref, wbm_ref, wo_ref, o_ref, *, d_model, d_v, d_inner):
    x = x_ref[...]
    hb = _modulated_norm(x, g_ref[...], mod_ref[0, 3:4, :], mod_ref[0, 4:5, :]).astype(BF16)
    gate = mod_ref[0, 5:6, :]
    z = jnp.dot(hb, wp_ref[...], preferred_element_type=F32)
    za = z[:, :d_v]
    zb = z[:, d_v:d_v + d_inner]
    ga = z[:, d_v + d_inner:d_v + d_inner + d_model]
    gb = z[:, d_v + d_inner + d_model:]

    o = of_ref[...] + ob_ref[...]
    parts = []
    for hh in range(d_v // GDN_D):
        oh = o[:, hh * GDN_D:(hh + 1) * GDN_D]
        parts.append(oh * lax.rsqrt(jnp.mean(oh * oh, axis=-1, keepdims=True) + EPS))
    ya = jnp.concatenate(parts, axis=1) * rows_ref[0:1, :d_v] * _silu(za)

    yb = (yf_ref[...] + yb_ref[...] + rows_ref[1:2, :] * xs_ref[...]) * _silu(zb)
    gw = d_inner // MB_GROUPS
    parts = []
    for g in range(MB_GROUPS):
        yg = yb[:, g * gw:(g + 1) * gw]
        parts.append(yg * lax.rsqrt(jnp.mean(yg * yg, axis=-1, keepdims=True) + EPS))
    yb = jnp.concatenate(parts, axis=1) * rows_ref[2:3, :]

    merged = (jax.nn.sigmoid(ga) * _bdot(ya, wbg_ref[...]) + jax.nn.sigmoid(gb) * _bdot(yb, wbm_ref[...]))
    o_ref[...] = x + gate * _bdot(merged, wo_ref[...])


def _mixout(x1, mod3, g, w_plain, o_f, o_b, y_f, y_b, cg, rows, wbg, wbm, wo, *, tm, n_ctx_tiles, seq):
    t_lat, d_v = o_f.shape
    d_inner = y_f.shape[1]
    d = x1.shape[1]
    return pl.pallas_call(
        functools.partial(_mixout_kernel, d_model=d, d_v=d_v, d_inner=d_inner),
        out_shape=jax.ShapeDtypeStruct((t_lat, d), F32),
        grid=(t_lat // tm,),
        in_specs=[pl.BlockSpec((tm, d), lambda i: (i + n_ctx_tiles, 0)),
                  pl.BlockSpec((1,) + mod3.shape[1:], lambda i: (1 + (i * tm) // seq, 0, 0)),
                  _const_spec(g.shape), _const_spec(w_plain.shape),
                  pl.BlockSpec((tm, d_v), lambda i: (i, 0)), pl.BlockSpec((tm, d_v), lambda i: (i, 0)),
                  pl.BlockSpec((tm, d_inner), lambda i: (i, 0)), pl.BlockSpec((tm, d_inner), lambda i: (i, 0)),
                  pl.BlockSpec((tm, d_inner), lambda i: (i + n_ctx_tiles, 0)),
                  _const_spec(rows.shape), _const_spec(wbg.shape), _const_spec(wbm.shape), _const_spec(wo.shape)],
        out_specs=pl.BlockSpec((tm, d), lambda i: (i, 0)),
        compiler_params=pltpu.CompilerParams(vmem_limit_bytes=VMEM_LIMIT),
        name="mixout",
    )(x1, mod3, g, w_plain, o_f, o_b, y_f, y_b, cg, rows, wbg, wbm, wo)


def kernel(x, c, ctx, c_ctx, w_ada, b_ada, norm_g, ffn_w_gu, ffn_w_down, w_in, gdn_conv_w, gdn_A_log, gdn_dt_bias, gdn_norm_g, mb_conv_w, mb_conv_b, mb_A_log, mb_dt_bias, mb_D, mb_norm_g, w_branch_gdn, w_branch_mb, w_out, final_g):
    n_batch, seq, d = x.shape
    ctx_len = ctx.shape[1]
    assert w_ada.shape[0] == 1, "single-layer operation"
    d_qk = GDN_HEADS * GDN_D
    d_v = GDN_HEADS * GDN_D
    d_inner = mb_norm_g.shape[1]
    n_mb_heads = d_inner // MB_HEADDIM
    d_bc = MB_GROUPS * MB_STATE
    tm = 256
    t_ctx, t_lat = n_batch * ctx_len, n_batch * seq
    assert t_ctx % tm == 0 and seq % tm == 0 and tm % GRID_W == 0 and tm % ctx_len == 0
    assert ctx_len % SCAN_CHUNK == 0 and seq % SCAN_CHUNK == 0
    assert N_DIR * (2 * GDN_HEADS + n_mb_heads) <= LANE
    n_ctx_tiles = t_ctx // tm

    rows = 16
    cc = jnp.concatenate([c_ctx[None, :], c, jnp.zeros((rows - 1 - n_batch, d), F32)], axis=0)
    mod3 = _adaln(cc, w_ada[0], b_ada[0][None, :]).reshape(rows, 9, d)

    def row_all(i):
        return jnp.where(i < n_ctx_tiles, 0, 1 + ((i - n_ctx_tiles) * tm) // seq)

    xa = jnp.concatenate([ctx.reshape(t_ctx, d), x.reshape(t_lat, d)], axis=0)
    x1 = _ffn(xa, mod3, norm_g[0, 0][None, :], ffn_w_gu[0, 0].astype(BF16), ffn_w_down[0, 0].astype(BF16),
              final_g[None, :], k=0, row_of_tile=row_all, tm=tm, final=False, name="ffn1")

    sizes = (2 * d_qk + d_v, d_v, N_DIR * GDN_HEADS, N_DIR * GDN_HEADS, d_inner, d_inner + 2 * d_bc,
             N_DIR * n_mb_heads)
    o_qkv, o_za, o_a, o_beta, o_zb, o_xbc, o_dt, o_gates = [sum(sizes[:j]) for j in range(len(sizes) + 1)]
    wi = w_in[0]

    def regroup(a, off_qkv, off_xbc):
        return jnp.concatenate([a[..., off_xbc:off_xbc + d_inner], a[..., off_qkv:off_qkv + 2 * d_qk + d_v],
                                a[..., off_xbc + d_inner:off_xbc + d_inner + 2 * d_bc]], axis=-1)

    w_cg = regroup(wi, o_qkv, o_xbc).astype(BF16)
    n_small = N_DIR * (2 * GDN_HEADS + n_mb_heads)
    w_small = jnp.concatenate([wi[:, o_a:o_a + N_DIR * GDN_HEADS], wi[:, o_beta:o_beta + N_DIR * GDN_HEADS],
                               wi[:, o_dt:o_dt + N_DIR * n_mb_heads], jnp.zeros((d, LANE - n_small), F32)], axis=1)
    conv_all = jnp.concatenate([gdn_conv_w[0], mb_conv_w[0]], axis=1)
    bias_all = jnp.concatenate([jnp.zeros((1, 2 * d_qk + d_v), F32), mb_conv_b[0][None, :]], axis=1)
    conv_p = regroup(jnp.concatenate([conv_all, bias_all, jnp.zeros((2, conv_all.shape[1]), F32)], axis=0),
                     0, 2 * d_qk + d_v)
    col_q = d_inner
    col_b = d_inner + 2 * d_qk + d_v
    blocks = ((0, d_inner // 2, "x"), (d_inner // 2, d_inner, "x"), (col_q, col_q + d_qk, "q"),
              (col_q + d_qk, col_q + 2 * d_qk, "k"), (col_q + 2 * d_qk, col_b, "v"), (col_b, col_b + 2 * d_bc, "x"))
    cg, sg = _inproj(x1, mod3, norm_g[0, 1][None, :], w_cg, w_small, conv_p, tm=tm, n_ctx_tiles=n_ctx_tiles,
                     ctx_len=ctx_len, seq=seq, blocks=blocks)
    sgt = sg.T

    pad = jnp.zeros((LANE - n_small,), F32)
    zeros_g = jnp.zeros((N_DIR * GDN_HEADS,), F32)
    zeros_m = jnp.zeros((N_DIR * n_mb_heads,), F32)
    prow = jnp.stack([jnp.concatenate([gdn_A_log[0].reshape(-1), zeros_g, zeros_m, pad]),
                      jnp.concatenate([gdn_dt_bias[0].reshape(-1), zeros_g, zeros_m, pad]),
                      jnp.concatenate([zeros_g, zeros_g, mb_A_log[0].reshape(-1), pad]),
                      jnp.concatenate([zeros_g, zeros_g, mb_dt_bias[0].reshape(-1), pad])]
                     + [jnp.zeros((LANE,), F32)] * 4)
    pcol = prow.T
    ncc, nlc = ctx_len // SCAN_CHUNK, seq // SCAN_CHUNK
    o_f, o_b = _gdn_scan(cg, sg, sgt, prow, pcol, n_batch=n_batch, ncc=ncc, nlc=nlc, col_q=col_q)

    lane_head = jnp.arange(d_inner) // MB_HEADDIM
    ex = jnp.stack([(jnp.arange(LANE)[:, None] == (2 * N_DIR * GDN_HEADS + dd * n_mb_heads + lane_head)[None, :])
                    for dd in range(N_DIR)]).astype(BF16)
    y_f, y_b = _ssd_scan(cg, sg, sgt, prow, pcol, ex, n_batch=n_batch, ncc=ncc, nlc=nlc, d_inner=d_inner, col_b=col_b)

    w_plain = jnp.concatenate([wi[:, o_za:o_za + d_v], wi[:, o_zb:o_zb + d_inner], wi[:, o_gates:o_gates + 2 * d]],
                              axis=1).astype(BF16)
    rows3 = jnp.stack([jnp.concatenate([jnp.tile(gdn_norm_g[0], GDN_HEADS), jnp.zeros((d_inner - d_v,), F32)]),
                       jnp.repeat(mb_D[0], MB_HEADDIM), mb_norm_g[0]] + [jnp.zeros((d_inner,), F32)] * 5)
    x2 = _mixout(x1, mod3, norm_g[0, 1][None, :], w_plain, o_f, o_b, y_f, y_b, cg, rows3,
                 w_branch_gdn[0].astype(BF16), w_branch_mb[0].astype(BF16), w_out[0].astype(BF16),
                 tm=tm, n_ctx_tiles=n_ctx_tiles, seq=seq)

    out = _ffn(x2, mod3, norm_g[0, 2][None, :], ffn_w_gu[0, 1].astype(BF16), ffn_w_down[0, 1].astype(BF16),
               final_g[None, :], k=2, row_of_tile=lambda i: 1 + (i * tm) // seq, tm=tm, final=True, name="ffn2")
    return out.reshape(n_batch, seq, d)
```

```python
import functools

import jax
import jax.numpy as jnp
from jax import lax
from jax.experimental import pallas as pl
from jax.experimental.pallas import tpu as pltpu

F32 = jnp.float32
BF16 = jnp.bfloat16
HIGHEST = lax.Precision.HIGHEST

EPS = 1e-6
FFN_RES = 0.5
GRID_W = 64
CONV_K = 5
GDN_HEADS = 8
GDN_D = 128
MB_HEADDIM = 64
MB_GROUPS = 2
MB_STATE = 128
N_DIR = 2

LANE = 128
SCAN_CHUNK = 128
VMEM_LIMIT = 56 * 1024 * 1024


def _const_spec(shape):
    nd = len(shape)
    return pl.BlockSpec(shape, lambda *_: (0,) * nd, pipeline_mode=pl.Buffered(1))


def _silu(x):
    return x * jax.nn.sigmoid(x)


def _softplus(x):
    return jnp.maximum(x, 0.0) + jnp.log(1.0 + jnp.exp(-jnp.abs(x)))


def _bdot(a, b):
    return jnp.dot(a.astype(BF16), b.astype(BF16), preferred_element_type=F32)


def _bdot_nt(a, b):
    return lax.dot_general(a.astype(BF16), b.astype(BF16), (((1,), (1,)), ((), ())),
                           preferred_element_type=F32)


def _bdot_tn(a, b):
    return lax.dot_general(a.astype(BF16), b.astype(BF16), (((0,), (0,)), ((), ())),
                           preferred_element_type=F32)


def _modulated_norm(x, g, shift, scale):
    ms = jnp.mean(x * x, axis=-1, keepdims=True)
    return x * lax.rsqrt(ms + EPS) * g * (1.0 + scale) + shift


def _adaln_kernel(c_ref, w_ref, b_ref, o_ref):
    s = _silu(c_ref[...])
    o_ref[...] = jnp.dot(s, w_ref[...], preferred_element_type=F32, precision=HIGHEST) + b_ref[...]


def _adaln(cc, w, b, tn=1152):
    rows, d = cc.shape
    n = w.shape[1]
    return pl.pallas_call(
        _adaln_kernel,
        out_shape=jax.ShapeDtypeStruct((rows, n), F32),
        grid=(n // tn,),
        in_specs=[pl.BlockSpec((rows, d), lambda j: (0, 0)),
                  pl.BlockSpec((d, tn), lambda j: (0, j)),
                  pl.BlockSpec((1, tn), lambda j: (0, j))],
        out_specs=pl.BlockSpec((rows, tn), lambda j: (0, j)),
        compiler_params=pltpu.CompilerParams(vmem_limit_bytes=VMEM_LIMIT),
        name="adaln",
    )(cc, w, b)


def _ffn_kernel(x_ref, mod_ref, g_ref, wgu_ref, wd_ref, fg_ref, o_ref, *, k, d_ff, final):
    x = x_ref[...]
    shift = mod_ref[0, 3 * k:3 * k + 1, :]
    scale = mod_ref[0, 3 * k + 1:3 * k + 2, :]
    gate = mod_ref[0, 3 * k + 2:3 * k + 3, :]
    hb = _modulated_norm(x, g_ref[...], shift, scale).astype(BF16)
    gt = jnp.dot(hb, wgu_ref[:, :d_ff], preferred_element_type=F32)
    up = jnp.dot(hb, wgu_ref[:, d_ff:], preferred_element_type=F32)
    act = (_silu(gt) * up).astype(BF16)
    y = x + FFN_RES * gate * jnp.dot(act, wd_ref[...], preferred_element_type=F32)
    if final:
        ms = jnp.mean(y * y, axis=-1, keepdims=True)
        y = y * lax.rsqrt(ms + EPS) * fg_ref[...]
    o_ref[...] = y


def _ffn(xa, mod3, g, wgu, wd, fg, *, k, row_of_tile, tm, final, name):
    t, d = xa.shape
    d_ff = wd.shape[0]
    return pl.pallas_call(
        functools.partial(_ffn_kernel, k=k, d_ff=d_ff, final=final),
        out_shape=jax.ShapeDtypeStruct((t, d), F32),
        grid=(t // tm,),
        in_specs=[pl.BlockSpec((tm, d), lambda i: (i, 0)),
                  pl.BlockSpec((1,) + mod3.shape[1:], lambda i: (row_of_tile(i), 0, 0)),
                  _const_spec(g.shape), _const_spec(wgu.shape), _const_spec(wd.shape),
                  _const_spec(fg.shape)],
        out_specs=pl.BlockSpec((tm, d), lambda i: (i, 0)),
        compiler_params=pltpu.CompilerParams(vmem_limit_bytes=VMEM_LIMIT),
        name=name,
    )(xa, mod3, g, wgu, wd, fg)


def _inproj_kernel(x_ref, mod_ref, g_ref, w_ref, ws_ref, cp_ref, cg_ref, sg_ref, *,
                   tm, n_ctx_tiles, ctx_len, blocks):
    i = pl.program_id(0)
    x = x_ref[...]
    h = _modulated_norm(x, g_ref[...], mod_ref[0, 3:4, :], mod_ref[0, 4:5, :])
    hb = h.astype(BF16)
    sg_ref[...] = jnp.dot(h, ws_ref[...], preferred_element_type=F32, precision=HIGHEST)

    period_mask = jnp.where(i < n_ctx_tiles, ctx_len - 1, GRID_W - 1)
    period = period_mask + 1
    pos = lax.broadcasted_iota(jnp.int32, (tm, 1), 0) & period_mask
    half = CONV_K // 2
    valid = {}
    for dlt in range(-half, half + 1):
        if dlt:
            valid[dlt] = ((pos + dlt >= 0) & (pos + dlt < period)).astype(F32)

    for c0, c1, kind in blocks:
        u = jnp.dot(hb, w_ref[:, c0:c1], preferred_element_type=F32)
        acc = u * cp_ref[half:half + 1, c0:c1]
        for dlt in range(-half, half + 1):
            if dlt:
                sh = pltpu.roll(u, (-dlt) % tm, 0)
                acc = acc + (sh * valid[dlt]) * cp_ref[half + dlt:half + dlt + 1, c0:c1]
        if kind == "x":
            acc = acc + cp_ref[CONV_K:CONV_K + 1, c0:c1]
        y = _silu(acc)
        if kind in ("q", "k"):
            parts = []
            for hh in range((c1 - c0) // GDN_D):
                yh = y[:, hh * GDN_D:(hh + 1) * GDN_D]
                inv = lax.rsqrt(jnp.sum(yh * yh, axis=-1, keepdims=True) + EPS)
                if kind == "q":
                    inv = inv * (GDN_D ** -0.5)
                parts.append(yh * inv)
            y = jnp.concatenate(parts, axis=1)
        cg_ref[:, c0:c1] = y


def _inproj(x1, mod3, g, w_cg, w_small, conv_p, *, tm, n_ctx_tiles, ctx_len, seq, blocks):
    t, d = x1.shape
    ncg = w_cg.shape[1]

    def row_of_tile(i):
        return jnp.where(i < n_ctx_tiles, 0, 1 + ((i - n_ctx_tiles) * tm) // seq)

    return pl.pallas_call(
        functools.partial(_inproj_kernel, tm=tm, n_ctx_tiles=n_ctx_tiles, ctx_len=ctx_len, blocks=blocks),
        out_shape=(jax.ShapeDtypeStruct((t, ncg), F32), jax.ShapeDtypeStruct((t, LANE), F32)),
        grid=(t // tm,),
        in_specs=[pl.BlockSpec((tm, d), lambda i: (i, 0)),
                  pl.BlockSpec((1,) + mod3.shape[1:], lambda i: (row_of_tile(i), 0, 0)),
                  _const_spec(g.shape), _const_spec(w_cg.shape), _const_spec(w_small.shape),
                  _const_spec(conv_p.shape)],
        out_specs=(pl.BlockSpec((tm, ncg), lambda i: (i, 0)), pl.BlockSpec((tm, LANE), lambda i: (i, 0))),
        compiler_params=pltpu.CompilerParams(vmem_limit_bytes=VMEM_LIMIT),
        name="inproj",
    )(x1, mod3, g, w_cg, w_small, conv_p)


def _tri_masks(n, rev):
    ii = lax.broadcasted_iota(jnp.int32, (n, n), 0)
    jj = lax.broadcasted_iota(jnp.int32, (n, n), 1)
    if rev:
        return ii, jj, ii <= jj, ii < jj
    return ii, jj, ii >= jj, ii > jj


def _scan_block_index(b, s, rev, n_batch, ncc, nlc):
    lc = jnp.maximum(s - ncc, 0)
    if rev:
        ctx_blk = b * ncc + (ncc - 1 - jnp.minimum(s, ncc - 1))
        lat_blk = n_batch * ncc + b * nlc + (nlc - 1 - lc)
    else:
        ctx_blk = b * ncc + jnp.minimum(s, ncc - 1)
        lat_blk = n_batch * ncc + b * nlc + lc
    return jnp.where(s < ncc, ctx_blk, lat_blk)


def _scan_out_index(b, s, rev, ncc, nlc):
    lc = jnp.maximum(s - ncc, 0)
    return b * nlc + ((nlc - 1 - lc) if rev else lc)


def _gdn_prep_kernel(q_ref, k_ref, v_ref, sg_ref, sgt_ref, prow_ref, pcol_ref,
                     uf, wf, qkf, qif, kof, ub, wb, qkb, qib, kob, dec_ref):
    n = SCAN_CHUNK
    q, k, v, sg, sgt = q_ref[...], k_ref[...], v_ref[...], sg_ref[...], sgt_ref[...]
    prow, pcol = prow_ref[...], pcol_ref[...]
    outs = ((uf, wf, qkf, qif, kof), (ub, wb, qkb, qib, kob))
    g_col = -jnp.exp(prow[0:1, :]) * _softplus(sg + prow[1:2, :])
    g_row = -jnp.exp(pcol[:, 0:1]) * _softplus(sgt + pcol[:, 1:2])
    beta_col = jax.nn.sigmoid(sg)
    probs = []
    for d, rev in ((0, False), (1, True)):
        ii, jj, incl, strict = _tri_masks(n, rev)
        mtri = incl.astype(F32)
        mtri_t = (ii >= jj if rev else ii <= jj).astype(F32)
        gc_col = jnp.dot(mtri, g_col, preferred_element_type=F32, precision=HIGHEST)
        gc_row = jnp.dot(g_row, mtri_t, preferred_element_type=F32, precision=HIGHEST)
        last = 0 if rev else n - 1
        tot_rows = jnp.exp(jnp.broadcast_to(gc_row[:, last:last + 1], (LANE, n)))
        dec_ref[0, d * GDN_HEADS:(d + 1) * GDN_HEADS, :] = tot_rows[d * GDN_HEADS:(d + 1) * GDN_HEADS, :]
        for hh in range(GDN_HEADS):
            ci = d * GDN_HEADS + hh
            lo, hi = hh * GDN_D, (hh + 1) * GDN_D
            qh, kh, vh = q[:, lo:hi], k[:, lo:hi], v[:, lo:hi]
            gcol = gc_col[:, ci:ci + 1]
            grow = gc_row[ci:ci + 1, :]
            bcol = beta_col[:, N_DIR * GDN_HEADS + ci:N_DIR * GDN_HEADS + ci + 1]
            eg = jnp.exp(gcol)
            decay = jnp.where(incl, jnp.exp(jnp.where(incl, gcol - grow, 0.0)), 0.0)
            kb = kh * bcol
            o_u, o_w, o_qk, o_qi, o_ko = outs[d]
            o_qi[:, lo:hi] = (qh * eg).astype(BF16)
            o_ko[:, lo:hi] = (kh * jnp.exp(gcol[last:last + 1, :] - gcol)).astype(BF16)
            probs.append(dict(d=d, lo=lo, hi=hi, ii=ii, jj=jj, strict=strict, decay=decay,
                              lhs=jnp.concatenate([kb, qh], axis=0).astype(BF16), kh=kh.astype(BF16),
                              rhs=jnp.concatenate([vh * bcol, kb * eg], axis=1).astype(BF16)))
    kk = [_bdot_nt(p["lhs"], p["kh"]) for p in probs]
    for p, kkp in zip(probs, kk):
        p["a"] = jnp.where(p["strict"], kkp[:n] * p["decay"], 0.0)
        outs[p["d"]][2][:, p["lo"]:p["hi"]] = (kkp[n:] * p["decay"]).astype(BF16)
    lg = 3
    for p in probs:
        blk = (p["ii"] >> lg) == (p["jj"] >> lg)
        p["a8"] = jnp.where(blk, p["a"], 0.0).astype(BF16)
        p["t"] = (p["ii"] == p["jj"]).astype(F32) - jnp.where(blk, p["a"], 0.0)
    x2 = [_bdot(p["a8"], p["a8"]).astype(BF16) for p in probs]
    x4 = [_bdot(x, x).astype(BF16) for x in x2]
    t1 = [p["t"] + _bdot(p["t"], x) for p, x in zip(probs, x2)]
    ts = [t + _bdot(t, x) for t, x in zip(t1, x4)]
    while (1 << lg) < n:
        bs = []
        for p in probs:
            off = ((p["ii"] >> (lg + 1)) == (p["jj"] >> (lg + 1))) & ((p["ii"] >> lg) != (p["jj"] >> lg))
            bs.append(jnp.where(off, p["a"], 0.0).astype(BF16))
        tb = [t.astype(BF16) for t in ts]
        bt = [_bdot(b, t) for b, t in zip(bs, tb)]
        ts = [t - _bdot(t16, x) for t, t16, x in zip(ts, tb, bt)]
        lg += 1
    sol = [_bdot(t, p["rhs"]) for t, p in zip(ts, probs)]
    for p, s in zip(probs, sol):
        o_u, o_w = outs[p["d"]][0], outs[p["d"]][1]
        o_u[:, p["lo"]:p["hi"]] = s[:, :GDN_D]
        o_w[:, p["lo"]:p["hi"]] = s[:, GDN_D:].astype(BF16)


def _gdn_prep(cg, sg, sgt, prow, pcol, *, col_q):
    n = SCAN_CHUNK
    t = cg.shape[0]
    dq = GDN_HEADS * GDN_D
    cq = col_q // dq
    tok = lambda col: pl.BlockSpec((n, dq), lambda i: (i, col))
    wide = jax.ShapeDtypeStruct((t, dq), F32)
    half = jax.ShapeDtypeStruct((t, dq), BF16)
    return pl.pallas_call(
        _gdn_prep_kernel,
        out_shape=(wide, half, half, half, half) * 2
                  + (jax.ShapeDtypeStruct((t // n, N_DIR * GDN_HEADS, n), F32),),
        grid=(t // n,),
        in_specs=[tok(cq), tok(cq + 1), tok(cq + 2), pl.BlockSpec((n, LANE), lambda i: (i, 0)),
                  pl.BlockSpec((LANE, n), lambda i: (0, i)), _const_spec(prow.shape), _const_spec(pcol.shape)],
        out_specs=(tok(0),) * 10 + (pl.BlockSpec((1, N_DIR * GDN_HEADS, n), lambda i: (i, 0, 0)),),
        compiler_params=pltpu.CompilerParams(vmem_limit_bytes=VMEM_LIMIT),
        name="gdn_prep",
    )(cg, cg, cg, sg, sgt, prow, pcol)


def _gdn_kernel(uf, wf, qkf, qif, kof, decf, ub, wb, qkb, qib, kob, decb, of_ref, ob_ref, state_ref):
    @pl.when(pl.program_id(1) == 0)
    def _():
        state_ref[...] = jnp.zeros_like(state_ref)

    n = SCAN_CHUNK
    ins = ((uf, wf, qkf, qif, kof, decf, of_ref), (ub, wb, qkb, qib, kob, decb, ob_ref))
    probs = [(d, hh) for d in range(N_DIR) for hh in range(GDN_HEADS)]
    sl = lambda hh: slice(hh * GDN_D, (hh + 1) * GDN_D)
    st = [state_ref[d, hh] for d, hh in probs]
    ws = [jnp.dot(jnp.concatenate([ins[d][1][:, sl(hh)], ins[d][3][:, sl(hh)]], axis=0), s.astype(BF16),
                  preferred_element_type=F32) for (d, hh), s in zip(probs, st)]
    v_new = [(ins[d][0][:, sl(hh)] - w[:n]).astype(BF16) for (d, hh), w in zip(probs, ws)]
    for (d, hh), s, w, vn in zip(probs, st, ws, v_new):
        ins[d][6][:, sl(hh)] = w[n:] + jnp.dot(ins[d][2][:, sl(hh)], vn, preferred_element_type=F32)
        ci = d * GDN_HEADS + hh
        state_ref[d, hh] = s * ins[d][5][0, ci:ci + 1, :] + lax.dot_general(
            ins[d][4][:, sl(hh)], vn, (((0,), (0,)), ((), ())), preferred_element_type=F32)


def _gdn_scan(prep, *, n_batch, ncc, nlc):
    n = SCAN_CHUNK
    dq = GDN_HEADS * GDN_D
    dec = prep[-1]

    def tok(rev):
        return pl.BlockSpec((n, dq), lambda b, s: (_scan_block_index(b, s, rev, n_batch, ncc, nlc), 0))

    def dec_spec(rev):
        return pl.BlockSpec((1,) + dec.shape[1:], lambda b, s: (_scan_block_index(b, s, rev, n_batch, ncc, nlc), 0, 0))

    def out(rev):
        return pl.BlockSpec((n, dq), lambda b, s: (_scan_out_index(b, s, rev, ncc, nlc), 0))

    in_specs = [tok(False)] * 5 + [dec_spec(False)] + [tok(True)] * 5 + [dec_spec(True)]
    t_lat = n_batch * nlc * n
    return pl.pallas_call(
        _gdn_kernel,
        out_shape=(jax.ShapeDtypeStruct((t_lat, dq), F32),) * 2,
        grid=(n_batch, ncc + nlc),
        in_specs=in_specs,
        out_specs=(out(False), out(True)),
        scratch_shapes=[pltpu.VMEM((N_DIR, GDN_HEADS, GDN_D, GDN_D), F32)],
        compiler_params=pltpu.CompilerParams(vmem_limit_bytes=VMEM_LIMIT,
                                             dimension_semantics=("arbitrary", "arbitrary")),
        name="gdn_scan",
    )(*prep[0:5], dec, *prep[5:10], dec)


def _expand(x, e):
    x1 = x.astype(BF16)
    r1 = x - x1.astype(F32)
    x2 = r1.astype(BF16)
    x3 = (r1 - x2.astype(F32)).astype(BF16)
    dot = functools.partial(jnp.dot, preferred_element_type=F32)
    return dot(x1, e) + dot(x2, e) + dot(x3, e)


def _ssd_dir(xs, bm, cm, sg, sgt, prow, pcol, ex, state_ref, y_ref, d, rev):
    n = SCAN_CHUNK
    n_heads = xs.shape[1] // MB_HEADDIM
    hpg = n_heads // MB_GROUPS
    gw = hpg * MB_HEADDIM
    ii, jj, incl, _ = _tri_masks(n, rev)
    mtri = incl.astype(F32)
    mtri_t = (ii <= jj if not rev else ii >= jj).astype(F32)
    dt_col = _softplus(sg + prow[3:4, :])
    dt_row = _softplus(sgt + pcol[:, 3:4])
    acs_col = jnp.dot(mtri, dt_col * -jnp.exp(prow[2:3, :]), preferred_element_type=F32, precision=HIGHEST)
    acs_row = jnp.dot(dt_row * -jnp.exp(pcol[:, 2:3]), mtri_t, preferred_element_type=F32, precision=HIGHEST)
    last = 0 if rev else n - 1
    tot = acs_col[last:last + 1, :]
    e_in = _expand(dt_col * jnp.exp(tot - acs_col), ex)
    e_out = _expand(jnp.exp(acs_col), ex)
    e_tot = _expand(jnp.broadcast_to(jnp.exp(tot), (8, LANE)), ex)[0:1, :]
    xw = xs * e_in
    lane = lax.broadcasted_iota(jnp.int32, (n, 2 * MB_HEADDIM), 1)
    left = lane < MB_HEADDIM
    for g in range(MB_GROUPS):
        bg = bm[:, g * MB_STATE:(g + 1) * MB_STATE]
        cgm = cm[:, g * MB_STATE:(g + 1) * MB_STATE]
        cb = _bdot_nt(cgm, bg)
        st = state_ref[d, g]
        y_off = _bdot(cgm, st) * e_out[:, g * gw:(g + 1) * gw]
        state_ref[d, g] = st * e_tot[:, g * gw:(g + 1) * gw] + _bdot_tn(bg, xw[:, g * gw:(g + 1) * gw])
        for pr in range(hpg // 2):
            h0 = g * hpg + 2 * pr
            c0 = N_DIR * GDN_HEADS * 2 + d * n_heads + h0
            ms = []
            for hh in (0, 1):
                ccol = acs_col[:, c0 + hh:c0 + hh + 1]
                crow = acs_row[c0 + hh:c0 + hh + 1, :]
                seg = jnp.where(incl, jnp.exp(jnp.where(incl, ccol - crow, 0.0)), 0.0)
                ms.append(cb * seg * dt_row[c0 + hh:c0 + hh + 1, :])
            m2 = jnp.concatenate(ms, axis=1)
            lo = h0 * MB_HEADDIM
            xp = xs[:, lo:lo + 2 * MB_HEADDIM]
            xbd = jnp.concatenate([jnp.where(left, xp, 0.0), jnp.where(left, 0.0, xp)], axis=0)
            y_ref[:, lo:lo + 2 * MB_HEADDIM] = _bdot(m2, xbd) + y_off[:, lo - g * gw:lo - g * gw + 2 * MB_HEADDIM]


def _ssd_kernel(xf, bf, cf, sgf, sgtf, xb, bb, cb, sgb, sgtb, prow_ref, pcol_ref, ex_ref, yf_ref, yb_ref, state_ref):
    @pl.when(pl.program_id(1) == 0)
    def _():
        state_ref[...] = jnp.zeros_like(state_ref)

    prow, pcol = prow_ref[...], pcol_ref[...]
    _ssd_dir(xf[...], bf[...], cf[...], sgf[...], sgtf[...], prow, pcol, ex_ref[0], state_ref, yf_ref, 0, False)
    _ssd_dir(xb[...], bb[...], cb[...], sgb[...], sgtb[...], prow, pcol, ex_ref[1], state_ref, yb_ref, 1, True)


def _ssd_scan(cg, sg, sgt, prow, pcol, ex, *, n_batch, ncc, nlc, d_inner, col_b):
    n = SCAN_CHUNK
    bw = MB_GROUPS * MB_STATE
    cb_ = col_b // bw

    def blk(rev, width, col):
        return pl.BlockSpec((n, width), lambda b, s: (_scan_block_index(b, s, rev, n_batch, ncc, nlc), col))

    def small_t(rev):
        return pl.BlockSpec((LANE, n), lambda b, s: (0, _scan_block_index(b, s, rev, n_batch, ncc, nlc)))

    def out(rev):
        return pl.BlockSpec((n, d_inner), lambda b, s: (_scan_out_index(b, s, rev, ncc, nlc), 0))

    in_specs = []
    for rev in (False, True):
        in_specs += [blk(rev, d_inner, 0), blk(rev, bw, cb_), blk(rev, bw, cb_ + 1), blk(rev, LANE, 0), small_t(rev)]
    in_specs += [_const_spec(prow.shape), _const_spec(pcol.shape), _const_spec(ex.shape)]
    t_lat = n_batch * nlc * n
    return pl.pallas_call(
        _ssd_kernel,
        out_shape=(jax.ShapeDtypeStruct((t_lat, d_inner), F32),) * 2,
        grid=(n_batch, ncc + nlc),
        in_specs=in_specs,
        out_specs=(out(False), out(True)),
        scratch_shapes=[pltpu.VMEM((N_DIR, MB_GROUPS, MB_STATE, d_inner // MB_GROUPS), F32)],
        compiler_params=pltpu.CompilerParams(vmem_limit_bytes=VMEM_LIMIT,
                                             dimension_semantics=("arbitrary", "arbitrary")),
        name="ssd_scan",
    )(cg, cg, cg, sg, sgt, cg, cg, cg, sg, sgt, prow, pcol, ex)


def _mixout_kernel(x_ref, mod_ref, g_ref, wp_ref, of_ref, ob_ref, yf_ref, yb_ref, xs_ref,
                   rows_ref, wbg_ref, wbm_ref, wo_ref, o_ref, *, d_model, d_v, d_inner):
    x = x_ref[...]
    hb = _modulated_norm(x, g_ref[...], mod_ref[0, 3:4, :], mod_ref[0, 4:5, :]).astype(BF16)
    gate = mod_ref[0, 5:6, :]
    z = jnp.dot(hb, wp_ref[...], preferred_element_type=F32)
    za = z[:, :d_v]
    zb = z[:, d_v:d_v + d_inner]
    ga = z[:, d_v + d_inner:d_v + d_inner + d_model]
    gb = z[:, d_v + d_inner + d_model:]

    o = of_ref[...] + ob_ref[...]
    parts = []
    for hh in range(d_v // GDN_D):
        oh = o[:, hh * GDN_D:(hh + 1) * GDN_D]
        parts.append(oh * lax.rsqrt(jnp.mean(oh * oh, axis=-1, keepdims=True) + EPS))
    ya = jnp.concatenate(parts, axis=1) * rows_ref[0:1, :d_v] * _silu(za)

    yb = (yf_ref[...] + yb_ref[...] + rows_ref[1:2, :] * xs_ref[...]) * _silu(zb)
    gw = d_inner // MB_GROUPS
    parts = []
    for g in range(MB_GROUPS):
        yg = yb[:, g * gw:(g + 1) * gw]
        parts.append(yg * lax.rsqrt(jnp.mean(yg * yg, axis=-1, keepdims=True) + EPS))
    yb = jnp.concatenate(parts, axis=1) * rows_ref[2:3, :]

    merged = (jax.nn.sigmoid(ga) * _bdot(ya, wbg_ref[...]) + jax.nn.sigmoid(gb) * _bdot(yb, wbm_ref[...]))
    o_ref[...] = x + gate * _bdot(merged, wo_ref[...])


def _mixout(x1, mod3, g, w_plain, o_f, o_b, y_f, y_b, cg, rows, wbg, wbm, wo, *, tm, n_ctx_tiles, seq):
    t_lat, d_v = o_f.shape
    d_inner = y_f.shape[1]
    d = x1.shape[1]
    return pl.pallas_call(
        functools.partial(_mixout_kernel, d_model=d, d_v=d_v, d_inner=d_inner),
        out_shape=jax.ShapeDtypeStruct((t_lat, d), F32),
        grid=(t_lat // tm,),
        in_specs=[pl.BlockSpec((tm, d), lambda i: (i + n_ctx_tiles, 0)),
                  pl.BlockSpec((1,) + mod3.shape[1:], lambda i: (1 + (i * tm) // seq, 0, 0)),
                  _const_spec(g.shape), _const_spec(w_plain.shape),
                  pl.BlockSpec((tm, d_v), lambda i: (i, 0)), pl.BlockSpec((tm, d_v), lambda i: (i, 0)),
                  pl.BlockSpec((tm, d_inner), lambda i: (i, 0)), pl.BlockSpec((tm, d_inner), lambda i: (i, 0)),
                  pl.BlockSpec((tm, d_inner), lambda i: (i + n_ctx_tiles, 0)),
                  _const_spec(rows.shape), _const_spec(wbg.shape), _const_spec(wbm.shape), _const_spec(wo.shape)],
        out_specs=pl.BlockSpec((tm, d), lambda i: (i, 0)),
        compiler_params=pltpu.CompilerParams(vmem_limit_bytes=VMEM_LIMIT),
        name="mixout",
    )(x1, mod3, g, w_plain, o_f, o_b, y_f, y_b, cg, rows, wbg, wbm, wo)


def kernel(x, c, ctx, c_ctx, w_ada, b_ada, norm_g, ffn_w_gu, ffn_w_down, w_in, gdn_conv_w, gdn_A_log, gdn_dt_bias, gdn_norm_g, mb_conv_w, mb_conv_b, mb_A_log, mb_dt_bias, mb_D, mb_norm_g, w_branch_gdn, w_branch_mb, w_out, final_g):
    n_batch, seq, d = x.shape
    ctx_len = ctx.shape[1]
    assert w_ada.shape[0] == 1, "single-layer operation"
    d_qk = GDN_HEADS * GDN_D
    d_v = GDN_HEADS * GDN_D
    d_inner = mb_norm_g.shape[1]
    n_mb_heads = d_inner // MB_HEADDIM
    d_bc = MB_GROUPS * MB_STATE
    tm = 256
    t_ctx, t_lat = n_batch * ctx_len, n_batch * seq
    assert t_ctx % tm == 0 and seq % tm == 0 and tm % GRID_W == 0 and tm % ctx_len == 0
    assert ctx_len % SCAN_CHUNK == 0 and seq % SCAN_CHUNK == 0
    assert N_DIR * (2 * GDN_HEADS + n_mb_heads) <= LANE
    n_ctx_tiles = t_ctx // tm

    rows = 16
    cc = jnp.concatenate([c_ctx[None, :], c, jnp.zeros((rows - 1 - n_batch, d), F32)], axis=0)
    mod3 = _adaln(cc, w_ada[0], b_ada[0][None, :]).reshape(rows, 9, d)

    def row_all(i):
        return jnp.where(i < n_ctx_tiles, 0, 1 + ((i - n_ctx_tiles) * tm) // seq)

    xa = jnp.concatenate([ctx.reshape(t_ctx, d), x.reshape(t_lat, d)], axis=0)
    x1 = _ffn(xa, mod3, norm_g[0, 0][None, :], ffn_w_gu[0, 0].astype(BF16), ffn_w_down[0, 0].astype(BF16),
              final_g[None, :], k=0, row_of_tile=row_all, tm=tm, final=False, name="ffn1")

    sizes = (2 * d_qk + d_v, d_v, N_DIR * GDN_HEADS, N_DIR * GDN_HEADS, d_inner, d_inner + 2 * d_bc,
             N_DIR * n_mb_heads)
    o_qkv, o_za, o_a, o_beta, o_zb, o_xbc, o_dt, o_gates = [sum(sizes[:j]) for j in range(len(sizes) + 1)]
    wi = w_in[0]

    def regroup(a, off_qkv, off_xbc):
        return jnp.concatenate([a[..., off_xbc:off_xbc + d_inner], a[..., off_qkv:off_qkv + 2 * d_qk + d_v],
                                a[..., off_xbc + d_inner:off_xbc + d_inner + 2 * d_bc]], axis=-1)

    w_cg = regroup(wi, o_qkv, o_xbc).astype(BF16)
    n_small = N_DIR * (2 * GDN_HEADS + n_mb_heads)
    w_small = jnp.concatenate([wi[:, o_a:o_a + N_DIR * GDN_HEADS], wi[:, o_beta:o_beta + N_DIR * GDN_HEADS],
                               wi[:, o_dt:o_dt + N_DIR * n_mb_heads], jnp.zeros((d, LANE - n_small), F32)], axis=1)
    conv_all = jnp.concatenate([gdn_conv_w[0], mb_conv_w[0]], axis=1)
    bias_all = jnp.concatenate([jnp.zeros((1, 2 * d_qk + d_v), F32), mb_conv_b[0][None, :]], axis=1)
    conv_p = regroup(jnp.concatenate([conv_all, bias_all, jnp.zeros((2, conv_all.shape[1]), F32)], axis=0),
                     0, 2 * d_qk + d_v)
    col_q = d_inner
    col_b = d_inner + 2 * d_qk + d_v
    blocks = ((0, d_inner // 2, "x"), (d_inner // 2, d_inner, "x"), (col_q, col_q + d_qk, "q"),
              (col_q + d_qk, col_q + 2 * d_qk, "k"), (col_q + 2 * d_qk, col_b, "v"), (col_b, col_b + 2 * d_bc, "x"))
    cg, sg = _inproj(x1, mod3, norm_g[0, 1][None, :], w_cg, w_small, conv_p, tm=tm, n_ctx_tiles=n_ctx_tiles,
                     ctx_len=ctx_len, seq=seq, blocks=blocks)
    sgt = sg.T

    pad = jnp.zeros((LANE - n_small,), F32)
    zeros_g = jnp.zeros((N_DIR * GDN_HEADS,), F32)
    zeros_m = jnp.zeros((N_DIR * n_mb_heads,), F32)
    prow = jnp.stack([jnp.concatenate([gdn_A_log[0].reshape(-1), zeros_g, zeros_m, pad]),
                      jnp.concatenate([gdn_dt_bias[0].reshape(-1), zeros_g, zeros_m, pad]),
                      jnp.concatenate([zeros_g, zeros_g, mb_A_log[0].reshape(-1), pad]),
                      jnp.concatenate([zeros_g, zeros_g, mb_dt_bias[0].reshape(-1), pad])]
                     + [jnp.zeros((LANE,), F32)] * 4)
    pcol = prow.T
    ncc, nlc = ctx_len // SCAN_CHUNK, seq // SCAN_CHUNK
    prep = _gdn_prep(cg, sg, sgt, prow, pcol, col_q=col_q)
    o_f, o_b = _gdn_scan(prep, n_batch=n_batch, ncc=ncc, nlc=nlc)

    lane_head = jnp.arange(d_inner) // MB_HEADDIM
    ex = jnp.stack([(jnp.arange(LANE)[:, None] == (2 * N_DIR * GDN_HEADS + dd * n_mb_heads + lane_head)[None, :])
                    for dd in range(N_DIR)]).astype(BF16)
    y_f, y_b = _ssd_scan(cg, sg, sgt, prow, pcol, ex, n_batch=n_batch, ncc=ncc, nlc=nlc, d_inner=d_inner, col_b=col_b)

    w_plain = jnp.concatenate([wi[:, o_za:o_za + d_v], wi[:, o_zb:o_zb + d_inner], wi[:, o_gates:o_gates + 2 * d]],
                              axis=1).astype(BF16)
    rows3 = jnp.stack([jnp.concatenate([jnp.tile(gdn_norm_g[0], GDN_HEADS), jnp.zeros((d_inner - d_v,), F32)]),
                       jnp.repeat(mb_D[0], MB_HEADDIM), mb_norm_g[0]] + [jnp.zeros((d_inner,), F32)] * 5)
    x2 = _mixout(x1, mod3, norm_g[0, 1][None, :], w_plain, o_f, o_b, y_f, y_b, cg, rows3,
                 w_branch_gdn[0].astype(BF16), w_branch_mb[0].astype(BF16), w_out[0].astype(BF16),
                 tm=tm, n_ctx_tiles=n_ctx_tiles, seq=seq)

    out = _ffn(x2, mod3, norm_g[0, 2][None, :], ffn_w_gu[0, 1].astype(BF16), ffn_w_down[0, 1].astype(BF16),
               final_g[None, :], k=2, row_of_tile=lambda i: 1 + (i * tm) // seq, tm=tm, final=True, name="ffn2")
    return out.reshape(n_batch, seq, d)
```

```python
import functools

import jax
import jax.numpy as jnp
import numpy as np
from jax import lax
from jax.experimental import pallas as pl
from jax.experimental.pallas import tpu as pltpu

F32 = jnp.float32
BF16 = jnp.bfloat16
HIGHEST = lax.Precision.HIGHEST

EPS = 1e-6
FFN_RES = 0.5
GRID_W = 64
CONV_K = 5
GDN_HEADS = 8
GDN_D = 128
MB_HEADDIM = 64
MB_GROUPS = 2
MB_STATE = 128
N_DIR = 2

LANE = 128
SCAN_CHUNK = 128
VMEM_LIMIT = 56 * 1024 * 1024


def _const_spec(shape):
    nd = len(shape)
    return pl.BlockSpec(shape, lambda *_: (0,) * nd, pipeline_mode=pl.Buffered(1))


def _silu(x):
    return x * jax.nn.sigmoid(x)


def _softplus(x):
    return jnp.maximum(x, 0.0) + jnp.log(1.0 + jnp.exp(-jnp.abs(x)))


def _bdot(a, b):
    return jnp.dot(a.astype(BF16), b.astype(BF16), preferred_element_type=F32)


def _bdot_nt(a, b):
    return lax.dot_general(a.astype(BF16), b.astype(BF16), (((1,), (1,)), ((), ())),
                           preferred_element_type=F32)


def _bdot_tn(a, b):
    return lax.dot_general(a.astype(BF16), b.astype(BF16), (((0,), (0,)), ((), ())),
                           preferred_element_type=F32)


def _modulated_norm(x, g, shift, scale):
    ms = jnp.mean(x * x, axis=-1, keepdims=True)
    return x * lax.rsqrt(ms + EPS) * g * (1.0 + scale) + shift


def _adaln_kernel(c_ref, w_ref, b_ref, o_ref):
    s = _silu(c_ref[...])
    o_ref[...] = jnp.dot(s, w_ref[...], preferred_element_type=F32, precision=HIGHEST) + b_ref[...]


def _adaln(cc, w, b, tn=1152):
    rows, d = cc.shape
    n = w.shape[1]
    return pl.pallas_call(
        _adaln_kernel,
        out_shape=jax.ShapeDtypeStruct((rows, n), F32),
        grid=(n // tn,),
        in_specs=[pl.BlockSpec((rows, d), lambda j: (0, 0)),
                  pl.BlockSpec((d, tn), lambda j: (0, j)),
                  pl.BlockSpec((1, tn), lambda j: (0, j))],
        out_specs=pl.BlockSpec((rows, tn), lambda j: (0, j)),
        compiler_params=pltpu.CompilerParams(vmem_limit_bytes=VMEM_LIMIT),
        name="adaln",
    )(cc, w, b)


def _ffn_kernel(c_ref, x_ref, mod_ref, g_ref, wgu_ref, wd_ref, fg_ref, o_ref, *, k, d_ff, final, n_ctx_tiles):
    if n_ctx_tiles:
        x = jnp.where(pl.program_id(0) < n_ctx_tiles, c_ref[...], x_ref[...])
    else:
        x = x_ref[...]
    shift = mod_ref[0, 3 * k:3 * k + 1, :]
    scale = mod_ref[0, 3 * k + 1:3 * k + 2, :]
    gate = mod_ref[0, 3 * k + 2:3 * k + 3, :]
    hb = _modulated_norm(x, g_ref[...], shift, scale).astype(BF16)
    gt = jnp.dot(hb, wgu_ref[:, :d_ff], preferred_element_type=F32)
    up = jnp.dot(hb, wgu_ref[:, d_ff:], preferred_element_type=F32)
    act = (_silu(gt) * up).astype(BF16)
    y = x + FFN_RES * gate * jnp.dot(act, wd_ref[...], preferred_element_type=F32)
    if final:
        ms = jnp.mean(y * y, axis=-1, keepdims=True)
        y = y * lax.rsqrt(ms + EPS) * fg_ref[...]
    o_ref[...] = y


def _ffn(xc, x, mod3, g, wgu, wd, fg, *, k, row_of_tile, tm, final, name):
    t_lat, d = x.shape
    d_ff = wd.shape[0]
    if xc is None:
        nct = 0
        xc, c_spec = x, pl.BlockSpec((8, d), lambda i: (0, 0))
    else:
        nct = xc.shape[0] // tm
        c_spec = pl.BlockSpec((tm, d), lambda i: (jnp.minimum(i, nct - 1), 0))
    t = nct * tm + t_lat
    return pl.pallas_call(
        functools.partial(_ffn_kernel, k=k, d_ff=d_ff, final=final, n_ctx_tiles=nct),
        out_shape=jax.ShapeDtypeStruct((t, d), F32),
        grid=(t // tm,),
        in_specs=[c_spec, pl.BlockSpec((tm, d), lambda i: (jnp.maximum(i - nct, 0), 0)),
                  pl.BlockSpec((1,) + mod3.shape[1:], lambda i: (row_of_tile(i), 0, 0)),
                  _const_spec(g.shape), _const_spec(wgu.shape), _const_spec(wd.shape),
                  _const_spec(fg.shape)],
        out_specs=pl.BlockSpec((tm, d), lambda i: (i, 0)),
        compiler_params=pltpu.CompilerParams(vmem_limit_bytes=VMEM_LIMIT),
        name=name,
    )(xc, x, mod3, g, wgu, wd, fg)


def _conv_shift_matrix(tm, period):
    t = np.arange(tm)[:, None]
    s = np.arange(tm)[None, :]
    half = CONV_K // 2
    mats = [(s == t + dlt) & (t // period == (t + dlt) // period)
            for dlt in range(-half, half + 1) if dlt]
    return np.concatenate(mats, axis=1).astype(np.float32)


CONV_SPLIT = 2
CONV_HALO = 16


def _conv_shift_parts(tm, period):
    taps = CONV_K - 1
    r = tm // CONV_SPLIT
    rest = _conv_shift_matrix(tm, period).reshape(tm, taps, tm)
    blocks = []
    for s in range(CONV_SPLIT):
        rows = slice(s * r, (s + 1) * r)
        blocks.append(rest[rows, :, rows].reshape(r, taps * r).copy())
        rest[rows, :, rows] = 0
    corr = [np.zeros((2 * CONV_HALO, taps * 2 * CONV_HALO), np.float32)]
    for b in range(1, CONV_SPLIT):
        win = slice(b * r - CONV_HALO, b * r + CONV_HALO)
        corr.append(rest[win, :, win].reshape(2 * CONV_HALO, taps * 2 * CONV_HALO).copy())
        rest[win, :, win] = 0
    assert not rest.any()
    return np.stack(blocks), np.stack(corr[-max(CONV_SPLIT - 1, 1):])


def _split_dot(a, b_ref):
    b = b_ref[...]
    a1 = a.astype(BF16)
    a2 = (a - a1.astype(F32)).astype(BF16)
    b1 = b.astype(BF16)
    b2 = (b - b1.astype(F32)).astype(BF16)
    dot = functools.partial(jnp.dot, preferred_element_type=F32)
    return dot(a1, b1) + (dot(a1, b2) + dot(a2, b1))


def _inproj_kernel(x_ref, mod_ref, g_ref, w_ref, ws_ref, cp_ref, sb_ref, sc_ref, cg_ref, sg_ref, *, blocks):
    x = x_ref[...]
    tm = x.shape[0]
    r = tm // CONV_SPLIT
    h = _modulated_norm(x, g_ref[...], mod_ref[0, 3:4, :], mod_ref[0, 4:5, :])
    hb = h.astype(BF16)
    sg_ref[...] = _split_dot(h, ws_ref)
    half = CONV_K // 2
    for c0, c1, kind in blocks:
        u = jnp.dot(hb, w_ref[:, c0:c1], preferred_element_type=F32)
        taps = [(u * cp_ref[half + dlt:half + dlt + 1, c0:c1]).astype(BF16)
                for dlt in range(-half, half + 1) if dlt]
        conv = [jnp.dot(sb_ref[0, s], jnp.concatenate([tp[s * r:(s + 1) * r] for tp in taps], axis=0),
                        preferred_element_type=F32) for s in range(CONV_SPLIT)]
        pieces = []
        for s in range(CONV_SPLIT):
            if s > 0:
                pieces[-1] = pieces[-1] + cross[:CONV_HALO]
                pieces.append(conv[s][:CONV_HALO] + cross[CONV_HALO:])
                pieces.append(conv[s][CONV_HALO:r - CONV_HALO] if s < CONV_SPLIT - 1 else conv[s][CONV_HALO:])
            else:
                pieces.append(conv[s][:r - CONV_HALO] if CONV_SPLIT > 1 else conv[s])
            if s < CONV_SPLIT - 1:
                pieces.append(conv[s][r - CONV_HALO:])
                b = (s + 1) * r
                cross = jnp.dot(sc_ref[0, s], jnp.concatenate([tp[b - CONV_HALO:b + CONV_HALO] for tp in taps],
                                                              axis=0), preferred_element_type=F32)
        acc = u * cp_ref[half:half + 1, c0:c1] + jnp.concatenate(pieces, axis=0)
        if kind == "x":
            acc = acc + cp_ref[CONV_K:CONV_K + 1, c0:c1]
        y = _silu(acc)
        if kind in ("q", "k"):
            parts = []
            for hh in range((c1 - c0) // GDN_D):
                yh = y[:, hh * GDN_D:(hh + 1) * GDN_D]
                inv = lax.rsqrt(jnp.sum(yh * yh, axis=-1, keepdims=True) + EPS)
                if kind == "q":
                    inv = inv * (GDN_D ** -0.5)
                parts.append(yh * inv)
            y = jnp.concatenate(parts, axis=1)
        cg_ref[:, c0:c1] = y


def _inproj(x1, mod3, g, w_cg, w_small, conv_p, *, tm, n_ctx_tiles, ctx_len, seq, blocks):
    t, d = x1.shape
    ncg = w_cg.shape[1]

    def row_of_tile(i):
        return jnp.where(i < n_ctx_tiles, 0, 1 + ((i - n_ctx_tiles) * tm) // seq)

    parts = [_conv_shift_parts(tm, ctx_len), _conv_shift_parts(tm, GRID_W)]
    sblk = jnp.asarray(np.stack([p[0] for p in parts]), dtype=BF16)
    scor = jnp.asarray(np.stack([p[1] for p in parts]), dtype=BF16)
    variant = lambda i: (jnp.where(i < n_ctx_tiles, 0, 1), 0, 0, 0)
    return pl.pallas_call(
        functools.partial(_inproj_kernel, blocks=blocks),
        out_shape=(jax.ShapeDtypeStruct((t, ncg), F32), jax.ShapeDtypeStruct((t, LANE), F32)),
        grid=(t // tm,),
        in_specs=[pl.BlockSpec((tm, d), lambda i: (i, 0)),
                  pl.BlockSpec((1,) + mod3.shape[1:], lambda i: (row_of_tile(i), 0, 0)),
                  _const_spec(g.shape), _const_spec(w_cg.shape), _const_spec(w_small.shape),
                  _const_spec(conv_p.shape),
                  pl.BlockSpec((1,) + sblk.shape[1:], variant), pl.BlockSpec((1,) + scor.shape[1:], variant)],
        out_specs=(pl.BlockSpec((tm, ncg), lambda i: (i, 0)), pl.BlockSpec((tm, LANE), lambda i: (i, 0))),
        compiler_params=pltpu.CompilerParams(vmem_limit_bytes=VMEM_LIMIT),
        name="inproj",
    )(x1, mod3, g, w_cg, w_small, conv_p, sblk, scor)


def _tri_masks(n, rev):
    ii = lax.broadcasted_iota(jnp.int32, (n, n), 0)
    jj = lax.broadcasted_iota(jnp.int32, (n, n), 1)
    if rev:
        return ii, jj, ii <= jj, ii < jj
    return ii, jj, ii >= jj, ii > jj


def _scan_block_index(b, s, rev, n_batch, ncc, nlc):
    lc = jnp.maximum(s - ncc, 0)
    if rev:
        ctx_blk = b * ncc + (ncc - 1 - jnp.minimum(s, ncc - 1))
        lat_blk = n_batch * ncc + b * nlc + (nlc - 1 - lc)
    else:
        ctx_blk = b * ncc + jnp.minimum(s, ncc - 1)
        lat_blk = n_batch * ncc + b * nlc + lc
    return jnp.where(s < ncc, ctx_blk, lat_blk)


def _scan_out_index(b, s, rev, ncc, nlc):
    lc = jnp.maximum(s - ncc, 0)
    return b * nlc + ((nlc - 1 - lc) if rev else lc)


def _gdn_prep_kernel(q_ref, k_ref, v_ref, sg_ref, sgt_ref, prow_ref, pcol_ref,
                     uf, wf, qkf, qif, kof, ub, wb, qkb, qib, kob, dec_ref):
    n = SCAN_CHUNK
    q, k, v, sg, sgt = q_ref[...], k_ref[...], v_ref[...], sg_ref[...], sgt_ref[...]
    prow, pcol = prow_ref[...], pcol_ref[...]
    outs = ((uf, wf, qkf, qif, kof), (ub, wb, qkb, qib, kob))
    g_col = -jnp.exp(prow[0:1, :]) * _softplus(sg + prow[1:2, :])
    g_row = -jnp.exp(pcol[:, 0:1]) * _softplus(sgt + pcol[:, 1:2])
    beta_col = jax.nn.sigmoid(sg)
    probs = []
    for d, rev in ((0, False), (1, True)):
        ii, jj, incl, strict = _tri_masks(n, rev)
        mtri = incl.astype(F32)
        mtri_t = (ii >= jj if rev else ii <= jj).astype(F32)
        gc_col = jnp.dot(mtri, g_col, preferred_element_type=F32, precision=HIGHEST)
        gc_row = jnp.dot(g_row, mtri_t, preferred_element_type=F32, precision=HIGHEST)
        last = 0 if rev else n - 1
        tot_rows = jnp.exp(jnp.broadcast_to(gc_row[:, last:last + 1], (LANE, n)))
        dec_ref[0, d * GDN_HEADS:(d + 1) * GDN_HEADS, :] = tot_rows[d * GDN_HEADS:(d + 1) * GDN_HEADS, :]
        for hh in range(GDN_HEADS):
            ci = d * GDN_HEADS + hh
            lo, hi = hh * GDN_D, (hh + 1) * GDN_D
            qh, kh, vh = q[:, lo:hi], k[:, lo:hi], v[:, lo:hi]
            gcol = gc_col[:, ci:ci + 1]
            grow = gc_row[ci:ci + 1, :]
            bcol = beta_col[:, N_DIR * GDN_HEADS + ci:N_DIR * GDN_HEADS + ci + 1]
            eg = jnp.exp(gcol)
            decay = jnp.where(incl, jnp.exp(jnp.where(incl, gcol - grow, 0.0)), 0.0)
            kb = kh * bcol
            o_u, o_w, o_qk, o_qi, o_ko = outs[d]
            o_qi[:, lo:hi] = (qh * eg).astype(BF16)
            o_ko[:, lo:hi] = (kh * jnp.exp(gcol[last:last + 1, :] - gcol)).astype(BF16)
            probs.append(dict(d=d, lo=lo, hi=hi, ii=ii, jj=jj, strict=strict, decay=decay,
                              lhs=jnp.concatenate([kb, qh], axis=0).astype(BF16), kh=kh.astype(BF16),
                              rhs=jnp.concatenate([vh * bcol, kb * eg], axis=1).astype(BF16)))
    kk = [_bdot_nt(p["lhs"], p["kh"]) for p in probs]
    for p, kkp in zip(probs, kk):
        p["a"] = jnp.where(p["strict"], kkp[:n] * p["decay"], 0.0)
        outs[p["d"]][2][:, p["lo"]:p["hi"]] = (kkp[n:] * p["decay"]).astype(BF16)
    lg = 3
    for p in probs:
        blk = (p["ii"] >> lg) == (p["jj"] >> lg)
        p["a8"] = jnp.where(blk, p["a"], 0.0).astype(BF16)
        p["t"] = (p["ii"] == p["jj"]).astype(F32) - jnp.where(blk, p["a"], 0.0)
    x2 = [_bdot(p["a8"], p["a8"]).astype(BF16) for p in probs]
    x4 = [_bdot(x, x).astype(BF16) for x in x2]
    t1 = [p["t"] + _bdot(p["t"], x) for p, x in zip(probs, x2)]
    ts = [t + _bdot(t, x) for t, x in zip(t1, x4)]
    while (1 << lg) < n:
        bs = []
        for p in probs:
            off = ((p["ii"] >> (lg + 1)) == (p["jj"] >> (lg + 1))) & ((p["ii"] >> lg) != (p["jj"] >> lg))
            bs.append(jnp.where(off, p["a"], 0.0).astype(BF16))
        tb = [t.astype(BF16) for t in ts]
        bt = [_bdot(b, t) for b, t in zip(bs, tb)]
        ts = [t - _bdot(t16, x) for t, t16, x in zip(ts, tb, bt)]
        lg += 1
    sol = [_bdot(t, p["rhs"]) for t, p in zip(ts, probs)]
    for p, s in zip(probs, sol):
        o_u, o_w = outs[p["d"]][0], outs[p["d"]][1]
        o_u[:, p["lo"]:p["hi"]] = s[:, :GDN_D]
        o_w[:, p["lo"]:p["hi"]] = s[:, GDN_D:].astype(BF16)


def _gdn_prep(cg, sg, sgt, prow, pcol, *, col_q):
    n = SCAN_CHUNK
    t = cg.shape[0]
    dq = GDN_HEADS * GDN_D
    cq = col_q // dq
    tok = lambda col: pl.BlockSpec((n, dq), lambda i: (i, col))
    wide = jax.ShapeDtypeStruct((t, dq), F32)
    half = jax.ShapeDtypeStruct((t, dq), BF16)
    return pl.pallas_call(
        _gdn_prep_kernel,
        out_shape=(wide, half, half, half, half) * 2
                  + (jax.ShapeDtypeStruct((t // n, N_DIR * GDN_HEADS, n), F32),),
        grid=(t // n,),
        in_specs=[tok(cq), tok(cq + 1), tok(cq + 2), pl.BlockSpec((n, LANE), lambda i: (i, 0)),
                  pl.BlockSpec((LANE, n), lambda i: (0, i)), _const_spec(prow.shape), _const_spec(pcol.shape)],
        out_specs=(tok(0),) * 10 + (pl.BlockSpec((1, N_DIR * GDN_HEADS, n), lambda i: (i, 0, 0)),),
        compiler_params=pltpu.CompilerParams(vmem_limit_bytes=VMEM_LIMIT),
        name="gdn_prep",
    )(cg, cg, cg, sg, sgt, prow, pcol)


def _gdn_kernel(uf, wf, qkf, qif, kof, decf, ub, wb, qkb, qib, kob, decb, of_ref, ob_ref, state_ref):
    @pl.when(pl.program_id(1) == 0)
    def _():
        state_ref[...] = jnp.zeros_like(state_ref)

    n = SCAN_CHUNK
    ins = ((uf, wf, qkf, qif, kof, decf, of_ref), (ub, wb, qkb, qib, kob, decb, ob_ref))
    probs = [(d, hh) for d in range(N_DIR) for hh in range(GDN_HEADS)]
    sl = lambda hh: slice(hh * GDN_D, (hh + 1) * GDN_D)
    st = [state_ref[d, hh] for d, hh in probs]
    ws = [jnp.dot(jnp.concatenate([ins[d][1][:, sl(hh)], ins[d][3][:, sl(hh)]], axis=0), s.astype(BF16),
                  preferred_element_type=F32) for (d, hh), s in zip(probs, st)]
    v_new = [(ins[d][0][:, sl(hh)] - w[:n]).astype(BF16) for (d, hh), w in zip(probs, ws)]
    for (d, hh), s, w, vn in zip(probs, st, ws, v_new):
        ins[d][6][:, sl(hh)] = w[n:] + jnp.dot(ins[d][2][:, sl(hh)], vn, preferred_element_type=F32)
        ci = d * GDN_HEADS + hh
        state_ref[d, hh] = s * ins[d][5][0, ci:ci + 1, :] + lax.dot_general(
            ins[d][4][:, sl(hh)], vn, (((0,), (0,)), ((), ())), preferred_element_type=F32)


def _gdn_scan(prep, *, n_batch, ncc, nlc):
    n = SCAN_CHUNK
    dq = GDN_HEADS * GDN_D
    dec = prep[-1]

    def tok(rev):
        return pl.BlockSpec((n, dq), lambda b, s: (_scan_block_index(b, s, rev, n_batch, ncc, nlc), 0))

    def dec_spec(rev):
        return pl.BlockSpec((1,) + dec.shape[1:], lambda b, s: (_scan_block_index(b, s, rev, n_batch, ncc, nlc), 0, 0))

    def out(rev):
        return pl.BlockSpec((n, dq), lambda b, s: (_scan_out_index(b, s, rev, ncc, nlc), 0))

    in_specs = [tok(False)] * 5 + [dec_spec(False)] + [tok(True)] * 5 + [dec_spec(True)]
    t_lat = n_batch * nlc * n
    return pl.pallas_call(
        _gdn_kernel,
        out_shape=(jax.ShapeDtypeStruct((t_lat, dq), F32),) * 2,
        grid=(n_batch, ncc + nlc),
        in_specs=in_specs,
        out_specs=(out(False), out(True)),
        scratch_shapes=[pltpu.VMEM((N_DIR, GDN_HEADS, GDN_D, GDN_D), F32)],
        compiler_params=pltpu.CompilerParams(vmem_limit_bytes=VMEM_LIMIT,
                                             dimension_semantics=("arbitrary", "arbitrary")),
        name="gdn_scan",
    )(*prep[0:5], dec, *prep[5:10], dec)


def _expand(x, e, pieces):
    acc = None
    r = x
    for _ in range(pieces):
        xp = r.astype(BF16)
        r = r - xp.astype(F32)
        term = jnp.dot(xp, e, preferred_element_type=F32)
        acc = term if acc is None else acc + term
    return acc


def _ssd_dir(xs, bm, cm, sg, sgt, prow, pcol, ex, state_ref, y_ref, d, rev):
    n = SCAN_CHUNK
    n_heads = xs.shape[1] // MB_HEADDIM
    hpg = n_heads // MB_GROUPS
    gw = hpg * MB_HEADDIM
    ii, jj, incl, _ = _tri_masks(n, rev)
    mtri = incl.astype(F32)
    mtri_t = (ii <= jj if not rev else ii >= jj).astype(F32)
    dt_col = _softplus(sg + prow[3:4, :])
    dt_row = _softplus(sgt + pcol[:, 3:4])
    acs_col = jnp.dot(mtri, dt_col * -jnp.exp(prow[2:3, :]), preferred_element_type=F32, precision=HIGHEST)
    acs_row = jnp.dot(dt_row * -jnp.exp(pcol[:, 2:3]), mtri_t, preferred_element_type=F32, precision=HIGHEST)
    last = 0 if rev else n - 1
    tot = acs_col[last:last + 1, :]
    f0 = 2 * N_DIR * GDN_HEADS + d * n_heads
    heads = lambda a: a[:, f0:f0 + n_heads]
    e_in = _expand(heads(dt_col * jnp.exp(tot - acs_col)), ex, 2)
    e_out = _expand(heads(jnp.exp(acs_col)), ex, 2)
    e_tot = _expand(heads(jnp.broadcast_to(jnp.exp(tot), (8, LANE))), ex, 3)[0:1, :]
    xw = xs * e_in
    lane = lax.broadcasted_iota(jnp.int32, (n, 2 * MB_HEADDIM), 1)
    left = lane < MB_HEADDIM
    for g in range(MB_GROUPS):
        bg = bm[:, g * MB_STATE:(g + 1) * MB_STATE]
        cgm = cm[:, g * MB_STATE:(g + 1) * MB_STATE]
        cb = _bdot_nt(cgm, bg)
        st = state_ref[d, g]
        y_off = _bdot(cgm, st) * e_out[:, g * gw:(g + 1) * gw]
        state_ref[d, g] = st * e_tot[:, g * gw:(g + 1) * gw] + _bdot_tn(bg, xw[:, g * gw:(g + 1) * gw])
        for pr in range(hpg // 2):
            h0 = g * hpg + 2 * pr
            c0 = N_DIR * GDN_HEADS * 2 + d * n_heads + h0
            ms = []
            for hh in (0, 1):
                ccol = acs_col[:, c0 + hh:c0 + hh + 1]
                crow = acs_row[c0 + hh:c0 + hh + 1, :]
                seg = jnp.where(incl, jnp.exp(jnp.where(incl, ccol - crow, 0.0)), 0.0)
                ms.append(cb * seg * dt_row[c0 + hh:c0 + hh + 1, :])
            m2 = jnp.concatenate(ms, axis=1)
            lo = h0 * MB_HEADDIM
            xp = xs[:, lo:lo + 2 * MB_HEADDIM]
            xbd = jnp.concatenate([jnp.where(left, xp, 0.0), jnp.where(left, 0.0, xp)], axis=0)
            y_ref[:, lo:lo + 2 * MB_HEADDIM] = _bdot(m2, xbd) + y_off[:, lo - g * gw:lo - g * gw + 2 * MB_HEADDIM]


def _ssd_kernel(xf, bf, cf, sgf, sgtf, xb, bb, cb, sgb, sgtb, prow_ref, pcol_ref, ex_ref, yf_ref, yb_ref, state_ref):
    @pl.when(pl.program_id(1) == 0)
    def _():
        state_ref[...] = jnp.zeros_like(state_ref)

    prow, pcol = prow_ref[...], pcol_ref[...]
    _ssd_dir(xf[...], bf[...], cf[...], sgf[...], sgtf[...], prow, pcol, ex_ref[...], state_ref, yf_ref, 0, False)
    _ssd_dir(xb[...], bb[...], cb[...], sgb[...], sgtb[...], prow, pcol, ex_ref[...], state_ref, yb_ref, 1, True)


def _ssd_scan(cg, sg, sgt, prow, pcol, ex, *, n_batch, ncc, nlc, d_inner, col_b):
    n = SCAN_CHUNK
    bw = MB_GROUPS * MB_STATE
    cb_ = col_b // bw

    def blk(rev, width, col):
        return pl.BlockSpec((n, width), lambda b, s: (_scan_block_index(b, s, rev, n_batch, ncc, nlc), col))

    def small_t(rev):
        return pl.BlockSpec((LANE, n), lambda b, s: (0, _scan_block_index(b, s, rev, n_batch, ncc, nlc)))

    def out(rev):
        return pl.BlockSpec((n, d_inner), lambda b, s: (_scan_out_index(b, s, rev, ncc, nlc), 0))

    in_specs = []
    for rev in (False, True):
        in_specs += [blk(rev, d_inner, 0), blk(rev, bw, cb_), blk(rev, bw, cb_ + 1), blk(rev, LANE, 0), small_t(rev)]
    in_specs += [_const_spec(prow.shape), _const_spec(pcol.shape), _const_spec(ex.shape)]
    t_lat = n_batch * nlc * n
    return pl.pallas_call(
        _ssd_kernel,
        out_shape=(jax.ShapeDtypeStruct((t_lat, d_inner), F32),) * 2,
        grid=(n_batch, ncc + nlc),
        in_specs=in_specs,
        out_specs=(out(False), out(True)),
        scratch_shapes=[pltpu.VMEM((N_DIR, MB_GROUPS, MB_STATE, d_inner // MB_GROUPS), F32)],
        compiler_params=pltpu.CompilerParams(vmem_limit_bytes=VMEM_LIMIT,
                                             dimension_semantics=("arbitrary", "arbitrary")),
        name="ssd_scan",
    )(cg, cg, cg, sg, sgt, cg, cg, cg, sg, sgt, prow, pcol, ex)


def _mixout_kernel(x_ref, mod_ref, g_ref, wp_ref, of_ref, ob_ref, yf_ref, yb_ref, xs_ref,
                   rows_ref, wbg_ref, wbm_ref, wo_ref, o_ref, *, d_model, d_v, d_inner):
    x = x_ref[...]
    hb = _modulated_norm(x, g_ref[...], mod_ref[0, 3:4, :], mod_ref[0, 4:5, :]).astype(BF16)
    gate = mod_ref[0, 5:6, :]
    z = jnp.dot(hb, wp_ref[...], preferred_element_type=F32)
    za = z[:, :d_v]
    zb = z[:, d_v:d_v + d_inner]
    ga = z[:, d_v + d_inner:d_v + d_inner + d_model]
    gb = z[:, d_v + d_inner + d_model:]

    o = of_ref[...] + ob_ref[...]
    parts = []
    for hh in range(d_v // GDN_D):
        oh = o[:, hh * GDN_D:(hh + 1) * GDN_D]
        parts.append(oh * lax.rsqrt(jnp.mean(oh * oh, axis=-1, keepdims=True) + EPS))
    ya = jnp.concatenate(parts, axis=1) * rows_ref[0:1, :d_v] * _silu(za)

    yb = (yf_ref[...] + yb_ref[...] + rows_ref[1:2, :] * xs_ref[...]) * _silu(zb)
    gw = d_inner // MB_GROUPS
    parts = []
    for g in range(MB_GROUPS):
        yg = yb[:, g * gw:(g + 1) * gw]
        parts.append(yg * lax.rsqrt(jnp.mean(yg * yg, axis=-1, keepdims=True) + EPS))
    yb = jnp.concatenate(parts, axis=1) * rows_ref[2:3, :]

    merged = (jax.nn.sigmoid(ga) * _bdot(ya, wbg_ref[...]) + jax.nn.sigmoid(gb) * _bdot(yb, wbm_ref[...]))
    o_ref[...] = x + gate * _bdot(merged, wo_ref[...])


def _mixout(x1, mod3, g, w_plain, o_f, o_b, y_f, y_b, cg, rows, wbg, wbm, wo, *, tm, n_ctx_tiles, seq):
    t_lat, d_v = o_f.shape
    d_inner = y_f.shape[1]
    d = x1.shape[1]
    return pl.pallas_call(
        functools.partial(_mixout_kernel, d_model=d, d_v=d_v, d_inner=d_inner),
        out_shape=jax.ShapeDtypeStruct((t_lat, d), F32),
        grid=(t_lat // tm,),
        in_specs=[pl.BlockSpec((tm, d), lambda i: (i + n_ctx_tiles, 0)),
                  pl.BlockSpec((1,) + mod3.shape[1:], lambda i: (1 + (i * tm) // seq, 0, 0)),
                  _const_spec(g.shape), _const_spec(w_plain.shape),
                  pl.BlockSpec((tm, d_v), lambda i: (i, 0)), pl.BlockSpec((tm, d_v), lambda i: (i, 0)),
                  pl.BlockSpec((tm, d_inner), lambda i: (i, 0)), pl.BlockSpec((tm, d_inner), lambda i: (i, 0)),
                  pl.BlockSpec((tm, d_inner), lambda i: (i + n_ctx_tiles, 0)),
                  _const_spec(rows.shape), _const_spec(wbg.shape), _const_spec(wbm.shape), _const_spec(wo.shape)],
        out_specs=pl.BlockSpec((tm, d), lambda i: (i, 0)),
        compiler_params=pltpu.CompilerParams(vmem_limit_bytes=VMEM_LIMIT),
        name="mixout",
    )(x1, mod3, g, w_plain, o_f, o_b, y_f, y_b, cg, rows, wbg, wbm, wo)


def kernel(x, c, ctx, c_ctx, w_ada, b_ada, norm_g, ffn_w_gu, ffn_w_down, w_in, gdn_conv_w, gdn_A_log, gdn_dt_bias, gdn_norm_g, mb_conv_w, mb_conv_b, mb_A_log, mb_dt_bias, mb_D, mb_norm_g, w_branch_gdn, w_branch_mb, w_out, final_g):
    n_batch, seq, d = x.shape
    ctx_len = ctx.shape[1]
    assert w_ada.shape[0] == 1, "single-layer operation"
    d_qk = GDN_HEADS * GDN_D
    d_v = GDN_HEADS * GDN_D
    d_inner = mb_norm_g.shape[1]
    n_mb_heads = d_inner // MB_HEADDIM
    d_bc = MB_GROUPS * MB_STATE
    tm = 256
    t_ctx, t_lat = n_batch * ctx_len, n_batch * seq
    assert t_ctx % tm == 0 and seq % tm == 0 and tm % GRID_W == 0 and tm % ctx_len == 0
    assert ctx_len % SCAN_CHUNK == 0 and seq % SCAN_CHUNK == 0
    assert N_DIR * (2 * GDN_HEADS + n_mb_heads) <= LANE
    n_ctx_tiles = t_ctx // tm

    rows = 16
    cc = jnp.concatenate([c_ctx[None, :], c, jnp.zeros((rows - 1 - n_batch, d), F32)], axis=0)
    mod3 = _adaln(cc, w_ada[0], b_ada[0][None, :]).reshape(rows, 9, d)

    def row_all(i):
        return jnp.where(i < n_ctx_tiles, 0, 1 + ((i - n_ctx_tiles) * tm) // seq)

    x1 = _ffn(ctx.reshape(t_ctx, d), x.reshape(t_lat, d), mod3, norm_g[0, 0][None, :], ffn_w_gu[0, 0].astype(BF16), ffn_w_down[0, 0].astype(BF16),
              final_g[None, :], k=0, row_of_tile=row_all, tm=tm, final=False, name="ffn1")

    sizes = (2 * d_qk + d_v, d_v, N_DIR * GDN_HEADS, N_DIR * GDN_HEADS, d_inner, d_inner + 2 * d_bc,
             N_DIR * n_mb_heads)
    o_qkv, o_za, o_a, o_beta, o_zb, o_xbc, o_dt, o_gates = [sum(sizes[:j]) for j in range(len(sizes) + 1)]
    wi = w_in[0]

    def regroup(a, off_qkv, off_xbc):
        return jnp.concatenate([a[..., off_xbc:off_xbc + d_inner], a[..., off_qkv:off_qkv + 2 * d_qk + d_v],
                                a[..., off_xbc + d_inner:off_xbc + d_inner + 2 * d_bc]], axis=-1)

    w_cg = regroup(wi, o_qkv, o_xbc).astype(BF16)
    n_small = N_DIR * (2 * GDN_HEADS + n_mb_heads)
    w_small = jnp.concatenate([wi[:, o_a:o_a + N_DIR * GDN_HEADS], wi[:, o_beta:o_beta + N_DIR * GDN_HEADS],
                               wi[:, o_dt:o_dt + N_DIR * n_mb_heads], jnp.zeros((d, LANE - n_small), F32)], axis=1)
    conv_all = jnp.concatenate([gdn_conv_w[0], mb_conv_w[0]], axis=1)
    bias_all = jnp.concatenate([jnp.zeros((1, 2 * d_qk + d_v), F32), mb_conv_b[0][None, :]], axis=1)
    conv_p = regroup(jnp.concatenate([conv_all, bias_all, jnp.zeros((2, conv_all.shape[1]), F32)], axis=0),
                     0, 2 * d_qk + d_v)
    col_q = d_inner
    col_b = d_inner + 2 * d_qk + d_v
    blocks = ((0, d_inner // 2, "x"), (d_inner // 2, d_inner, "x"), (col_q, col_q + d_qk, "q"),
              (col_q + d_qk, col_q + 2 * d_qk, "k"), (col_q + 2 * d_qk, col_b, "v"), (col_b, col_b + 2 * d_bc, "x"))
    cg, sg = _inproj(x1, mod3, norm_g[0, 1][None, :], w_cg, w_small, conv_p, tm=tm, n_ctx_tiles=n_ctx_tiles,
                     ctx_len=ctx_len, seq=seq, blocks=blocks)
    sgt = sg.T

    pad = jnp.zeros((LANE - n_small,), F32)
    zeros_g = jnp.zeros((N_DIR * GDN_HEADS,), F32)
    zeros_m = jnp.zeros((N_DIR * n_mb_heads,), F32)
    prow = jnp.stack([jnp.concatenate([gdn_A_log[0].reshape(-1), zeros_g, zeros_m, pad]),
                      jnp.concatenate([gdn_dt_bias[0].reshape(-1), zeros_g, zeros_m, pad]),
                      jnp.concatenate([zeros_g, zeros_g, mb_A_log[0].reshape(-1), pad]),
                      jnp.concatenate([zeros_g, zeros_g, mb_dt_bias[0].reshape(-1), pad])]
                     + [jnp.zeros((LANE,), F32)] * 4)
    pcol = prow.T
    ncc, nlc = ctx_len // SCAN_CHUNK, seq // SCAN_CHUNK
    prep = _gdn_prep(cg, sg, sgt, prow, pcol, col_q=col_q)
    o_f, o_b = _gdn_scan(prep, n_batch=n_batch, ncc=ncc, nlc=nlc)

    ex = (jnp.arange(n_mb_heads)[:, None] == (jnp.arange(d_inner) // MB_HEADDIM)[None, :]).astype(BF16)
    y_f, y_b = _ssd_scan(cg, sg, sgt, prow, pcol, ex, n_batch=n_batch, ncc=ncc, nlc=nlc, d_inner=d_inner, col_b=col_b)

    w_plain = jnp.concatenate([wi[:, o_za:o_za + d_v], wi[:, o_zb:o_zb + d_inner], wi[:, o_gates:o_gates + 2 * d]],
                              axis=1).astype(BF16)
    rows3 = jnp.stack([jnp.concatenate([jnp.tile(gdn_norm_g[0], GDN_HEADS), jnp.zeros((d_inner - d_v,), F32)]),
                       jnp.repeat(mb_D[0], MB_HEADDIM), mb_norm_g[0]] + [jnp.zeros((d_inner,), F32)] * 5)
    x2 = _mixout(x1, mod3, norm_g[0, 1][None, :], w_plain, o_f, o_b, y_f, y_b, cg, rows3,
                 w_branch_gdn[0].astype(BF16), w_branch_mb[0].astype(BF16), w_out[0].astype(BF16),
                 tm=tm, n_ctx_tiles=n_ctx_tiles, seq=seq)

    out = _ffn(None, x2, mod3, norm_g[0, 2][None, :], ffn_w_gu[0, 1].astype(BF16), ffn_w_down[0, 1].astype(BF16),
               final_g[None, :], k=2, row_of_tile=lambda i: 1 + (i * tm) // seq, tm=tm, final=True, name="ffn2")
    return out.reshape(n_batch, seq, d)
```

```python
import functools

import jax
import jax.numpy as jnp
import numpy as np
from jax import lax
from jax.experimental import pallas as pl
from jax.experimental.pallas import tpu as pltpu

F32 = jnp.float32
BF16 = jnp.bfloat16
HIGHEST = lax.Precision.HIGHEST

EPS = 1e-6
FFN_RES = 0.5
GRID_W = 64
CONV_K = 5
GDN_HEADS = 8
GDN_D = 128
MB_HEADDIM = 64
MB_GROUPS = 2
MB_STATE = 128
N_DIR = 2

LANE = 128
SCAN_CHUNK = 128
VMEM_LIMIT = 56 * 1024 * 1024


def _const_spec(shape):
    nd = len(shape)
    return pl.BlockSpec(shape, lambda *_: (0,) * nd, pipeline_mode=pl.Buffered(1))


def _silu(x):
    return x * jax.nn.sigmoid(x)


def _softplus(x):
    return jnp.maximum(x, 0.0) + jnp.log(1.0 + jnp.exp(-jnp.abs(x)))


def _bdot(a, b):
    return jnp.dot(a.astype(BF16), b.astype(BF16), preferred_element_type=F32)


def _bdot_nt(a, b):
    return lax.dot_general(a.astype(BF16), b.astype(BF16), (((1,), (1,)), ((), ())),
                           preferred_element_type=F32)


def _bdot_tn(a, b):
    return lax.dot_general(a.astype(BF16), b.astype(BF16), (((0,), (0,)), ((), ())),
                           preferred_element_type=F32)


def _modulated_norm(x, g, shift, scale):
    ms = jnp.mean(x * x, axis=-1, keepdims=True)
    return x * lax.rsqrt(ms + EPS) * g * (1.0 + scale) + shift


def _adaln_kernel(c_ref, w_ref, b_ref, o_ref):
    s = _silu(c_ref[...])
    o_ref[...] = jnp.dot(s, w_ref[...], preferred_element_type=F32, precision=HIGHEST) + b_ref[...]


def _adaln(cc, w, b, tn=1152):
    rows, d = cc.shape
    n = w.shape[1]
    return pl.pallas_call(
        _adaln_kernel,
        out_shape=jax.ShapeDtypeStruct((rows, n), F32),
        grid=(n // tn,),
        in_specs=[pl.BlockSpec((rows, d), lambda j: (0, 0)),
                  pl.BlockSpec((d, tn), lambda j: (0, j)),
                  pl.BlockSpec((1, tn), lambda j: (0, j))],
        out_specs=pl.BlockSpec((rows, tn), lambda j: (0, j)),
        compiler_params=pltpu.CompilerParams(vmem_limit_bytes=VMEM_LIMIT),
        name="adaln",
    )(cc, w, b)


def _ffn_kernel(c_ref, x_ref, mod_ref, g_ref, wgu_ref, wd_ref, fg_ref, o_ref, *, k, d_ff, final, n_ctx_tiles):
    if n_ctx_tiles:
        x = jnp.where(pl.program_id(0) < n_ctx_tiles, c_ref[...], x_ref[...])
    else:
        x = x_ref[...]
    shift = mod_ref[0, 3 * k:3 * k + 1, :]
    scale = mod_ref[0, 3 * k + 1:3 * k + 2, :]
    gate = mod_ref[0, 3 * k + 2:3 * k + 3, :]
    hb = _modulated_norm(x, g_ref[...], shift, scale).astype(BF16)
    gt = jnp.dot(hb, wgu_ref[0, :, :d_ff], preferred_element_type=F32)
    up = jnp.dot(hb, wgu_ref[0, :, d_ff:], preferred_element_type=F32)
    act = (_silu(gt) * up).astype(BF16)
    y = x + FFN_RES * gate * jnp.dot(act, wd_ref[0], preferred_element_type=F32)
    if final:
        ms = jnp.mean(y * y, axis=-1, keepdims=True)
        y = y * lax.rsqrt(ms + EPS) * fg_ref[...]
    o_ref[...] = y


def _ffn(xc, x, mod3, g, wgu, wd, fg, *, k, row_of_tile, tm, final, name):
    t_lat, d = x.shape
    d_ff = wd.shape[1]
    which = k // 2
    w_spec = lambda w: pl.BlockSpec((1,) + w.shape[1:], lambda i: (which, 0, 0), pipeline_mode=pl.Buffered(1))
    if xc is None:
        nct = 0
        xc, c_spec = x, pl.BlockSpec((8, d), lambda i: (0, 0))
    else:
        nct = xc.shape[0] // tm
        c_spec = pl.BlockSpec((tm, d), lambda i: (jnp.minimum(i, nct - 1), 0))
    t = nct * tm + t_lat
    return pl.pallas_call(
        functools.partial(_ffn_kernel, k=k, d_ff=d_ff, final=final, n_ctx_tiles=nct),
        out_shape=jax.ShapeDtypeStruct((t, d), F32),
        grid=(t // tm,),
        in_specs=[c_spec, pl.BlockSpec((tm, d), lambda i: (jnp.maximum(i - nct, 0), 0)),
                  pl.BlockSpec((1,) + mod3.shape[1:], lambda i: (row_of_tile(i), 0, 0)),
                  _const_spec(g.shape), w_spec(wgu), w_spec(wd), _const_spec(fg.shape)],
        out_specs=pl.BlockSpec((tm, d), lambda i: (i, 0)),
        compiler_params=pltpu.CompilerParams(vmem_limit_bytes=VMEM_LIMIT),
        name=name,
    )(xc, x, mod3, g, wgu, wd, fg)


def _regroup_kernel(w_ref, cg_ref, small_ref, plain_ref, *, cg_cols, small_cols, plain_cols):
    w = w_ref[...]
    take = lambda cols: [w[:, a:b] for a, b in cols]
    cg_ref[...] = jnp.concatenate(take(cg_cols), axis=1).astype(BF16)
    n_small = sum(b - a for a, b in small_cols)
    small_ref[...] = jnp.concatenate(take(small_cols) + [jnp.zeros((w.shape[0], LANE - n_small), F32)], axis=1)
    plain_ref[...] = jnp.concatenate(take(plain_cols), axis=1).astype(BF16)


def _regroup_w_in(wi, cg_cols, small_cols, plain_cols, tr=128):
    d, n_in = wi.shape
    width = lambda cols: sum(b - a for a, b in cols)
    row = lambda n: pl.BlockSpec((tr, n), lambda i: (i, 0))
    return pl.pallas_call(
        functools.partial(_regroup_kernel, cg_cols=cg_cols, small_cols=small_cols, plain_cols=plain_cols),
        out_shape=(jax.ShapeDtypeStruct((d, width(cg_cols)), BF16), jax.ShapeDtypeStruct((d, LANE), F32),
                   jax.ShapeDtypeStruct((d, width(plain_cols)), BF16)),
        grid=(d // tr,),
        in_specs=[row(n_in)],
        out_specs=(row(width(cg_cols)), row(LANE), row(width(plain_cols))),
        compiler_params=pltpu.CompilerParams(vmem_limit_bytes=VMEM_LIMIT),
        name="regroup_w_in",
    )(wi)


def _conv_shift_matrix(tm, period):
    t = np.arange(tm)[:, None]
    s = np.arange(tm)[None, :]
    half = CONV_K // 2
    mats = [(s == t + dlt) & (t // period == (t + dlt) // period)
            for dlt in range(-half, half + 1) if dlt]
    return np.concatenate(mats, axis=1).astype(np.float32)


CONV_SPLIT = 2
CONV_HALO = 16


def _conv_shift_parts(tm, period):
    taps = CONV_K - 1
    r = tm // CONV_SPLIT
    rest = _conv_shift_matrix(tm, period).reshape(tm, taps, tm)
    blocks = []
    for s in range(CONV_SPLIT):
        rows = slice(s * r, (s + 1) * r)
        blocks.append(rest[rows, :, rows].reshape(r, taps * r).copy())
        rest[rows, :, rows] = 0
    corr = [np.zeros((2 * CONV_HALO, taps * 2 * CONV_HALO), np.float32)]
    for b in range(1, CONV_SPLIT):
        win = slice(b * r - CONV_HALO, b * r + CONV_HALO)
        corr.append(rest[win, :, win].reshape(2 * CONV_HALO, taps * 2 * CONV_HALO).copy())
        rest[win, :, win] = 0
    assert not rest.any()
    return np.stack(blocks), np.stack(corr[-max(CONV_SPLIT - 1, 1):])


def _split_dot(a, b_ref):
    b = b_ref[...]
    a1 = a.astype(BF16)
    a2 = (a - a1.astype(F32)).astype(BF16)
    b1 = b.astype(BF16)
    b2 = (b - b1.astype(F32)).astype(BF16)
    dot = functools.partial(jnp.dot, preferred_element_type=F32)
    return dot(a1, b1) + (dot(a1, b2) + dot(a2, b1))


def _inproj_kernel(x_ref, mod_ref, g_ref, w_ref, ws_ref, cp_ref, sb_ref, sc_ref, cg_ref, sg_ref, sgt_ref, *,
                   blocks):
    x = x_ref[...]
    tm = x.shape[0]
    r = tm // CONV_SPLIT
    h = _modulated_norm(x, g_ref[...], mod_ref[0, 3:4, :], mod_ref[0, 4:5, :])
    hb = h.astype(BF16)
    small = _split_dot(h, ws_ref)
    sg_ref[...] = small
    sgt_ref[...] = small.T
    half = CONV_K // 2
    for c0, c1, kind in blocks:
        u = jnp.dot(hb, w_ref[:, c0:c1], preferred_element_type=F32)
        taps = [(u * cp_ref[half + dlt:half + dlt + 1, c0:c1]).astype(BF16)
                for dlt in range(-half, half + 1) if dlt]
        conv = [jnp.dot(sb_ref[0, s], jnp.concatenate([tp[s * r:(s + 1) * r] for tp in taps], axis=0),
                        preferred_element_type=F32) for s in range(CONV_SPLIT)]
        pieces = []
        for s in range(CONV_SPLIT):
            if s > 0:
                pieces[-1] = pieces[-1] + cross[:CONV_HALO]
                pieces.append(conv[s][:CONV_HALO] + cross[CONV_HALO:])
                pieces.append(conv[s][CONV_HALO:r - CONV_HALO] if s < CONV_SPLIT - 1 else conv[s][CONV_HALO:])
            else:
                pieces.append(conv[s][:r - CONV_HALO] if CONV_SPLIT > 1 else conv[s])
            if s < CONV_SPLIT - 1:
                pieces.append(conv[s][r - CONV_HALO:])
                b = (s + 1) * r
                cross = jnp.dot(sc_ref[0, s], jnp.concatenate([tp[b - CONV_HALO:b + CONV_HALO] for tp in taps],
                                                              axis=0), preferred_element_type=F32)
        acc = u * cp_ref[half:half + 1, c0:c1] + jnp.concatenate(pieces, axis=0)
        if kind == "x":
            acc = acc + cp_ref[CONV_K:CONV_K + 1, c0:c1]
        y = _silu(acc)
        if kind in ("q", "k"):
            parts = []
            for hh in range((c1 - c0) // GDN_D):
                yh = y[:, hh * GDN_D:(hh + 1) * GDN_D]
                inv = lax.rsqrt(jnp.sum(yh * yh, axis=-1, keepdims=True) + EPS)
                if kind == "q":
                    inv = inv * (GDN_D ** -0.5)
                parts.append(yh * inv)
            y = jnp.concatenate(parts, axis=1)
        cg_ref[:, c0:c1] = y


def _inproj(x1, mod3, g, w_cg, w_small, conv_p, *, tm, n_ctx_tiles, ctx_len, seq, blocks):
    t, d = x1.shape
    ncg = w_cg.shape[1]

    def row_of_tile(i):
        return jnp.where(i < n_ctx_tiles, 0, 1 + ((i - n_ctx_tiles) * tm) // seq)

    parts = [_conv_shift_parts(tm, ctx_len), _conv_shift_parts(tm, GRID_W)]
    sblk = jnp.asarray(np.stack([p[0] for p in parts]), dtype=BF16)
    scor = jnp.asarray(np.stack([p[1] for p in parts]), dtype=BF16)
    variant = lambda i: (jnp.where(i < n_ctx_tiles, 0, 1), 0, 0, 0)
    return pl.pallas_call(
        functools.partial(_inproj_kernel, blocks=blocks),
        out_shape=(jax.ShapeDtypeStruct((t, ncg), F32), jax.ShapeDtypeStruct((t, LANE), F32),
                   jax.ShapeDtypeStruct((LANE, t), F32)),
        grid=(t // tm,),
        in_specs=[pl.BlockSpec((tm, d), lambda i: (i, 0)),
                  pl.BlockSpec((1,) + mod3.shape[1:], lambda i: (row_of_tile(i), 0, 0)),
                  _const_spec(g.shape), _const_spec(w_cg.shape), _const_spec(w_small.shape),
                  _const_spec(conv_p.shape),
                  pl.BlockSpec((1,) + sblk.shape[1:], variant), pl.BlockSpec((1,) + scor.shape[1:], variant)],
        out_specs=(pl.BlockSpec((tm, ncg), lambda i: (i, 0)), pl.BlockSpec((tm, LANE), lambda i: (i, 0)),
                   pl.BlockSpec((LANE, tm), lambda i: (0, i))),
        compiler_params=pltpu.CompilerParams(vmem_limit_bytes=VMEM_LIMIT),
        name="inproj",
    )(x1, mod3, g, w_cg, w_small, conv_p, sblk, scor)


def _tri_masks(n, rev):
    ii = lax.broadcasted_iota(jnp.int32, (n, n), 0)
    jj = lax.broadcasted_iota(jnp.int32, (n, n), 1)
    if rev:
        return ii, jj, ii <= jj, ii < jj
    return ii, jj, ii >= jj, ii > jj


def _scan_block_index(b, s, rev, n_batch, ncc, nlc):
    lc = jnp.maximum(s - ncc, 0)
    if rev:
        ctx_blk = b * ncc + (ncc - 1 - jnp.minimum(s, ncc - 1))
        lat_blk = n_batch * ncc + b * nlc + (nlc - 1 - lc)
    else:
        ctx_blk = b * ncc + jnp.minimum(s, ncc - 1)
        lat_blk = n_batch * ncc + b * nlc + lc
    return jnp.where(s < ncc, ctx_blk, lat_blk)


def _scan_out_index(b, s, rev, ncc, nlc):
    lc = jnp.maximum(s - ncc, 0)
    return b * nlc + ((nlc - 1 - lc) if rev else lc)


def _gdn_kernel(qf, kf, vf, sgf, sgtf, qb, kb_, vb, sgb, sgtb, prow_ref, pcol_ref, of_ref, ob_ref, state_ref):
    @pl.when(pl.program_id(1) == 0)
    def _():
        state_ref[...] = jnp.zeros_like(state_ref)

    n = SCAN_CHUNK
    prow, pcol = prow_ref[...], pcol_ref[...]
    probs = []
    for d, rev, refs in ((0, False, (qf, kf, vf, sgf, sgtf, of_ref)), (1, True, (qb, kb_, vb, sgb, sgtb, ob_ref))):
        q, k, v, sg, sgt = [r[...] for r in refs[:5]]
        g_col = -jnp.exp(prow[0:1, :]) * _softplus(sg + prow[1:2, :])
        g_row = -jnp.exp(pcol[:, 0:1]) * _softplus(sgt + pcol[:, 1:2])
        beta_col = jax.nn.sigmoid(sg)
        ii, jj, incl, strict = _tri_masks(n, rev)
        mtri = incl.astype(F32)
        mtri_t = (ii >= jj if rev else ii <= jj).astype(F32)
        gc_col = jnp.dot(mtri, g_col, preferred_element_type=F32, precision=HIGHEST)
        gc_row = jnp.dot(g_row, mtri_t, preferred_element_type=F32, precision=HIGHEST)
        last = 0 if rev else n - 1
        for hh in range(GDN_HEADS):
            ci = d * GDN_HEADS + hh
            lo, hi = hh * GDN_D, (hh + 1) * GDN_D
            qh, kh, vh = q[:, lo:hi], k[:, lo:hi], v[:, lo:hi]
            gcol = gc_col[:, ci:ci + 1]
            grow = gc_row[ci:ci + 1, :]
            bcol = beta_col[:, N_DIR * GDN_HEADS + ci:N_DIR * GDN_HEADS + ci + 1]
            eg = jnp.exp(gcol)
            decay = jnp.where(incl, jnp.exp(jnp.where(incl, gcol - grow, 0.0)), 0.0)
            kb = kh * bcol
            g_last = gcol[last:last + 1, :]
            probs.append(dict(d=d, hh=hh, lo=lo, hi=hi, ii=ii, jj=jj, strict=strict, decay=decay, o_ref=refs[5],
                              lhs=jnp.concatenate([kb, qh], axis=0).astype(BF16), kh=kh.astype(BF16),
                              rhs=jnp.concatenate([vh * bcol, kb * eg], axis=1).astype(BF16),
                              q_in=(qh * eg).astype(BF16), k_out=(kh * jnp.exp(g_last - gcol)).astype(BF16),
                              g_tot=jnp.exp(g_last)))
    kk = [_bdot_nt(p["lhs"], p["kh"]) for p in probs]
    for p, kkp in zip(probs, kk):
        p["a"] = jnp.where(p["strict"], kkp[:n] * p["decay"], 0.0)
        p["qk"] = (kkp[n:] * p["decay"]).astype(BF16)
    lg = 3
    for p in probs:
        blk = (p["ii"] >> lg) == (p["jj"] >> lg)
        p["a8"] = jnp.where(blk, p["a"], 0.0).astype(BF16)
        p["t"] = (p["ii"] == p["jj"]).astype(F32) - jnp.where(blk, p["a"], 0.0)
    x2 = [_bdot(p["a8"], p["a8"]).astype(BF16) for p in probs]
    x4 = [_bdot(x, x).astype(BF16) for x in x2]
    t1 = [p["t"] + _bdot(p["t"], x) for p, x in zip(probs, x2)]
    ts = [t + _bdot(t, x) for t, x in zip(t1, x4)]
    while (1 << lg) < n:
        bs = []
        for p in probs:
            off = ((p["ii"] >> (lg + 1)) == (p["jj"] >> (lg + 1))) & ((p["ii"] >> lg) != (p["jj"] >> lg))
            bs.append(jnp.where(off, p["a"], 0.0).astype(BF16))
        tb = [t.astype(BF16) for t in ts]
        bt = [_bdot(b, t) for b, t in zip(bs, tb)]
        ts = [t - _bdot(t16, x) for t, t16, x in zip(ts, tb, bt)]
        lg += 1
    sol = [_bdot(t, p["rhs"]) for t, p in zip(ts, probs)]
    st = [state_ref[p["d"], p["hh"]] for p in probs]
    ws = [jnp.dot(jnp.concatenate([s[:, GDN_D:].astype(BF16), p["q_in"]], axis=0), sta.astype(BF16),
                  preferred_element_type=F32) for p, s, sta in zip(probs, sol, st)]
    v_new = [(s[:, :GDN_D] - w[:n]).astype(BF16) for s, w in zip(sol, ws)]
    for p, sta, w, vn in zip(probs, st, ws, v_new):
        p["o_ref"][:, p["lo"]:p["hi"]] = w[n:] + jnp.dot(p["qk"], vn, preferred_element_type=F32)
        state_ref[p["d"], p["hh"]] = sta * p["g_tot"] + lax.dot_general(
            p["k_out"], vn, (((0,), (0,)), ((), ())), preferred_element_type=F32)


def _gdn_scan(cg, sg, sgt, prow, pcol, *, n_batch, ncc, nlc, col_q):
    n = SCAN_CHUNK
    dq = GDN_HEADS * GDN_D
    cq = col_q // dq

    def tok(rev, col):
        return pl.BlockSpec((n, dq), lambda b, s: (_scan_block_index(b, s, rev, n_batch, ncc, nlc), col))

    def small(rev):
        return pl.BlockSpec((n, LANE), lambda b, s: (_scan_block_index(b, s, rev, n_batch, ncc, nlc), 0))

    def small_t(rev):
        return pl.BlockSpec((LANE, n), lambda b, s: (0, _scan_block_index(b, s, rev, n_batch, ncc, nlc)))

    def out(rev):
        return pl.BlockSpec((n, dq), lambda b, s: (_scan_out_index(b, s, rev, ncc, nlc), 0))

    in_specs = []
    for rev in (False, True):
        in_specs += [tok(rev, cq), tok(rev, cq + 1), tok(rev, cq + 2), small(rev), small_t(rev)]
    in_specs += [_const_spec(prow.shape), _const_spec(pcol.shape)]
    t_lat = n_batch * nlc * n
    return pl.pallas_call(
        _gdn_kernel,
        out_shape=(jax.ShapeDtypeStruct((t_lat, dq), F32),) * 2,
        grid=(n_batch, ncc + nlc),
        in_specs=in_specs,
        out_specs=(out(False), out(True)),
        scratch_shapes=[pltpu.VMEM((N_DIR, GDN_HEADS, GDN_D, GDN_D), F32)],
        compiler_params=pltpu.CompilerParams(vmem_limit_bytes=VMEM_LIMIT,
                                             dimension_semantics=("arbitrary", "arbitrary")),
        name="gdn_scan",
    )(cg, cg, cg, sg, sgt, cg, cg, cg, sg, sgt, prow, pcol)


def _expand(x, e, pieces):
    acc = None
    r = x
    for _ in range(pieces):
        xp = r.astype(BF16)
        r = r - xp.astype(F32)
        term = jnp.dot(xp, e, preferred_element_type=F32)
        acc = term if acc is None else acc + term
    return acc


def _ssd_dir(xs, bm, cm, sg, sgt, prow, pcol, ex, state_ref, y_ref, d, rev):
    n = SCAN_CHUNK
    n_heads = xs.shape[1] // MB_HEADDIM
    hpg = n_heads // MB_GROUPS
    gw = hpg * MB_HEADDIM
    ii, jj, incl, _ = _tri_masks(n, rev)
    mtri = incl.astype(F32)
    mtri_t = (ii <= jj if not rev else ii >= jj).astype(F32)
    dt_col = _softplus(sg + prow[3:4, :])
    dt_row = _softplus(sgt + pcol[:, 3:4])
    acs_col = jnp.dot(mtri, dt_col * -jnp.exp(prow[2:3, :]), preferred_element_type=F32, precision=HIGHEST)
    acs_row = jnp.dot(dt_row * -jnp.exp(pcol[:, 2:3]), mtri_t, preferred_element_type=F32, precision=HIGHEST)
    last = 0 if rev else n - 1
    tot = acs_col[last:last + 1, :]
    f0 = 2 * N_DIR * GDN_HEADS + d * n_heads
    heads = lambda a: a[:, f0:f0 + n_heads]
    e_in = _expand(heads(dt_col * jnp.exp(tot - acs_col)), ex, 2)
    e_out = _expand(heads(jnp.exp(acs_col)), ex, 2)
    e_tot = _expand(heads(jnp.broadcast_to(jnp.exp(tot), (8, LANE))), ex, 3)[0:1, :]
    xw = xs * e_in
    lane = lax.broadcasted_iota(jnp.int32, (n, 2 * MB_HEADDIM), 1)
    left = lane < MB_HEADDIM
    for g in range(MB_GROUPS):
        bg = bm[:, g * MB_STATE:(g + 1) * MB_STATE]
        cgm = cm[:, g * MB_STATE:(g + 1) * MB_STATE]
        cb = _bdot_nt(cgm, bg)
        st = state_ref[d, g]
        y_off = _bdot(cgm, st) * e_out[:, g * gw:(g + 1) * gw]
        state_ref[d, g] = st * e_tot[:, g * gw:(g + 1) * gw] + _bdot_tn(bg, xw[:, g * gw:(g + 1) * gw])
        for pr in range(hpg // 2):
            h0 = g * hpg + 2 * pr
            c0 = N_DIR * GDN_HEADS * 2 + d * n_heads + h0
            ms = []
            for hh in (0, 1):
                ccol = acs_col[:, c0 + hh:c0 + hh + 1]
                crow = acs_row[c0 + hh:c0 + hh + 1, :]
                seg = jnp.where(incl, jnp.exp(jnp.where(incl, ccol - crow, 0.0)), 0.0)
                ms.append(cb * seg * dt_row[c0 + hh:c0 + hh + 1, :])
            m2 = jnp.concatenate(ms, axis=1)
            lo = h0 * MB_HEADDIM
            xp = xs[:, lo:lo + 2 * MB_HEADDIM]
            xbd = jnp.concatenate([jnp.where(left, xp, 0.0), jnp.where(left, 0.0, xp)], axis=0)
            y_ref[:, lo:lo + 2 * MB_HEADDIM] = _bdot(m2, xbd) + y_off[:, lo - g * gw:lo - g * gw + 2 * MB_HEADDIM]


def _ssd_kernel(xf, bf, cf, sgf, sgtf, xb, bb, cb, sgb, sgtb, prow_ref, pcol_ref, ex_ref, yf_ref, yb_ref, state_ref):
    @pl.when(pl.program_id(1) == 0)
    def _():
        state_ref[...] = jnp.zeros_like(state_ref)

    prow, pcol = prow_ref[...], pcol_ref[...]
    _ssd_dir(xf[...], bf[...], cf[...], sgf[...], sgtf[...], prow, pcol, ex_ref[...], state_ref, yf_ref, 0, False)
    _ssd_dir(xb[...], bb[...], cb[...], sgb[...], sgtb[...], prow, pcol, ex_ref[...], state_ref, yb_ref, 1, True)


def _ssd_scan(cg, sg, sgt, prow, pcol, ex, *, n_batch, ncc, nlc, d_inner, col_b):
    n = SCAN_CHUNK
    bw = MB_GROUPS * MB_STATE
    cb_ = col_b // bw

    def blk(rev, width, col):
        return pl.BlockSpec((n, width), lambda b, s: (_scan_block_index(b, s, rev, n_batch, ncc, nlc), col))

    def small_t(rev):
        return pl.BlockSpec((LANE, n), lambda b, s: (0, _scan_block_index(b, s, rev, n_batch, ncc, nlc)))

    def out(rev):
        return pl.BlockSpec((n, d_inner), lambda b, s: (_scan_out_index(b, s, rev, ncc, nlc), 0))

    in_specs = []
    for rev in (False, True):
        in_specs += [blk(rev, d_inner, 0), blk(rev, bw, cb_), blk(rev, bw, cb_ + 1), blk(rev, LANE, 0), small_t(rev)]
    in_specs += [_const_spec(prow.shape), _const_spec(pcol.shape), _const_spec(ex.shape)]
    t_lat = n_batch * nlc * n
    return pl.pallas_call(
        _ssd_kernel,
        out_shape=(jax.ShapeDtypeStruct((t_lat, d_inner), F32),) * 2,
        grid=(n_batch, ncc + nlc),
        in_specs=in_specs,
        out_specs=(out(False), out(True)),
        scratch_shapes=[pltpu.VMEM((N_DIR, MB_GROUPS, MB_STATE, d_inner // MB_GROUPS), F32)],
        compiler_params=pltpu.CompilerParams(vmem_limit_bytes=VMEM_LIMIT,
                                             dimension_semantics=("arbitrary", "arbitrary")),
        name="ssd_scan",
    )(cg, cg, cg, sg, sgt, cg, cg, cg, sg, sgt, prow, pcol, ex)


def _mixout_kernel(x_ref, mod_ref, g_ref, wp_ref, of_ref, ob_ref, yf_ref, yb_ref, xs_ref,
                   rows_ref, wbg_ref, wbm_ref, wo_ref, o_ref, *, d_model, d_v, d_inner):
    x = x_ref[...]
    hb = _modulated_norm(x, g_ref[...], mod_ref[0, 3:4, :], mod_ref[0, 4:5, :]).astype(BF16)
    gate = mod_ref[0, 5:6, :]
    z = jnp.dot(hb, wp_ref[...], preferred_element_type=F32)
    za = z[:, :d_v]
    zb = z[:, d_v:d_v + d_inner]
    ga = z[:, d_v + d_inner:d_v + d_inner + d_model]
    gb = z[:, d_v + d_inner + d_model:]

    o = of_ref[...] + ob_ref[...]
    parts = []
    for hh in range(d_v // GDN_D):
        oh = o[:, hh * GDN_D:(hh + 1) * GDN_D]
        parts.append(oh * lax.rsqrt(jnp.mean(oh * oh, axis=-1, keepdims=True) + EPS))
    ya = jnp.concatenate(parts, axis=1) * rows_ref[0:1, :d_v] * _silu(za)

    yb = (yf_ref[...] + yb_ref[...] + rows_ref[1:2, :] * xs_ref[...]) * _silu(zb)
    gw = d_inner // MB_GROUPS
    parts = []
    for g in range(MB_GROUPS):
        yg = yb[:, g * gw:(g + 1) * gw]
        parts.append(yg * lax.rsqrt(jnp.mean(yg * yg, axis=-1, keepdims=True) + EPS))
    yb = jnp.concatenate(parts, axis=1) * rows_ref[2:3, :]

    merged = (jax.nn.sigmoid(ga) * _bdot(ya, wbg_ref[...]) + jax.nn.sigmoid(gb) * _bdot(yb, wbm_ref[...]))
    o_ref[...] = x + gate * _bdot(merged, wo_ref[...])


def _mixout(x1, mod3, g, w_plain, o_f, o_b, y_f, y_b, cg, rows, wbg, wbm, wo, *, tm, n_ctx_tiles, seq):
    t_lat, d_v = o_f.shape
    d_inner = y_f.shape[1]
    d = x1.shape[1]
    return pl.pallas_call(
        functools.partial(_mixout_kernel, d_model=d, d_v=d_v, d_inner=d_inner),
        out_shape=jax.ShapeDtypeStruct((t_lat, d), F32),
        grid=(t_lat // tm,),
        in_specs=[pl.BlockSpec((tm, d), lambda i: (i + n_ctx_tiles, 0)),
                  pl.BlockSpec((1,) + mod3.shape[1:], lambda i: (1 + (i * tm) // seq, 0, 0)),
                  _const_spec(g.shape), _const_spec(w_plain.shape),
                  pl.BlockSpec((tm, d_v), lambda i: (i, 0)), pl.BlockSpec((tm, d_v), lambda i: (i, 0)),
                  pl.BlockSpec((tm, d_inner), lambda i: (i, 0)), pl.BlockSpec((tm, d_inner), lambda i: (i, 0)),
                  pl.BlockSpec((tm, d_inner), lambda i: (i + n_ctx_tiles, 0)),
                  _const_spec(rows.shape), _const_spec(wbg.shape), _const_spec(wbm.shape), _const_spec(wo.shape)],
        out_specs=pl.BlockSpec((tm, d), lambda i: (i, 0)),
        compiler_params=pltpu.CompilerParams(vmem_limit_bytes=VMEM_LIMIT),
        name="mixout",
    )(x1, mod3, g, w_plain, o_f, o_b, y_f, y_b, cg, rows, wbg, wbm, wo)


def kernel(x, c, ctx, c_ctx, w_ada, b_ada, norm_g, ffn_w_gu, ffn_w_down, w_in, gdn_conv_w, gdn_A_log, gdn_dt_bias, gdn_norm_g, mb_conv_w, mb_conv_b, mb_A_log, mb_dt_bias, mb_D, mb_norm_g, w_branch_gdn, w_branch_mb, w_out, final_g):
    n_batch, seq, d = x.shape
    ctx_len = ctx.shape[1]
    assert w_ada.shape[0] == 1, "single-layer operation"
    d_qk = GDN_HEADS * GDN_D
    d_v = GDN_HEADS * GDN_D
    d_inner = mb_norm_g.shape[1]
    n_mb_heads = d_inner // MB_HEADDIM
    d_bc = MB_GROUPS * MB_STATE
    tm = 256
    t_ctx, t_lat = n_batch * ctx_len, n_batch * seq
    assert t_ctx % tm == 0 and seq % tm == 0 and tm % GRID_W == 0 and tm % ctx_len == 0
    assert ctx_len % SCAN_CHUNK == 0 and seq % SCAN_CHUNK == 0
    assert N_DIR * (2 * GDN_HEADS + n_mb_heads) <= LANE
    n_ctx_tiles = t_ctx // tm

    rows = 16
    cc = jnp.concatenate([c_ctx[None, :], c, jnp.zeros((rows - 1 - n_batch, d), F32)], axis=0)
    mod3 = _adaln(cc, w_ada[0], b_ada[0][None, :]).reshape(rows, 9, d)

    def row_all(i):
        return jnp.where(i < n_ctx_tiles, 0, 1 + ((i - n_ctx_tiles) * tm) // seq)

    ffn_gu, ffn_down = ffn_w_gu[0].astype(BF16), ffn_w_down[0].astype(BF16)
    x1 = _ffn(ctx.reshape(t_ctx, d), x.reshape(t_lat, d), mod3, norm_g[0, 0][None, :], ffn_gu, ffn_down,
              final_g[None, :], k=0, row_of_tile=row_all, tm=tm, final=False, name="ffn1")

    sizes = (2 * d_qk + d_v, d_v, N_DIR * GDN_HEADS, N_DIR * GDN_HEADS, d_inner, d_inner + 2 * d_bc,
             N_DIR * n_mb_heads)
    o_qkv, o_za, o_a, o_beta, o_zb, o_xbc, o_dt, o_gates = [sum(sizes[:j]) for j in range(len(sizes) + 1)]
    wi = w_in[0]

    def regroup(a, off_qkv, off_xbc):
        return jnp.concatenate([a[..., off_xbc:off_xbc + d_inner], a[..., off_qkv:off_qkv + 2 * d_qk + d_v],
                                a[..., off_xbc + d_inner:off_xbc + d_inner + 2 * d_bc]], axis=-1)

    n_small = N_DIR * (2 * GDN_HEADS + n_mb_heads)
    w_cg, w_small, w_plain = _regroup_w_in(
        wi,
        cg_cols=((o_xbc, o_xbc + d_inner), (o_qkv, o_za), (o_xbc + d_inner, o_dt)),
        small_cols=((o_a, o_zb), (o_dt, o_gates)),
        plain_cols=((o_za, o_a), (o_zb, o_xbc), (o_gates, o_gates + 2 * d)))
    conv_all = jnp.concatenate([gdn_conv_w[0], mb_conv_w[0]], axis=1)
    bias_all = jnp.concatenate([jnp.zeros((1, 2 * d_qk + d_v), F32), mb_conv_b[0][None, :]], axis=1)
    conv_p = regroup(jnp.concatenate([conv_all, bias_all, jnp.zeros((2, conv_all.shape[1]), F32)], axis=0),
                     0, 2 * d_qk + d_v)
    col_q = d_inner
    col_b = d_inner + 2 * d_qk + d_v
    blocks = ((0, d_inner // 2, "x"), (d_inner // 2, d_inner, "x"), (col_q, col_q + d_qk, "q"),
              (col_q + d_qk, col_q + 2 * d_qk, "k"), (col_q + 2 * d_qk, col_b, "v"), (col_b, col_b + 2 * d_bc, "x"))
    cg, sg, sgt = _inproj(x1, mod3, norm_g[0, 1][None, :], w_cg, w_small, conv_p, tm=tm, n_ctx_tiles=n_ctx_tiles,
                          ctx_len=ctx_len, seq=seq, blocks=blocks)

    pad = jnp.zeros((LANE - n_small,), F32)
    zeros_g = jnp.zeros((N_DIR * GDN_HEADS,), F32)
    zeros_m = jnp.zeros((N_DIR * n_mb_heads,), F32)
    prow = jnp.stack([jnp.concatenate([gdn_A_log[0].reshape(-1), zeros_g, zeros_m, pad]),
                      jnp.concatenate([gdn_dt_bias[0].reshape(-1), zeros_g, zeros_m, pad]),
                      jnp.concatenate([zeros_g, zeros_g, mb_A_log[0].reshape(-1), pad]),
                      jnp.concatenate([zeros_g, zeros_g, mb_dt_bias[0].reshape(-1), pad])]
                     + [jnp.zeros((LANE,), F32)] * 4)
    pcol = prow.T
    ncc, nlc = ctx_len // SCAN_CHUNK, seq // SCAN_CHUNK
    o_f, o_b = _gdn_scan(cg, sg, sgt, prow, pcol, n_batch=n_batch, ncc=ncc, nlc=nlc, col_q=col_q)

    ex = (jnp.arange(n_mb_heads)[:, None] == (jnp.arange(d_inner) // MB_HEADDIM)[None, :]).astype(BF16)
    y_f, y_b = _ssd_scan(cg, sg, sgt, prow, pcol, ex, n_batch=n_batch, ncc=ncc, nlc=nlc, d_inner=d_inner, col_b=col_b)

    rows3 = jnp.stack([jnp.concatenate([jnp.tile(gdn_norm_g[0], GDN_HEADS), jnp.zeros((d_inner - d_v,), F32)]),
                       jnp.repeat(mb_D[0], MB_HEADDIM), mb_norm_g[0]] + [jnp.zeros((d_inner,), F32)] * 5)
    x2 = _mixout(x1, mod3, norm_g[0, 1][None, :], w_plain, o_f, o_b, y_f, y_b, cg, rows3,
                 w_branch_gdn[0].astype(BF16), w_branch_mb[0].astype(BF16), w_out[0].astype(BF16),
                 tm=tm, n_ctx_tiles=n_ctx_tiles, seq=seq)

    out = _ffn(None, x2, mod3, norm_g[0, 2][None, :], ffn_gu, ffn_down, final_g[None, :], k=2, row_of_tile=lambda i: 1 + (i * tm) // seq, tm=tm, final=True, name="ffn2")
    return out.reshape(n_batch, seq, d)
```

```python
import functools

import jax
import jax.numpy as jnp
import numpy as np
from jax import lax
from jax.experimental import pallas as pl
from jax.experimental.pallas import tpu as pltpu

F32 = jnp.float32
BF16 = jnp.bfloat16
HIGHEST = lax.Precision.HIGHEST

EPS = 1e-6
FFN_RES = 0.5
GRID_W = 64
CONV_K = 5
GDN_HEADS = 8
GDN_D = 128
MB_HEADDIM = 64
MB_GROUPS = 2
MB_STATE = 128
N_DIR = 2

LANE = 128
SCAN_CHUNK = 128
VMEM_LIMIT = 56 * 1024 * 1024


def _const_spec(shape):
    nd = len(shape)
    return pl.BlockSpec(shape, lambda *_: (0,) * nd, pipeline_mode=pl.Buffered(1))


def _silu(x):
    return x * jax.nn.sigmoid(x)


def _softplus(x):
    return jnp.maximum(x, 0.0) + jnp.log(1.0 + jnp.exp(-jnp.abs(x)))


def _bdot(a, b):
    return jnp.dot(a.astype(BF16), b.astype(BF16), preferred_element_type=F32)


def _bdot_nt(a, b):
    return lax.dot_general(a.astype(BF16), b.astype(BF16), (((1,), (1,)), ((), ())),
                           preferred_element_type=F32)


def _bdot_tn(a, b):
    return lax.dot_general(a.astype(BF16), b.astype(BF16), (((0,), (0,)), ((), ())),
                           preferred_element_type=F32)


def _modulated_norm(x, g, shift, scale):
    ms = jnp.mean(x * x, axis=-1, keepdims=True)
    return x * lax.rsqrt(ms + EPS) * g * (1.0 + scale) + shift


def _adaln_kernel(c_ref, w_ref, b_ref, o_ref):
    s = _silu(c_ref[...])
    o_ref[...] = jnp.dot(s, w_ref[...], preferred_element_type=F32, precision=HIGHEST) + b_ref[...]


def _adaln(cc, w, b, tn=1152):
    rows, d = cc.shape
    n = w.shape[1]
    return pl.pallas_call(
        _adaln_kernel,
        out_shape=jax.ShapeDtypeStruct((rows, n), F32),
        grid=(n // tn,),
        in_specs=[pl.BlockSpec((rows, d), lambda j: (0, 0)),
                  pl.BlockSpec((d, tn), lambda j: (0, j)),
                  pl.BlockSpec((1, tn), lambda j: (0, j))],
        out_specs=pl.BlockSpec((rows, tn), lambda j: (0, j)),
        compiler_params=pltpu.CompilerParams(vmem_limit_bytes=VMEM_LIMIT),
        name="adaln",
    )(cc, w, b)


def _ffn_kernel(c_ref, x_ref, mod_ref, g_ref, wgu_ref, wd_ref, fg_ref, o_ref, *, k, d_ff, final, n_ctx_tiles):
    if n_ctx_tiles:
        x = jnp.where(pl.program_id(0) < n_ctx_tiles, c_ref[...], x_ref[...])
    else:
        x = x_ref[...]
    shift = mod_ref[0, 3 * k:3 * k + 1, :]
    scale = mod_ref[0, 3 * k + 1:3 * k + 2, :]
    gate = mod_ref[0, 3 * k + 2:3 * k + 3, :]
    hb = _modulated_norm(x, g_ref[...], shift, scale).astype(BF16)
    gt = jnp.dot(hb, wgu_ref[0, :, :d_ff], preferred_element_type=F32)
    up = jnp.dot(hb, wgu_ref[0, :, d_ff:], preferred_element_type=F32)
    act = (_silu(gt) * up).astype(BF16)
    y = x + FFN_RES * gate * jnp.dot(act, wd_ref[0], preferred_element_type=F32)
    if final:
        ms = jnp.mean(y * y, axis=-1, keepdims=True)
        y = y * lax.rsqrt(ms + EPS) * fg_ref[...]
    o_ref[...] = y


def _ffn(xc, x, mod3, g, wgu, wd, fg, *, k, row_of_tile, tm, final, name):
    t_lat, d = x.shape
    d_ff = wd.shape[1]
    which = k // 2
    w_spec = lambda w: pl.BlockSpec((1,) + w.shape[1:], lambda i: (which, 0, 0), pipeline_mode=pl.Buffered(1))
    if xc is None:
        nct = 0
        xc, c_spec = x, pl.BlockSpec((8, d), lambda i: (0, 0))
    else:
        nct = xc.shape[0] // tm
        c_spec = pl.BlockSpec((tm, d), lambda i: (jnp.minimum(i, nct - 1), 0))
    t = nct * tm + t_lat
    return pl.pallas_call(
        functools.partial(_ffn_kernel, k=k, d_ff=d_ff, final=final, n_ctx_tiles=nct),
        out_shape=jax.ShapeDtypeStruct((t, d), F32),
        grid=(t // tm,),
        in_specs=[c_spec, pl.BlockSpec((tm, d), lambda i: (jnp.maximum(i - nct, 0), 0)),
                  pl.BlockSpec((1,) + mod3.shape[1:], lambda i: (row_of_tile(i), 0, 0)),
                  _const_spec(g.shape), w_spec(wgu), w_spec(wd), _const_spec(fg.shape)],
        out_specs=pl.BlockSpec((tm, d), lambda i: (i, 0)),
        compiler_params=pltpu.CompilerParams(vmem_limit_bytes=VMEM_LIMIT),
        name=name,
    )(xc, x, mod3, g, wgu, wd, fg)


def _regroup_kernel(offs_ref, wt_ref, *rest, n_cg_blocks, n_small_in):
    del offs_ref
    small_in = rest[:n_small_in]
    cg_ref, small_ref, plain_ref = rest[n_small_in:]
    j = pl.program_id(0)
    blk = wt_ref[...].T.astype(BF16)

    @pl.when(j < n_cg_blocks)
    def _():
        cg_ref[...] = blk

    @pl.when(j >= n_cg_blocks)
    def _():
        plain_ref[...] = blk

    @pl.when(j == 0)
    def _():
        rows = [r[...] for r in small_in]
        pad = LANE - sum(r.shape[0] for r in rows)
        small_ref[...] = jnp.concatenate(rows + [jnp.zeros((pad, rows[0].shape[1]), F32)], axis=0).T


def _regroup_w_in(wt, cg_cols, small_cols, plain_cols, bw=512):
    d = wt.shape[1]
    width = lambda cols: sum(b - a for a, b in cols)
    assert all((b - a) % bw == 0 for a, b in cg_cols + plain_cols)
    starts = lambda cols: [s for a, b in cols for s in range(a, b, bw)]
    all_starts = starts(cg_cols) + starts(plain_cols)
    unit = 8
    assert all(s % unit == 0 for s in all_starts)
    offs = jnp.asarray([s // unit for s in all_starts], jnp.int32)
    ncg = len(starts(cg_cols))
    in_specs = [pl.BlockSpec((pl.Element(bw), pl.Element(d)), lambda j, offs: (offs[j] * unit, 0))]
    in_specs += [pl.BlockSpec((pl.Element(b - a), pl.Element(d)), lambda j, offs, a=a: (a, 0))
                 for a, b in small_cols]
    out_specs = (pl.BlockSpec((d, bw), lambda j, offs: (0, jnp.minimum(j, ncg - 1))),
                 pl.BlockSpec((d, LANE), lambda j, offs: (0, 0)),
                 pl.BlockSpec((d, bw), lambda j, offs: (0, jnp.maximum(j - ncg, 0))))
    return pl.pallas_call(
        functools.partial(_regroup_kernel, n_cg_blocks=ncg, n_small_in=len(small_cols)),
        out_shape=(jax.ShapeDtypeStruct((d, width(cg_cols)), BF16), jax.ShapeDtypeStruct((d, LANE), F32),
                   jax.ShapeDtypeStruct((d, width(plain_cols)), BF16)),
        grid_spec=pltpu.PrefetchScalarGridSpec(num_scalar_prefetch=1, grid=(offs.shape[0],),
                                               in_specs=in_specs, out_specs=out_specs),
        compiler_params=pltpu.CompilerParams(vmem_limit_bytes=VMEM_LIMIT,
                                             dimension_semantics=("arbitrary",)),
        name="regroup_w_in",
    )(offs, wt, *([wt] * len(small_cols)))


def _conv_shift_matrix(tm, period):
    t = np.arange(tm)[:, None]
    s = np.arange(tm)[None, :]
    half = CONV_K // 2
    mats = [(s == t + dlt) & (t // period == (t + dlt) // period)
            for dlt in range(-half, half + 1) if dlt]
    return np.concatenate(mats, axis=1).astype(np.float32)


CONV_SPLIT = 2
CONV_HALO = 16


def _conv_shift_parts(tm, period):
    taps = CONV_K - 1
    r = tm // CONV_SPLIT
    rest = _conv_shift_matrix(tm, period).reshape(tm, taps, tm)
    blocks = []
    for s in range(CONV_SPLIT):
        rows = slice(s * r, (s + 1) * r)
        blocks.append(rest[rows, :, rows].reshape(r, taps * r).copy())
        rest[rows, :, rows] = 0
    corr = [np.zeros((2 * CONV_HALO, taps * 2 * CONV_HALO), np.float32)]
    for b in range(1, CONV_SPLIT):
        win = slice(b * r - CONV_HALO, b * r + CONV_HALO)
        corr.append(rest[win, :, win].reshape(2 * CONV_HALO, taps * 2 * CONV_HALO).copy())
        rest[win, :, win] = 0
    assert not rest.any()
    return np.stack(blocks), np.stack(corr[-max(CONV_SPLIT - 1, 1):])


def _split_dot(a, b_ref):
    b = b_ref[...]
    a1 = a.astype(BF16)
    a2 = (a - a1.astype(F32)).astype(BF16)
    b1 = b.astype(BF16)
    b2 = (b - b1.astype(F32)).astype(BF16)
    dot = functools.partial(jnp.dot, preferred_element_type=F32)
    return dot(a1, b1) + (dot(a1, b2) + dot(a2, b1))


def _inproj_kernel(x_ref, mod_ref, g_ref, w_ref, ws_ref, cp_ref, sb_ref, sc_ref, cg_ref, sg_ref, sgt_ref, *,
                   blocks):
    x = x_ref[...]
    tm = x.shape[0]
    r = tm // CONV_SPLIT
    h = _modulated_norm(x, g_ref[...], mod_ref[0, 3:4, :], mod_ref[0, 4:5, :])
    hb = h.astype(BF16)
    small = _split_dot(h, ws_ref)
    sg_ref[...] = small
    sgt_ref[...] = small.T
    half = CONV_K // 2
    for c0, c1, kind in blocks:
        u = jnp.dot(hb, w_ref[:, c0:c1], preferred_element_type=F32)
        taps = [(u * cp_ref[half + dlt:half + dlt + 1, c0:c1]).astype(BF16)
                for dlt in range(-half, half + 1) if dlt]
        conv = [jnp.dot(sb_ref[0, s], jnp.concatenate([tp[s * r:(s + 1) * r] for tp in taps], axis=0),
                        preferred_element_type=F32) for s in range(CONV_SPLIT)]
        pieces = []
        for s in range(CONV_SPLIT):
            if s > 0:
                pieces[-1] = pieces[-1] + cross[:CONV_HALO]
                pieces.append(conv[s][:CONV_HALO] + cross[CONV_HALO:])
                pieces.append(conv[s][CONV_HALO:r - CONV_HALO] if s < CONV_SPLIT - 1 else conv[s][CONV_HALO:])
            else:
                pieces.append(conv[s][:r - CONV_HALO] if CONV_SPLIT > 1 else conv[s])
            if s < CONV_SPLIT - 1:
                pieces.append(conv[s][r - CONV_HALO:])
                b = (s + 1) * r
                cross = jnp.dot(sc_ref[0, s], jnp.concatenate([tp[b - CONV_HALO:b + CONV_HALO] for tp in taps],
                                                              axis=0), preferred_element_type=F32)
        acc = u * cp_ref[half:half + 1, c0:c1] + jnp.concatenate(pieces, axis=0)
        if kind == "x":
            acc = acc + cp_ref[CONV_K:CONV_K + 1, c0:c1]
        y = _silu(acc)
        if kind in ("q", "k"):
            parts = []
            for hh in range((c1 - c0) // GDN_D):
                yh = y[:, hh * GDN_D:(hh + 1) * GDN_D]
                inv = lax.rsqrt(jnp.sum(yh * yh, axis=-1, keepdims=True) + EPS)
                if kind == "q":
                    inv = inv * (GDN_D ** -0.5)
                parts.append(yh * inv)
            y = jnp.concatenate(parts, axis=1)
        cg_ref[:, c0:c1] = y


def _inproj(x1, mod3, g, w_cg, w_small, conv_p, *, tm, n_ctx_tiles, ctx_len, seq, blocks):
    t, d = x1.shape
    ncg = w_cg.shape[1]

    def row_of_tile(i):
        return jnp.where(i < n_ctx_tiles, 0, 1 + ((i - n_ctx_tiles) * tm) // seq)

    parts = [_conv_shift_parts(tm, ctx_len), _conv_shift_parts(tm, GRID_W)]
    sblk = jnp.asarray(np.stack([p[0] for p in parts]), dtype=BF16)
    scor = jnp.asarray(np.stack([p[1] for p in parts]), dtype=BF16)
    variant = lambda i: (jnp.where(i < n_ctx_tiles, 0, 1), 0, 0, 0)
    return pl.pallas_call(
        functools.partial(_inproj_kernel, blocks=blocks),
        out_shape=(jax.ShapeDtypeStruct((t, ncg), F32), jax.ShapeDtypeStruct((t, LANE), F32),
                   jax.ShapeDtypeStruct((LANE, t), F32)),
        grid=(t // tm,),
        in_specs=[pl.BlockSpec((tm, d), lambda i: (i, 0)),
                  pl.BlockSpec((1,) + mod3.shape[1:], lambda i: (row_of_tile(i), 0, 0)),
                  _const_spec(g.shape), _const_spec(w_cg.shape), _const_spec(w_small.shape),
                  _const_spec(conv_p.shape),
                  pl.BlockSpec((1,) + sblk.shape[1:], variant), pl.BlockSpec((1,) + scor.shape[1:], variant)],
        out_specs=(pl.BlockSpec((tm, ncg), lambda i: (i, 0)), pl.BlockSpec((tm, LANE), lambda i: (i, 0)),
                   pl.BlockSpec((LANE, tm), lambda i: (0, i))),
        compiler_params=pltpu.CompilerParams(vmem_limit_bytes=VMEM_LIMIT),
        name="inproj",
    )(x1, mod3, g, w_cg, w_small, conv_p, sblk, scor)


def _expand(x, e, pieces, e_left=False):
    acc = None
    r = x
    for _ in range(pieces):
        xp = r.astype(BF16)
        r = r - xp.astype(F32)
        term = jnp.dot(e, xp, preferred_element_type=F32) if e_left else jnp.dot(xp, e, preferred_element_type=F32)
        acc = term if acc is None else acc + term
    return acc


def _tri_masks(n, rev):
    ii = lax.broadcasted_iota(jnp.int32, (n, n), 0)
    jj = lax.broadcasted_iota(jnp.int32, (n, n), 1)
    if rev:
        return ii, jj, ii <= jj, ii < jj
    return ii, jj, ii >= jj, ii > jj


def _scan_block_index(b, s, rev, n_batch, ncc, nlc):
    lc = jnp.maximum(s - ncc, 0)
    if rev:
        ctx_blk = b * ncc + (ncc - 1 - jnp.minimum(s, ncc - 1))
        lat_blk = n_batch * ncc + b * nlc + (nlc - 1 - lc)
    else:
        ctx_blk = b * ncc + jnp.minimum(s, ncc - 1)
        lat_blk = n_batch * ncc + b * nlc + lc
    return jnp.where(s < ncc, ctx_blk, lat_blk)


def _scan_out_index(b, s, rev, ncc, nlc):
    lc = jnp.maximum(s - ncc, 0)
    return b * nlc + ((nlc - 1 - lc) if rev else lc)


def _gdn_kernel(qf, kf, vf, sgf, sgtf, qb, kb_, vb, sgb, sgtb, prow_ref, pcol_ref, of_ref, ob_ref, state_ref):
    @pl.when(pl.program_id(1) == 0)
    def _():
        state_ref[...] = jnp.zeros_like(state_ref)

    n = SCAN_CHUNK
    prow, pcol = prow_ref[...], pcol_ref[...]
    probs = []
    n_sub = qf.shape[0] // n
    for d, rev, refs, c in [(0, False, (qf, kf, vf, sgf, sgtf, of_ref), c) for c in range(n_sub)] + \
                           [(1, True, (qb, kb_, vb, sgb, sgtb, ob_ref), c) for c in range(n_sub)]:
        rows = slice(c * n, (c + 1) * n)
        q, k, v, sg = [r[rows, :] for r in refs[:4]]
        sgt = refs[4][:, rows]
        g_col = -jnp.exp(prow[0:1, :]) * _softplus(sg + prow[1:2, :])
        g_row = -jnp.exp(pcol[:, 0:1]) * _softplus(sgt + pcol[:, 1:2])
        beta_col = jax.nn.sigmoid(sg)
        ii, jj, incl, strict = _tri_masks(n, rev)
        gc_col = _expand(g_col, incl.astype(BF16), 3, e_left=True)
        gc_row = _expand(g_row, (ii >= jj if rev else ii <= jj).astype(BF16), 3)
        last = 0 if rev else n - 1
        for hh in range(GDN_HEADS):
            ci = d * GDN_HEADS + hh
            lo, hi = hh * GDN_D, (hh + 1) * GDN_D
            qh, kh, vh = q[:, lo:hi], k[:, lo:hi], v[:, lo:hi]
            gcol = gc_col[:, ci:ci + 1]
            grow = gc_row[ci:ci + 1, :]
            bcol = beta_col[:, N_DIR * GDN_HEADS + ci:N_DIR * GDN_HEADS + ci + 1]
            eg = jnp.exp(gcol)
            decay = jnp.where(incl, jnp.exp(jnp.where(incl, gcol - grow, 0.0)), 0.0)
            kb = kh * bcol
            g_last = gcol[last:last + 1, :]
            probs.append(dict(d=d, hh=hh, lo=lo, hi=hi, ii=ii, jj=jj, strict=strict, decay=decay, o_ref=refs[5],
                              rows=rows, order=(n_sub - 1 - c) if rev else c,
                              lhs=jnp.concatenate([kb, qh], axis=0).astype(BF16), kh=kh.astype(BF16),
                              rhs=jnp.concatenate([vh * bcol, kb * eg], axis=1).astype(BF16),
                              q_in=(qh * eg).astype(BF16), k_out=(kh * jnp.exp(g_last - gcol)).astype(BF16),
                              g_tot=jnp.exp(g_last)))
    kk = [_bdot_nt(p["lhs"], p["kh"]) for p in probs]
    for p, kkp in zip(probs, kk):
        p["a"] = jnp.where(p["strict"], kkp[:n] * p["decay"], 0.0)
        p["qk"] = (kkp[n:] * p["decay"]).astype(BF16)
    lg = 3
    for p in probs:
        blk = (p["ii"] >> lg) == (p["jj"] >> lg)
        p["a8"] = jnp.where(blk, p["a"], 0.0).astype(BF16)
        p["t"] = (p["ii"] == p["jj"]).astype(F32) - jnp.where(blk, p["a"], 0.0)
    x2 = [_bdot(p["a8"], p["a8"]).astype(BF16) for p in probs]
    x4 = [_bdot(x, x).astype(BF16) for x in x2]
    t1 = [p["t"] + _bdot(p["t"], x) for p, x in zip(probs, x2)]
    ts = [t + _bdot(t, x) for t, x in zip(t1, x4)]
    while (1 << lg) < n:
        s_blk = 1 << lg

        def pick(x, rev):
            return jnp.concatenate([x[b0 + (0 if rev else s_blk):b0 + (s_blk if rev else 2 * s_blk)]
                                    for b0 in range(0, n, 2 * s_blk)], axis=0)

        def merge(full, part, rev):
            rows = []
            for idx, b0 in enumerate(range(0, n, 2 * s_blk)):
                new = part[idx * s_blk:(idx + 1) * s_blk]
                old = (jnp.zeros((s_blk, n), F32) if full is None else
                       full[b0 + (s_blk if rev else 0):b0 + (2 * s_blk if rev else s_blk)])
                rows += [new, old] if rev else [old, new]
            return jnp.concatenate(rows, axis=0)

        bs = []
        for p in probs:
            rev = p["d"] == 1
            r = lax.broadcasted_iota(jnp.int32, (n // 2, n), 0)
            pi = ((r >> lg) << (lg + 1)) + (r & (s_blk - 1)) + (0 if rev else s_blk)
            pj = lax.broadcasted_iota(jnp.int32, (n // 2, n), 1)
            off = ((pi >> (lg + 1)) == (pj >> (lg + 1))) & ((pi >> lg) != (pj >> lg))
            bs.append(jnp.where(off, pick(p["a"], rev), 0.0).astype(BF16))
        tb = [t.astype(BF16) for t in ts]
        bt = [_bdot(b, t) for b, t in zip(bs, tb)]
        bt_full = [merge(None, x, p["d"] == 1).astype(BF16) for x, p in zip(bt, probs)]
        upd = [_bdot(pick(t, p["d"] == 1), x) for t, x, p in zip(ts, bt_full, probs)]
        ts = [merge(t, pick(t, p["d"] == 1) - u, p["d"] == 1) for t, u, p in zip(ts, upd, probs)]
        lg += 1
    sol = [_bdot(t, p["rhs"]) for t, p in zip(ts, probs)]
    state = {(d, hh): state_ref[d, hh] for d in range(N_DIR) for hh in range(GDN_HEADS)}
    for order in range(n_sub):
        cur = [(p, s) for p, s in zip(probs, sol) if p["order"] == order]
        stb = [state[p["d"], p["hh"]].astype(BF16) for p, _ in cur]
        ws = [jnp.dot(s[:, GDN_D:].astype(BF16), sb, preferred_element_type=F32) for (_, s), sb in zip(cur, stb)]
        v_new = [(s[:, :GDN_D] - w).astype(BF16) for (_, s), w in zip(cur, ws)]
        for (p, _), sb, vn in zip(cur, stb, v_new):
            p["o_ref"][p["rows"], p["lo"]:p["hi"]] = jnp.dot(
                jnp.concatenate([p["q_in"], p["qk"]], axis=1), jnp.concatenate([sb, vn], axis=0),
                preferred_element_type=F32)
            state[p["d"], p["hh"]] = state[p["d"], p["hh"]] * p["g_tot"] + lax.dot_general(
                p["k_out"], vn, (((0,), (0,)), ((), ())), preferred_element_type=F32)
    for (d, hh), sta in state.items():
        state_ref[d, hh] = sta


GDN_CHUNKS_PER_STEP = 2


def _gdn_scan(cg, sg, sgt, prow, pcol, *, n_batch, ncc, nlc, col_q):
    assert ncc % GDN_CHUNKS_PER_STEP == 0 and nlc % GDN_CHUNKS_PER_STEP == 0
    n = SCAN_CHUNK * GDN_CHUNKS_PER_STEP
    ncc, nlc = ncc // GDN_CHUNKS_PER_STEP, nlc // GDN_CHUNKS_PER_STEP
    dq = GDN_HEADS * GDN_D
    cq = col_q // dq

    def tok(rev, col):
        return pl.BlockSpec((n, dq), lambda b, s: (_scan_block_index(b, s, rev, n_batch, ncc, nlc), col))

    def small(rev):
        return pl.BlockSpec((n, LANE), lambda b, s: (_scan_block_index(b, s, rev, n_batch, ncc, nlc), 0))

    def small_t(rev):
        return pl.BlockSpec((LANE, n), lambda b, s: (0, _scan_block_index(b, s, rev, n_batch, ncc, nlc)))

    def out(rev):
        return pl.BlockSpec((n, dq), lambda b, s: (_scan_out_index(b, s, rev, ncc, nlc), 0))

    in_specs = []
    for rev in (False, True):
        in_specs += [tok(rev, cq), tok(rev, cq + 1), tok(rev, cq + 2), small(rev), small_t(rev)]
    in_specs += [_const_spec(prow.shape), _const_spec(pcol.shape)]
    t_lat = n_batch * nlc * n
    return pl.pallas_call(
        _gdn_kernel,
        out_shape=(jax.ShapeDtypeStruct((t_lat, dq), F32),) * 2,
        grid=(n_batch, ncc + nlc),
        in_specs=in_specs,
        out_specs=(out(False), out(True)),
        scratch_shapes=[pltpu.VMEM((N_DIR, GDN_HEADS, GDN_D, GDN_D), F32)],
        compiler_params=pltpu.CompilerParams(vmem_limit_bytes=VMEM_LIMIT,
                                             dimension_semantics=("arbitrary", "arbitrary")),
        name="gdn_scan",
    )(cg, cg, cg, sg, sgt, cg, cg, cg, sg, sgt, prow, pcol)


def _ssd_dir(xs, bm, cm, sg, sgt, prow, pcol, ex, state_ref, y_ref, d, rev):
    n = SCAN_CHUNK
    n_heads = xs.shape[1] // MB_HEADDIM
    hpg = n_heads // MB_GROUPS
    gw = hpg * MB_HEADDIM
    ii, jj, incl, _ = _tri_masks(n, rev)
    dt_col = _softplus(sg + prow[3:4, :])
    dt_row = _softplus(sgt + pcol[:, 3:4])
    acs_col = _expand(dt_col * -jnp.exp(prow[2:3, :]), incl.astype(BF16), 3, e_left=True)
    acs_row = _expand(dt_row * -jnp.exp(pcol[:, 2:3]), (ii >= jj if rev else ii <= jj).astype(BF16), 3)
    last = 0 if rev else n - 1
    tot = acs_col[last:last + 1, :]
    f0 = 2 * N_DIR * GDN_HEADS + d * n_heads
    heads = lambda a: a[:, f0:f0 + n_heads]
    e_in = _expand(heads(dt_col * jnp.exp(tot - acs_col)), ex, 2)
    e_out = _expand(heads(jnp.exp(acs_col)), ex, 2)
    e_tot = _expand(heads(jnp.broadcast_to(jnp.exp(tot), (8, LANE))), ex, 3)[0:1, :]
    xw = xs * e_in
    lane = lax.broadcasted_iota(jnp.int32, (n, 2 * MB_HEADDIM), 1)
    left = lane < MB_HEADDIM
    for g in range(MB_GROUPS):
        bg = bm[:, g * MB_STATE:(g + 1) * MB_STATE]
        cgm = cm[:, g * MB_STATE:(g + 1) * MB_STATE]
        cb = _bdot_nt(cgm, bg)
        st = state_ref[d, g]
        y_off = _bdot(cgm, st) * e_out[:, g * gw:(g + 1) * gw]
        state_ref[d, g] = st * e_tot[:, g * gw:(g + 1) * gw] + _bdot_tn(bg, xw[:, g * gw:(g + 1) * gw])
        for pr in range(hpg // 2):
            h0 = g * hpg + 2 * pr
            c0 = N_DIR * GDN_HEADS * 2 + d * n_heads + h0
            ms = []
            for hh in (0, 1):
                ccol = acs_col[:, c0 + hh:c0 + hh + 1]
                crow = acs_row[c0 + hh:c0 + hh + 1, :]
                seg = jnp.where(incl, jnp.exp(jnp.where(incl, ccol - crow, 0.0)), 0.0)
                ms.append(cb * seg * dt_row[c0 + hh:c0 + hh + 1, :])
            m2 = jnp.concatenate(ms, axis=1)
            lo = h0 * MB_HEADDIM
            xp = xs[:, lo:lo + 2 * MB_HEADDIM]
            xbd = jnp.concatenate([jnp.where(left, xp, 0.0), jnp.where(left, 0.0, xp)], axis=0)
            y_ref[:, lo:lo + 2 * MB_HEADDIM] = _bdot(m2, xbd) + y_off[:, lo - g * gw:lo - g * gw + 2 * MB_HEADDIM]


def _ssd_kernel(xf, bf, cf, sgf, sgtf, xb, bb, cb, sgb, sgtb, prow_ref, pcol_ref, ex_ref, yf_ref, yb_ref, state_ref):
    @pl.when(pl.program_id(1) == 0)
    def _():
        state_ref[...] = jnp.zeros_like(state_ref)

    prow, pcol = prow_ref[...], pcol_ref[...]
    _ssd_dir(xf[...], bf[...], cf[...], sgf[...], sgtf[...], prow, pcol, ex_ref[...], state_ref, yf_ref, 0, False)
    _ssd_dir(xb[...], bb[...], cb[...], sgb[...], sgtb[...], prow, pcol, ex_ref[...], state_ref, yb_ref, 1, True)


def _ssd_scan(cg, sg, sgt, prow, pcol, ex, *, n_batch, ncc, nlc, d_inner, col_b):
    n = SCAN_CHUNK
    bw = MB_GROUPS * MB_STATE
    cb_ = col_b // bw

    def blk(rev, width, col):
        return pl.BlockSpec((n, width), lambda b, s: (_scan_block_index(b, s, rev, n_batch, ncc, nlc), col))

    def small_t(rev):
        return pl.BlockSpec((LANE, n), lambda b, s: (0, _scan_block_index(b, s, rev, n_batch, ncc, nlc)))

    def out(rev):
        return pl.BlockSpec((n, d_inner), lambda b, s: (_scan_out_index(b, s, rev, ncc, nlc), 0))

    in_specs = []
    for rev in (False, True):
        in_specs += [blk(rev, d_inner, 0), blk(rev, bw, cb_), blk(rev, bw, cb_ + 1), blk(rev, LANE, 0), small_t(rev)]
    in_specs += [_const_spec(prow.shape), _const_spec(pcol.shape), _const_spec(ex.shape)]
    t_lat = n_batch * nlc * n
    return pl.pallas_call(
        _ssd_kernel,
        out_shape=(jax.ShapeDtypeStruct((t_lat, d_inner), F32),) * 2,
        grid=(n_batch, ncc + nlc),
        in_specs=in_specs,
        out_specs=(out(False), out(True)),
        scratch_shapes=[pltpu.VMEM((N_DIR, MB_GROUPS, MB_STATE, d_inner // MB_GROUPS), F32)],
        compiler_params=pltpu.CompilerParams(vmem_limit_bytes=VMEM_LIMIT,
                                             dimension_semantics=("arbitrary", "arbitrary")),
        name="ssd_scan",
    )(cg, cg, cg, sg, sgt, cg, cg, cg, sg, sgt, prow, pcol, ex)


def _mixout_kernel(x_ref, mod_ref, g_ref, wp_ref, of_ref, ob_ref, yf_ref, yb_ref, xs_ref,
                   rows_ref, wbg_ref, wbm_ref, wo_ref, o_ref, *, d_model, d_v, d_inner):
    x = x_ref[...]
    hb = _modulated_norm(x, g_ref[...], mod_ref[0, 3:4, :], mod_ref[0, 4:5, :]).astype(BF16)
    gate = mod_ref[0, 5:6, :]
    z = jnp.dot(hb, wp_ref[...], preferred_element_type=F32)
    za = z[:, :d_v]
    zb = z[:, d_v:d_v + d_inner]
    ga = z[:, d_v + d_inner:d_v + d_inner + d_model]
    gb = z[:, d_v + d_inner + d_model:]

    o = of_ref[...] + ob_ref[...]
    parts = []
    for hh in range(d_v // GDN_D):
        oh = o[:, hh * GDN_D:(hh + 1) * GDN_D]
        parts.append(oh * lax.rsqrt(jnp.mean(oh * oh, axis=-1, keepdims=True) + EPS))
    ya = jnp.concatenate(parts, axis=1) * rows_ref[0:1, :d_v] * _silu(za)

    yb = (yf_ref[...] + yb_ref[...] + rows_ref[1:2, :] * xs_ref[...]) * _silu(zb)
    gw = d_inner // MB_GROUPS
    parts = []
    for g in range(MB_GROUPS):
        yg = yb[:, g * gw:(g + 1) * gw]
        parts.append(yg * lax.rsqrt(jnp.mean(yg * yg, axis=-1, keepdims=True) + EPS))
    yb = jnp.concatenate(parts, axis=1) * rows_ref[2:3, :]

    merged = (jax.nn.sigmoid(ga) * _bdot(ya, wbg_ref[...]) + jax.nn.sigmoid(gb) * _bdot(yb, wbm_ref[...]))
    o_ref[...] = x + gate * _bdot(merged, wo_ref[...])


def _mixout(x1, mod3, g, w_plain, o_f, o_b, y_f, y_b, cg, rows, wbg, wbm, wo, *, tm, n_ctx_tiles, seq):
    t_lat, d_v = o_f.shape
    d_inner = y_f.shape[1]
    d = x1.shape[1]
    return pl.pallas_call(
        functools.partial(_mixout_kernel, d_model=d, d_v=d_v, d_inner=d_inner),
        out_shape=jax.ShapeDtypeStruct((t_lat, d), F32),
        grid=(t_lat // tm,),
        in_specs=[pl.BlockSpec((tm, d), lambda i: (i + n_ctx_tiles, 0)),
                  pl.BlockSpec((1,) + mod3.shape[1:], lambda i: (1 + (i * tm) // seq, 0, 0)),
                  _const_spec(g.shape), _const_spec(w_plain.shape),
                  pl.BlockSpec((tm, d_v), lambda i: (i, 0)), pl.BlockSpec((tm, d_v), lambda i: (i, 0)),
                  pl.BlockSpec((tm, d_inner), lambda i: (i, 0)), pl.BlockSpec((tm, d_inner), lambda i: (i, 0)),
                  pl.BlockSpec((tm, d_inner), lambda i: (i + n_ctx_tiles, 0)),
                  _const_spec(rows.shape), _const_spec(wbg.shape), _const_spec(wbm.shape), _const_spec(wo.shape)],
        out_specs=pl.BlockSpec((tm, d), lambda i: (i, 0)),
        compiler_params=pltpu.CompilerParams(vmem_limit_bytes=VMEM_LIMIT),
        name="mixout",
    )(x1, mod3, g, w_plain, o_f, o_b, y_f, y_b, cg, rows, wbg, wbm, wo)


def kernel(x, c, ctx, c_ctx, w_ada, b_ada, norm_g, ffn_w_gu, ffn_w_down, w_in, gdn_conv_w, gdn_A_log, gdn_dt_bias, gdn_norm_g, mb_conv_w, mb_conv_b, mb_A_log, mb_dt_bias, mb_D, mb_norm_g, w_branch_gdn, w_branch_mb, w_out, final_g):
    n_batch, seq, d = x.shape
    ctx_len = ctx.shape[1]
    assert w_ada.shape[0] == 1, "single-layer operation"
    d_qk = GDN_HEADS * GDN_D
    d_v = GDN_HEADS * GDN_D
    d_inner = mb_norm_g.shape[1]
    n_mb_heads = d_inner // MB_HEADDIM
    d_bc = MB_GROUPS * MB_STATE
    tm = 256
    t_ctx, t_lat = n_batch * ctx_len, n_batch * seq
    assert t_ctx % tm == 0 and seq % tm == 0 and tm % GRID_W == 0 and tm % ctx_len == 0
    assert ctx_len % SCAN_CHUNK == 0 and seq % SCAN_CHUNK == 0
    assert N_DIR * (2 * GDN_HEADS + n_mb_heads) <= LANE
    n_ctx_tiles = t_ctx // tm

    rows = 16
    cc = jnp.concatenate([c_ctx[None, :], c, jnp.zeros((rows - 1 - n_batch, d), F32)], axis=0)
    mod3 = _adaln(cc, w_ada[0], b_ada[0][None, :]).reshape(rows, 9, d)

    def row_all(i):
        return jnp.where(i < n_ctx_tiles, 0, 1 + ((i - n_ctx_tiles) * tm) // seq)

    ffn_gu, ffn_down = ffn_w_gu[0].astype(BF16), ffn_w_down[0].astype(BF16)
    x1 = _ffn(ctx.reshape(t_ctx, d), x.reshape(t_lat, d), mod3, norm_g[0, 0][None, :], ffn_gu, ffn_down,
              final_g[None, :], k=0, row_of_tile=row_all, tm=tm, final=False, name="ffn1")

    sizes = (2 * d_qk + d_v, d_v, N_DIR * GDN_HEADS, N_DIR * GDN_HEADS, d_inner, d_inner + 2 * d_bc,
             N_DIR * n_mb_heads)
    o_qkv, o_za, o_a, o_beta, o_zb, o_xbc, o_dt, o_gates = [sum(sizes[:j]) for j in range(len(sizes) + 1)]

    def regroup(a, off_qkv, off_xbc):
        return jnp.concatenate([a[..., off_xbc:off_xbc + d_inner], a[..., off_qkv:off_qkv + 2 * d_qk + d_v],
                                a[..., off_xbc + d_inner:off_xbc + d_inner + 2 * d_bc]], axis=-1)

    n_small = N_DIR * (2 * GDN_HEADS + n_mb_heads)
    w_cg, w_small, w_plain = _regroup_w_in(
        w_in[0].T,
        cg_cols=((o_xbc, o_xbc + d_inner), (o_qkv, o_za), (o_xbc + d_inner, o_dt)),
        small_cols=((o_a, o_zb), (o_dt, o_gates)),
        plain_cols=((o_za, o_a), (o_zb, o_xbc), (o_gates, o_gates + 2 * d)))
    conv_all = jnp.concatenate([gdn_conv_w[0], mb_conv_w[0]], axis=1)
    bias_all = jnp.concatenate([jnp.zeros((1, 2 * d_qk + d_v), F32), mb_conv_b[0][None, :]], axis=1)
    conv_p = regroup(jnp.concatenate([conv_all, bias_all, jnp.zeros((2, conv_all.shape[1]), F32)], axis=0),
                     0, 2 * d_qk + d_v)
    col_q = d_inner
    col_b = d_inner + 2 * d_qk + d_v
    blocks = ((0, d_inner // 2, "x"), (d_inner // 2, d_inner, "x"), (col_q, col_q + d_qk, "q"),
              (col_q + d_qk, col_q + 2 * d_qk, "k"), (col_q + 2 * d_qk, col_b, "v"), (col_b, col_b + 2 * d_bc, "x"))
    cg, sg, sgt = _inproj(x1, mod3, norm_g[0, 1][None, :], w_cg, w_small, conv_p, tm=tm, n_ctx_tiles=n_ctx_tiles,
                          ctx_len=ctx_len, seq=seq, blocks=blocks)

    pad = jnp.zeros((LANE - n_small,), F32)
    zeros_g = jnp.zeros((N_DIR * GDN_HEADS,), F32)
    zeros_m = jnp.zeros((N_DIR * n_mb_heads,), F32)
    prow = jnp.stack([jnp.concatenate([gdn_A_log[0].reshape(-1), zeros_g, zeros_m, pad]),
                      jnp.concatenate([gdn_dt_bias[0].reshape(-1), zeros_g, zeros_m, pad]),
                      jnp.concatenate([zeros_g, zeros_g, mb_A_log[0].reshape(-1), pad]),
                      jnp.concatenate([zeros_g, zeros_g, mb_dt_bias[0].reshape(-1), pad])]
                     + [jnp.zeros((LANE,), F32)] * 4)
    pcol = prow.T
    ncc, nlc = ctx_len // SCAN_CHUNK, seq // SCAN_CHUNK
    o_f, o_b = _gdn_scan(cg, sg, sgt, prow, pcol, n_batch=n_batch, ncc=ncc, nlc=nlc, col_q=col_q)

    ex = (jnp.arange(n_mb_heads)[:, None] == (jnp.arange(d_inner) // MB_HEADDIM)[None, :]).astype(BF16)
    y_f, y_b = _ssd_scan(cg, sg, sgt, prow, pcol, ex, n_batch=n_batch, ncc=ncc, nlc=nlc, d_inner=d_inner, col_b=col_b)

    rows3 = jnp.stack([jnp.concatenate([jnp.tile(gdn_norm_g[0], GDN_HEADS), jnp.zeros((d_inner - d_v,), F32)]),
                       jnp.repeat(mb_D[0], MB_HEADDIM), mb_norm_g[0]] + [jnp.zeros((d_inner,), F32)] * 5)
    x2 = _mixout(x1, mod3, norm_g[0, 1][None, :], w_plain, o_f, o_b, y_f, y_b, cg, rows3,
                 w_branch_gdn[0].astype(BF16), w_branch_mb[0].astype(BF16), w_out[0].astype(BF16),
                 tm=tm, n_ctx_tiles=n_ctx_tiles, seq=seq)

    out = _ffn(None, x2, mod3, norm_g[0, 2][None, :], ffn_gu, ffn_down, final_g[None, :], k=2, row_of_tile=lambda i: 1 + (i * tm) // seq, tm=tm, final=True, name="ffn2")
    return out.reshape(n_batch, seq, d)
```

```python
import functools

import jax
import jax.numpy as jnp
import numpy as np
from jax import lax
from jax.experimental import pallas as pl
from jax.experimental.pallas import tpu as pltpu

F32 = jnp.float32
BF16 = jnp.bfloat16
HIGHEST = lax.Precision.HIGHEST

EPS = 1e-6
FFN_RES = 0.5
GRID_W = 64
CONV_K = 5
GDN_HEADS = 8
GDN_D = 128
MB_HEADDIM = 64
MB_GROUPS = 2
MB_STATE = 128
N_DIR = 2

LANE = 128
SCAN_CHUNK = 128
VMEM_LIMIT = 56 * 1024 * 1024


def _const_spec(shape):
    nd = len(shape)
    return pl.BlockSpec(shape, lambda *_: (0,) * nd, pipeline_mode=pl.Buffered(1))


def _silu(x):
    return x * jax.nn.sigmoid(x)


def _softplus(x):
    return jnp.maximum(x, 0.0) + jnp.log(1.0 + jnp.exp(-jnp.abs(x)))


def _bdot(a, b):
    return jnp.dot(a.astype(BF16), b.astype(BF16), preferred_element_type=F32)


def _bdot_nt(a, b):
    return lax.dot_general(a.astype(BF16), b.astype(BF16), (((1,), (1,)), ((), ())),
                           preferred_element_type=F32)


def _bdot_tn(a, b):
    return lax.dot_general(a.astype(BF16), b.astype(BF16), (((0,), (0,)), ((), ())),
                           preferred_element_type=F32)


def _modulated_norm(x, g, shift, scale):
    ms = jnp.mean(x * x, axis=-1, keepdims=True)
    return x * lax.rsqrt(ms + EPS) * g * (1.0 + scale) + shift


def _adaln_kernel(c_ref, w_ref, b_ref, o_ref):
    s = _silu(c_ref[...])
    o_ref[...] = jnp.dot(s, w_ref[...], preferred_element_type=F32, precision=HIGHEST) + b_ref[...]


def _adaln(cc, w, b, tn=1152):
    rows, d = cc.shape
    n = w.shape[1]
    return pl.pallas_call(
        _adaln_kernel,
        out_shape=jax.ShapeDtypeStruct((rows, n), F32),
        grid=(n // tn,),
        in_specs=[pl.BlockSpec((rows, d), lambda j: (0, 0)),
                  pl.BlockSpec((d, tn), lambda j: (0, j)),
                  pl.BlockSpec((1, tn), lambda j: (0, j))],
        out_specs=pl.BlockSpec((rows, tn), lambda j: (0, j)),
        compiler_params=pltpu.CompilerParams(vmem_limit_bytes=VMEM_LIMIT),
        name="adaln",
    )(cc, w, b)


def _ffn_kernel(c_ref, x_ref, mod_ref, g_ref, wgu_ref, wd_ref, fg_ref, o_ref, *, k, d_ff, final, n_ctx_tiles):
    if n_ctx_tiles:
        x = jnp.where(pl.program_id(0) < n_ctx_tiles, c_ref[...], x_ref[...])
    else:
        x = x_ref[...]
    shift = mod_ref[0, 3 * k:3 * k + 1, :]
    scale = mod_ref[0, 3 * k + 1:3 * k + 2, :]
    gate = mod_ref[0, 3 * k + 2:3 * k + 3, :]
    hb = _modulated_norm(x, g_ref[...], shift, scale).astype(BF16)
    gt = jnp.dot(hb, wgu_ref[0, :, :d_ff], preferred_element_type=F32)
    up = jnp.dot(hb, wgu_ref[0, :, d_ff:], preferred_element_type=F32)
    act = (_silu(gt) * up).astype(BF16)
    y = x + FFN_RES * gate * jnp.dot(act, wd_ref[0], preferred_element_type=F32)
    if final:
        ms = jnp.mean(y * y, axis=-1, keepdims=True)
        y = y * lax.rsqrt(ms + EPS) * fg_ref[...]
    o_ref[...] = y


def _ffn(xc, x, mod3, g, wgu, wd, fg, *, k, row_of_tile, tm, final, name):
    t_lat, d = x.shape
    d_ff = wd.shape[1]
    which = k // 2
    w_spec = lambda w: pl.BlockSpec((1,) + w.shape[1:], lambda i: (which, 0, 0), pipeline_mode=pl.Buffered(1))
    if xc is None:
        nct = 0
        xc, c_spec = x, pl.BlockSpec((8, d), lambda i: (0, 0))
    else:
        nct = xc.shape[0] // tm
        c_spec = pl.BlockSpec((tm, d), lambda i: (jnp.minimum(i, nct - 1), 0))
    t = nct * tm + t_lat
    return pl.pallas_call(
        functools.partial(_ffn_kernel, k=k, d_ff=d_ff, final=final, n_ctx_tiles=nct),
        out_shape=jax.ShapeDtypeStruct((t, d), F32),
        grid=(t // tm,),
        in_specs=[c_spec, pl.BlockSpec((tm, d), lambda i: (jnp.maximum(i - nct, 0), 0)),
                  pl.BlockSpec((1,) + mod3.shape[1:], lambda i: (row_of_tile(i), 0, 0)),
                  _const_spec(g.shape), w_spec(wgu), w_spec(wd), _const_spec(fg.shape)],
        out_specs=pl.BlockSpec((tm, d), lambda i: (i, 0)),
        compiler_params=pltpu.CompilerParams(vmem_limit_bytes=VMEM_LIMIT),
        name=name,
    )(xc, x, mod3, g, wgu, wd, fg)


def _regroup_kernel(offs_ref, wt_ref, *rest, n_cg_blocks, n_small_in):
    del offs_ref
    small_in = rest[:n_small_in]
    cg_ref, small_ref, plain_ref = rest[n_small_in:]
    j = pl.program_id(0)
    blk = wt_ref[...].T.astype(BF16)

    @pl.when(j < n_cg_blocks)
    def _():
        cg_ref[...] = blk

    @pl.when(j >= n_cg_blocks)
    def _():
        plain_ref[...] = blk

    @pl.when(j == 0)
    def _():
        rows = [r[...] for r in small_in]
        pad = LANE - sum(r.shape[0] for r in rows)
        small_ref[...] = jnp.concatenate(rows + [jnp.zeros((pad, rows[0].shape[1]), F32)], axis=0).T


def _regroup_w_in(wt, cg_cols, small_cols, plain_cols, bw=512):
    d = wt.shape[1]
    width = lambda cols: sum(b - a for a, b in cols)
    assert all((b - a) % bw == 0 for a, b in cg_cols + plain_cols)
    starts = lambda cols: [s for a, b in cols for s in range(a, b, bw)]
    all_starts = starts(cg_cols) + starts(plain_cols)
    unit = 8
    assert all(s % unit == 0 for s in all_starts)
    offs = jnp.asarray([s // unit for s in all_starts], jnp.int32)
    ncg = len(starts(cg_cols))
    in_specs = [pl.BlockSpec((pl.Element(bw), pl.Element(d)), lambda j, offs: (offs[j] * unit, 0))]
    in_specs += [pl.BlockSpec((pl.Element(b - a), pl.Element(d)), lambda j, offs, a=a: (a, 0))
                 for a, b in small_cols]
    out_specs = (pl.BlockSpec((d, bw), lambda j, offs: (0, jnp.minimum(j, ncg - 1))),
                 pl.BlockSpec((d, LANE), lambda j, offs: (0, 0)),
                 pl.BlockSpec((d, bw), lambda j, offs: (0, jnp.maximum(j - ncg, 0))))
    return pl.pallas_call(
        functools.partial(_regroup_kernel, n_cg_blocks=ncg, n_small_in=len(small_cols)),
        out_shape=(jax.ShapeDtypeStruct((d, width(cg_cols)), BF16), jax.ShapeDtypeStruct((d, LANE), F32),
                   jax.ShapeDtypeStruct((d, width(plain_cols)), BF16)),
        grid_spec=pltpu.PrefetchScalarGridSpec(num_scalar_prefetch=1, grid=(offs.shape[0],),
                                               in_specs=in_specs, out_specs=out_specs),
        compiler_params=pltpu.CompilerParams(vmem_limit_bytes=VMEM_LIMIT,
                                             dimension_semantics=("arbitrary",)),
        name="regroup_w_in",
    )(offs, wt, *([wt] * len(small_cols)))


def _conv_shift_matrix(tm, period):
    t = np.arange(tm)[:, None]
    s = np.arange(tm)[None, :]
    half = CONV_K // 2
    mats = [(s == t + dlt) & (t // period == (t + dlt) // period)
            for dlt in range(-half, half + 1) if dlt]
    return np.concatenate(mats, axis=1).astype(np.float32)


CONV_SPLIT = 2
CONV_HALO = 16


def _conv_shift_parts(tm, period):
    taps = CONV_K - 1
    r = tm // CONV_SPLIT
    rest = _conv_shift_matrix(tm, period).reshape(tm, taps, tm)
    blocks = []
    for s in range(CONV_SPLIT):
        rows = slice(s * r, (s + 1) * r)
        blocks.append(rest[rows, :, rows].reshape(r, taps * r).copy())
        rest[rows, :, rows] = 0
    corr = [np.zeros((2 * CONV_HALO, taps * 2 * CONV_HALO), np.float32)]
    for b in range(1, CONV_SPLIT):
        win = slice(b * r - CONV_HALO, b * r + CONV_HALO)
        corr.append(rest[win, :, win].reshape(2 * CONV_HALO, taps * 2 * CONV_HALO).copy())
        rest[win, :, win] = 0
    assert not rest.any()
    return np.stack(blocks), np.stack(corr[-max(CONV_SPLIT - 1, 1):])


def _split_dot(a, b_ref):
    b = b_ref[...]
    a1 = a.astype(BF16)
    a2 = (a - a1.astype(F32)).astype(BF16)
    b1 = b.astype(BF16)
    b2 = (b - b1.astype(F32)).astype(BF16)
    dot = functools.partial(jnp.dot, preferred_element_type=F32)
    return dot(a1, b1) + (dot(a1, b2) + dot(a2, b1))


def _inproj_kernel(x_ref, mod_ref, g_ref, w_ref, ws_ref, cp_ref, sb_ref, sc_ref, cg_ref, sg_ref, sgt_ref, *,
                   blocks):
    x = x_ref[...]
    tm = x.shape[0]
    r = tm // CONV_SPLIT
    h = _modulated_norm(x, g_ref[...], mod_ref[0, 3:4, :], mod_ref[0, 4:5, :])
    hb = h.astype(BF16)
    small = _split_dot(h, ws_ref)
    sg_ref[...] = small
    sgt_ref[...] = small.T
    half = CONV_K // 2
    for c0, c1, kind in blocks:
        u = jnp.dot(hb, w_ref[:, c0:c1], preferred_element_type=F32)
        taps = [(u * cp_ref[half + dlt:half + dlt + 1, c0:c1]).astype(BF16)
                for dlt in range(-half, half + 1) if dlt]
        conv = [jnp.dot(sb_ref[0, s], jnp.concatenate([tp[s * r:(s + 1) * r] for tp in taps], axis=0),
                        preferred_element_type=F32) for s in range(CONV_SPLIT)]
        pieces = []
        for s in range(CONV_SPLIT):
            if s > 0:
                pieces[-1] = pieces[-1] + cross[:CONV_HALO]
                pieces.append(conv[s][:CONV_HALO] + cross[CONV_HALO:])
                pieces.append(conv[s][CONV_HALO:r - CONV_HALO] if s < CONV_SPLIT - 1 else conv[s][CONV_HALO:])
            else:
                pieces.append(conv[s][:r - CONV_HALO] if CONV_SPLIT > 1 else conv[s])
            if s < CONV_SPLIT - 1:
                pieces.append(conv[s][r - CONV_HALO:])
                b = (s + 1) * r
                cross = jnp.dot(sc_ref[0, s], jnp.concatenate([tp[b - CONV_HALO:b + CONV_HALO] for tp in taps],
                                                              axis=0), preferred_element_type=F32)
        acc = u * cp_ref[half:half + 1, c0:c1] + jnp.concatenate(pieces, axis=0)
        if kind == "x":
            acc = acc + cp_ref[CONV_K:CONV_K + 1, c0:c1]
        y = _silu(acc)
        if kind in ("q", "k"):
            parts = []
            for hh in range((c1 - c0) // GDN_D):
                yh = y[:, hh * GDN_D:(hh + 1) * GDN_D]
                inv = lax.rsqrt(jnp.sum(yh * yh, axis=-1, keepdims=True) + EPS)
                if kind == "q":
                    inv = inv * (GDN_D ** -0.5)
                parts.append(yh * inv)
            y = jnp.concatenate(parts, axis=1)
        cg_ref[:, c0:c1] = y


def _inproj(x1, mod3, g, w_cg, w_small, conv_p, *, tm, n_ctx_tiles, ctx_len, seq, blocks):
    t, d = x1.shape
    ncg = w_cg.shape[1]

    def row_of_tile(i):
        return jnp.where(i < n_ctx_tiles, 0, 1 + ((i - n_ctx_tiles) * tm) // seq)

    parts = [_conv_shift_parts(tm, ctx_len), _conv_shift_parts(tm, GRID_W)]
    sblk = jnp.asarray(np.stack([p[0] for p in parts]), dtype=BF16)
    scor = jnp.asarray(np.stack([p[1] for p in parts]), dtype=BF16)
    variant = lambda i: (jnp.where(i < n_ctx_tiles, 0, 1), 0, 0, 0)
    return pl.pallas_call(
        functools.partial(_inproj_kernel, blocks=blocks),
        out_shape=(jax.ShapeDtypeStruct((t, ncg), F32), jax.ShapeDtypeStruct((t, LANE), F32),
                   jax.ShapeDtypeStruct((LANE, t), F32)),
        grid=(t // tm,),
        in_specs=[pl.BlockSpec((tm, d), lambda i: (i, 0)),
                  pl.BlockSpec((1,) + mod3.shape[1:], lambda i: (row_of_tile(i), 0, 0)),
                  _const_spec(g.shape), _const_spec(w_cg.shape), _const_spec(w_small.shape),
                  _const_spec(conv_p.shape),
                  pl.BlockSpec((1,) + sblk.shape[1:], variant), pl.BlockSpec((1,) + scor.shape[1:], variant)],
        out_specs=(pl.BlockSpec((tm, ncg), lambda i: (i, 0)), pl.BlockSpec((tm, LANE), lambda i: (i, 0)),
                   pl.BlockSpec((LANE, tm), lambda i: (0, i))),
        compiler_params=pltpu.CompilerParams(vmem_limit_bytes=VMEM_LIMIT),
        name="inproj",
    )(x1, mod3, g, w_cg, w_small, conv_p, sblk, scor)


def _expand(x, e, pieces, e_left=False):
    acc = None
    r = x
    for _ in range(pieces):
        xp = r.astype(BF16)
        r = r - xp.astype(F32)
        term = jnp.dot(e, xp, preferred_element_type=F32) if e_left else jnp.dot(xp, e, preferred_element_type=F32)
        acc = term if acc is None else acc + term
    return acc


def _tri_masks(n, rev):
    ii = lax.broadcasted_iota(jnp.int32, (n, n), 0)
    jj = lax.broadcasted_iota(jnp.int32, (n, n), 1)
    if rev:
        return ii, jj, ii <= jj, ii < jj
    return ii, jj, ii >= jj, ii > jj


def _scan_block_index(b, s, rev, n_batch, ncc, nlc):
    lc = jnp.maximum(s - ncc, 0)
    if rev:
        ctx_blk = b * ncc + (ncc - 1 - jnp.minimum(s, ncc - 1))
        lat_blk = n_batch * ncc + b * nlc + (nlc - 1 - lc)
    else:
        ctx_blk = b * ncc + jnp.minimum(s, ncc - 1)
        lat_blk = n_batch * ncc + b * nlc + lc
    return jnp.where(s < ncc, ctx_blk, lat_blk)


def _scan_out_index(b, s, rev, ncc, nlc):
    lc = jnp.maximum(s - ncc, 0)
    return b * nlc + ((nlc - 1 - lc) if rev else lc)


def _gdn_kernel(qf, kf, vf, sgf, sgtf, qb, kb_, vb, sgb, sgtb, prow_ref, pcol_ref, of_ref, ob_ref, state_ref):
    @pl.when(pl.program_id(1) == 0)
    def _():
        state_ref[...] = jnp.zeros_like(state_ref)

    n = SCAN_CHUNK
    prow, pcol = prow_ref[...], pcol_ref[...]
    probs = []
    n_sub = qf.shape[0] // n
    for d, rev, refs, c in [(0, False, (qf, kf, vf, sgf, sgtf, of_ref), c) for c in range(n_sub)] + \
                           [(1, True, (qb, kb_, vb, sgb, sgtb, ob_ref), c) for c in range(n_sub)]:
        rows = slice(c * n, (c + 1) * n)
        q, k, v, sg = [r[rows, :] for r in refs[:4]]
        sgt = refs[4][:, rows]
        g_col = -jnp.exp(prow[0:1, :]) * _softplus(sg + prow[1:2, :])
        g_row = -jnp.exp(pcol[:, 0:1]) * _softplus(sgt + pcol[:, 1:2])
        beta_col = jax.nn.sigmoid(sg)
        ii, jj, incl, strict = _tri_masks(n, rev)
        gc_col = _expand(g_col, incl.astype(BF16), 3, e_left=True)
        gc_row = _expand(g_row, (ii >= jj if rev else ii <= jj).astype(BF16), 3)
        last = 0 if rev else n - 1
        for hh in range(GDN_HEADS):
            ci = d * GDN_HEADS + hh
            lo, hi = hh * GDN_D, (hh + 1) * GDN_D
            qh, kh, vh = q[:, lo:hi], k[:, lo:hi], v[:, lo:hi]
            gcol = gc_col[:, ci:ci + 1]
            grow = gc_row[ci:ci + 1, :]
            bcol = beta_col[:, N_DIR * GDN_HEADS + ci:N_DIR * GDN_HEADS + ci + 1]
            eg = jnp.exp(gcol)
            decay = jnp.where(incl, jnp.exp(jnp.where(incl, gcol - grow, 0.0)), 0.0)
            kb = kh * bcol
            g_last = gcol[last:last + 1, :]
            probs.append(dict(d=d, hh=hh, lo=lo, hi=hi, ii=ii, jj=jj, strict=strict, decay=decay, o_ref=refs[5],
                              rows=rows, order=(n_sub - 1 - c) if rev else c,
                              lhs=jnp.concatenate([kb, qh], axis=0).astype(BF16), kh=kh.astype(BF16),
                              rhs=jnp.concatenate([vh * bcol, kb * eg], axis=1).astype(BF16),
                              q_in=(qh * eg).astype(BF16), k_out=(kh * jnp.exp(g_last - gcol)).astype(BF16),
                              g_tot=jnp.exp(g_last)))
    kk = [_bdot_nt(p["lhs"], p["kh"]) for p in probs]
    for p, kkp in zip(probs, kk):
        p["a"] = jnp.where(p["strict"], kkp[:n] * p["decay"], 0.0)
        p["qk"] = (kkp[n:] * p["decay"]).astype(BF16)
    lg = 3
    for p in probs:
        blk = (p["ii"] >> lg) == (p["jj"] >> lg)
        p["a8"] = jnp.where(blk, p["a"], 0.0).astype(BF16)
        p["t"] = (p["ii"] == p["jj"]).astype(F32) - jnp.where(blk, p["a"], 0.0)
    x2 = [_bdot(p["a8"], p["a8"]).astype(BF16) for p in probs]
    x4 = [_bdot(x, x).astype(BF16) for x in x2]
    t1 = [p["t"] + _bdot(p["t"], x) for p, x in zip(probs, x2)]
    ts = [t + _bdot(t, x) for t, x in zip(t1, x4)]
    while (1 << lg) < n:
        s_blk = 1 << lg

        def pick(x, rev):
            return jnp.concatenate([x[b0 + (0 if rev else s_blk):b0 + (s_blk if rev else 2 * s_blk)]
                                    for b0 in range(0, n, 2 * s_blk)], axis=0)

        def merge(full, part, rev):
            rows = []
            for idx, b0 in enumerate(range(0, n, 2 * s_blk)):
                new = part[idx * s_blk:(idx + 1) * s_blk]
                old = (jnp.zeros((s_blk, n), F32) if full is None else
                       full[b0 + (s_blk if rev else 0):b0 + (2 * s_blk if rev else s_blk)])
                rows += [new, old] if rev else [old, new]
            return jnp.concatenate(rows, axis=0)

        bs = []
        for p in probs:
            rev = p["d"] == 1
            r = lax.broadcasted_iota(jnp.int32, (n // 2, n), 0)
            pi = ((r >> lg) << (lg + 1)) + (r & (s_blk - 1)) + (0 if rev else s_blk)
            pj = lax.broadcasted_iota(jnp.int32, (n // 2, n), 1)
            off = ((pi >> (lg + 1)) == (pj >> (lg + 1))) & ((pi >> lg) != (pj >> lg))
            bs.append(jnp.where(off, pick(p["a"], rev), 0.0).astype(BF16))
        tb = [t.astype(BF16) for t in ts]
        bt = [_bdot(b, t) for b, t in zip(bs, tb)]
        bt_full = [merge(None, x, p["d"] == 1).astype(BF16) for x, p in zip(bt, probs)]
        upd = [_bdot(pick(t, p["d"] == 1), x) for t, x, p in zip(ts, bt_full, probs)]
        ts = [merge(t, pick(t, p["d"] == 1) - u, p["d"] == 1) for t, u, p in zip(ts, upd, probs)]
        lg += 1
    sol = [_bdot(t, p["rhs"]) for t, p in zip(ts, probs)]
    state = {(d, hh): state_ref[d, hh] for d in range(N_DIR) for hh in range(GDN_HEADS)}
    for order in range(n_sub):
        cur = [(p, s) for p, s in zip(probs, sol) if p["order"] == order]
        stb = [state[p["d"], p["hh"]].astype(BF16) for p, _ in cur]
        ws = [jnp.dot(s[:, GDN_D:].astype(BF16), sb, preferred_element_type=F32) for (_, s), sb in zip(cur, stb)]
        v_new = [(s[:, :GDN_D] - w).astype(BF16) for (_, s), w in zip(cur, ws)]
        for (p, _), sb, vn in zip(cur, stb, v_new):
            p["o_ref"][p["rows"], p["lo"]:p["hi"]] = jnp.dot(
                jnp.concatenate([p["q_in"], p["qk"]], axis=1), jnp.concatenate([sb, vn], axis=0),
                preferred_element_type=F32)
            state[p["d"], p["hh"]] = state[p["d"], p["hh"]] * p["g_tot"] + lax.dot_general(
                p["k_out"], vn, (((0,), (0,)), ((), ())), preferred_element_type=F32)
    for (d, hh), sta in state.items():
        state_ref[d, hh] = sta


GDN_CHUNKS_PER_STEP = 2


def _gdn_scan(cg, sg, sgt, prow, pcol, *, n_batch, ncc, nlc, col_q):
    assert ncc % GDN_CHUNKS_PER_STEP == 0 and nlc % GDN_CHUNKS_PER_STEP == 0
    n = SCAN_CHUNK * GDN_CHUNKS_PER_STEP
    ncc, nlc = ncc // GDN_CHUNKS_PER_STEP, nlc // GDN_CHUNKS_PER_STEP
    dq = GDN_HEADS * GDN_D
    cq = col_q // dq

    def tok(rev, col):
        return pl.BlockSpec((n, dq), lambda b, s: (_scan_block_index(b, s, rev, n_batch, ncc, nlc), col))

    def small(rev):
        return pl.BlockSpec((n, LANE), lambda b, s: (_scan_block_index(b, s, rev, n_batch, ncc, nlc), 0))

    def small_t(rev):
        return pl.BlockSpec((LANE, n), lambda b, s: (0, _scan_block_index(b, s, rev, n_batch, ncc, nlc)))

    def out(rev):
        return pl.BlockSpec((n, dq), lambda b, s: (_scan_out_index(b, s, rev, ncc, nlc), 0))

    in_specs = []
    for rev in (False, True):
        in_specs += [tok(rev, cq), tok(rev, cq + 1), tok(rev, cq + 2), small(rev), small_t(rev)]
    in_specs += [_const_spec(prow.shape), _const_spec(pcol.shape)]
    t_lat = n_batch * nlc * n
    return pl.pallas_call(
        _gdn_kernel,
        out_shape=(jax.ShapeDtypeStruct((t_lat, dq), F32),) * 2,
        grid=(n_batch, ncc + nlc),
        in_specs=in_specs,
        out_specs=(out(False), out(True)),
        scratch_shapes=[pltpu.VMEM((N_DIR, GDN_HEADS, GDN_D, GDN_D), F32)],
        compiler_params=pltpu.CompilerParams(vmem_limit_bytes=VMEM_LIMIT,
                                             dimension_semantics=("arbitrary", "arbitrary")),
        name="gdn_scan",
    )(cg, cg, cg, sg, sgt, cg, cg, cg, sg, sgt, prow, pcol)


def _ssd_dir(xs, bm, cm, sg, sgt, prow, pcol, ex, state_ref, y_ref, d, rev):
    n = SCAN_CHUNK
    n_heads = xs.shape[1] // MB_HEADDIM
    hpg = n_heads // MB_GROUPS
    gw = hpg * MB_HEADDIM
    ii, jj, incl, _ = _tri_masks(n, rev)
    dt_col = _softplus(sg + prow[3:4, :])
    dt_row = _softplus(sgt + pcol[:, 3:4])
    acs_col = _expand(dt_col * -jnp.exp(prow[2:3, :]), incl.astype(BF16), 3, e_left=True)
    acs_row = _expand(dt_row * -jnp.exp(pcol[:, 2:3]), (ii >= jj if rev else ii <= jj).astype(BF16), 3)
    last = 0 if rev else n - 1
    tot = acs_col[last:last + 1, :]
    f0 = 2 * N_DIR * GDN_HEADS + d * n_heads
    heads = lambda a: a[:, f0:f0 + n_heads]
    e_in = _expand(heads(dt_col * jnp.exp(tot - acs_col)), ex, 1)
    e_out = _expand(heads(jnp.exp(acs_col)), ex, 1)
    e_tot = _expand(heads(jnp.broadcast_to(jnp.exp(tot), (8, LANE))), ex, 3)[0:1, :]
    xw = xs * e_in
    lane = lax.broadcasted_iota(jnp.int32, (n, 2 * MB_HEADDIM), 1)
    left = lane < MB_HEADDIM
    for g in range(MB_GROUPS):
        bg = bm[:, g * MB_STATE:(g + 1) * MB_STATE]
        cgm = cm[:, g * MB_STATE:(g + 1) * MB_STATE]
        cb = _bdot_nt(cgm, bg)
        st = state_ref[d, g]
        y_off = _bdot(cgm, st) * e_out[:, g * gw:(g + 1) * gw]
        state_ref[d, g] = st * e_tot[:, g * gw:(g + 1) * gw] + _bdot_tn(bg, xw[:, g * gw:(g + 1) * gw])
        for pr in range(hpg // 2):
            h0 = g * hpg + 2 * pr
            c0 = N_DIR * GDN_HEADS * 2 + d * n_heads + h0
            ms = []
            for hh in (0, 1):
                ccol = acs_col[:, c0 + hh:c0 + hh + 1]
                crow = acs_row[c0 + hh:c0 + hh + 1, :]
                seg = jnp.where(incl, jnp.exp(jnp.where(incl, ccol - crow, 0.0)), 0.0)
                ms.append(cb * seg * dt_row[c0 + hh:c0 + hh + 1, :])
            m2 = jnp.concatenate(ms, axis=1)
            lo = h0 * MB_HEADDIM
            xp = xs[:, lo:lo + 2 * MB_HEADDIM]
            xbd = jnp.concatenate([jnp.where(left, xp, 0.0), jnp.where(left, 0.0, xp)], axis=0)
            y_ref[:, lo:lo + 2 * MB_HEADDIM] = _bdot(m2, xbd) + y_off[:, lo - g * gw:lo - g * gw + 2 * MB_HEADDIM]


def _ssd_kernel(xf, bf, cf, sgf, sgtf, xb, bb, cb, sgb, sgtb, prow_ref, pcol_ref, ex_ref, yf_ref, yb_ref, state_ref):
    @pl.when(pl.program_id(1) == 0)
    def _():
        state_ref[...] = jnp.zeros_like(state_ref)

    prow, pcol = prow_ref[...], pcol_ref[...]
    _ssd_dir(xf[...], bf[...], cf[...], sgf[...], sgtf[...], prow, pcol, ex_ref[...], state_ref, yf_ref, 0, False)
    _ssd_dir(xb[...], bb[...], cb[...], sgb[...], sgtb[...], prow, pcol, ex_ref[...], state_ref, yb_ref, 1, True)


def _ssd_scan(cg, sg, sgt, prow, pcol, ex, *, n_batch, ncc, nlc, d_inner, col_b):
    n = SCAN_CHUNK
    bw = MB_GROUPS * MB_STATE
    cb_ = col_b // bw

    def blk(rev, width, col):
        return pl.BlockSpec((n, width), lambda b, s: (_scan_block_index(b, s, rev, n_batch, ncc, nlc), col))

    def small_t(rev):
        return pl.BlockSpec((LANE, n), lambda b, s: (0, _scan_block_index(b, s, rev, n_batch, ncc, nlc)))

    def out(rev):
        return pl.BlockSpec((n, d_inner), lambda b, s: (_scan_out_index(b, s, rev, ncc, nlc), 0))

    in_specs = []
    for rev in (False, True):
        in_specs += [blk(rev, d_inner, 0), blk(rev, bw, cb_), blk(rev, bw, cb_ + 1), blk(rev, LANE, 0), small_t(rev)]
    in_specs += [_const_spec(prow.shape), _const_spec(pcol.shape), _const_spec(ex.shape)]
    t_lat = n_batch * nlc * n
    return pl.pallas_call(
        _ssd_kernel,
        out_shape=(jax.ShapeDtypeStruct((t_lat, d_inner), F32),) * 2,
        grid=(n_batch, ncc + nlc),
        in_specs=in_specs,
        out_specs=(out(False), out(True)),
        scratch_shapes=[pltpu.VMEM((N_DIR, MB_GROUPS, MB_STATE, d_inner // MB_GROUPS), F32)],
        compiler_params=pltpu.CompilerParams(vmem_limit_bytes=VMEM_LIMIT,
                                             dimension_semantics=("arbitrary", "arbitrary")),
        name="ssd_scan",
    )(cg, cg, cg, sg, sgt, cg, cg, cg, sg, sgt, prow, pcol, ex)


def _mixout_kernel(x_ref, mod_ref, g_ref, wp_ref, of_ref, ob_ref, yf_ref, yb_ref, xs_ref,
                   rows_ref, wbg_ref, wbm_ref, wo_ref, o_ref, *, d_model, d_v, d_inner):
    x = x_ref[...]
    hb = _modulated_norm(x, g_ref[...], mod_ref[0, 3:4, :], mod_ref[0, 4:5, :]).astype(BF16)
    gate = mod_ref[0, 5:6, :]
    z = jnp.dot(hb, wp_ref[...], preferred_element_type=F32)
    za = z[:, :d_v]
    zb = z[:, d_v:d_v + d_inner]
    ga = z[:, d_v + d_inner:d_v + d_inner + d_model]
    gb = z[:, d_v + d_inner + d_model:]

    o = of_ref[...] + ob_ref[...]
    parts = []
    for hh in range(d_v // GDN_D):
        oh = o[:, hh * GDN_D:(hh + 1) * GDN_D]
        parts.append(oh * lax.rsqrt(jnp.mean(oh * oh, axis=-1, keepdims=True) + EPS))
    ya = jnp.concatenate(parts, axis=1) * rows_ref[0:1, :d_v] * _silu(za)

    yb = (yf_ref[...] + yb_ref[...] + rows_ref[1:2, :] * xs_ref[...]) * _silu(zb)
    gw = d_inner // MB_GROUPS
    parts = []
    for g in range(MB_GROUPS):
        yg = yb[:, g * gw:(g + 1) * gw]
        parts.append(yg * lax.rsqrt(jnp.mean(yg * yg, axis=-1, keepdims=True) + EPS))
    yb = jnp.concatenate(parts, axis=1) * rows_ref[2:3, :]

    merged = (jax.nn.sigmoid(ga) * _bdot(ya, wbg_ref[...]) + jax.nn.sigmoid(gb) * _bdot(yb, wbm_ref[...]))
    o_ref[...] = x + gate * _bdot(merged, wo_ref[...])


def _mixout(x1, mod3, g, w_plain, o_f, o_b, y_f, y_b, cg, rows, wbg, wbm, wo, *, tm, n_ctx_tiles, seq):
    t_lat, d_v = o_f.shape
    d_inner = y_f.shape[1]
    d = x1.shape[1]
    return pl.pallas_call(
        functools.partial(_mixout_kernel, d_model=d, d_v=d_v, d_inner=d_inner),
        out_shape=jax.ShapeDtypeStruct((t_lat, d), F32),
        grid=(t_lat // tm,),
        in_specs=[pl.BlockSpec((tm, d), lambda i: (i + n_ctx_tiles, 0)),
                  pl.BlockSpec((1,) + mod3.shape[1:], lambda i: (1 + (i * tm) // seq, 0, 0)),
                  _const_spec(g.shape), _const_spec(w_plain.shape),
                  pl.BlockSpec((tm, d_v), lambda i: (i, 0)), pl.BlockSpec((tm, d_v), lambda i: (i, 0)),
                  pl.BlockSpec((tm, d_inner), lambda i: (i, 0)), pl.BlockSpec((tm, d_inner), lambda i: (i, 0)),
                  pl.BlockSpec((tm, d_inner), lambda i: (i + n_ctx_tiles, 0)),
                  _const_spec(rows.shape), _const_spec(wbg.shape), _const_spec(wbm.shape), _const_spec(wo.shape)],
        out_specs=pl.BlockSpec((tm, d), lambda i: (i, 0)),
        compiler_params=pltpu.CompilerParams(vmem_limit_bytes=VMEM_LIMIT),
        name="mixout",
    )(x1, mod3, g, w_plain, o_f, o_b, y_f, y_b, cg, rows, wbg, wbm, wo)


def kernel(x, c, ctx, c_ctx, w_ada, b_ada, norm_g, ffn_w_gu, ffn_w_down, w_in, gdn_conv_w, gdn_A_log, gdn_dt_bias, gdn_norm_g, mb_conv_w, mb_conv_b, mb_A_log, mb_dt_bias, mb_D, mb_norm_g, w_branch_gdn, w_branch_mb, w_out, final_g):
    n_batch, seq, d = x.shape
    ctx_len = ctx.shape[1]
    assert w_ada.shape[0] == 1, "single-layer operation"
    d_qk = GDN_HEADS * GDN_D
    d_v = GDN_HEADS * GDN_D
    d_inner = mb_norm_g.shape[1]
    n_mb_heads = d_inner // MB_HEADDIM
    d_bc = MB_GROUPS * MB_STATE
    tm = 256
    t_ctx, t_lat = n_batch * ctx_len, n_batch * seq
    assert t_ctx % tm == 0 and seq % tm == 0 and tm % GRID_W == 0 and tm % ctx_len == 0
    assert ctx_len % SCAN_CHUNK == 0 and seq % SCAN_CHUNK == 0
    assert N_DIR * (2 * GDN_HEADS + n_mb_heads) <= LANE
    n_ctx_tiles = t_ctx // tm

    rows = 16
    cc = jnp.concatenate([c_ctx[None, :], c, jnp.zeros((rows - 1 - n_batch, d), F32)], axis=0)
    mod3 = _adaln(cc, w_ada[0], b_ada[0][None, :]).reshape(rows, 9, d)

    def row_all(i):
        return jnp.where(i < n_ctx_tiles, 0, 1 + ((i - n_ctx_tiles) * tm) // seq)

    tm_ffn = 2 * tm
    assert t_ctx % tm_ffn == 0 and seq % tm_ffn == 0

    def row_ffn(i):
        return jnp.where(i < t_ctx // tm_ffn, 0, 1 + ((i - t_ctx // tm_ffn) * tm_ffn) // seq)

    ffn_gu, ffn_down = ffn_w_gu[0].astype(BF16), ffn_w_down[0].astype(BF16)
    x1 = _ffn(ctx.reshape(t_ctx, d), x.reshape(t_lat, d), mod3, norm_g[0, 0][None, :], ffn_gu, ffn_down,
              final_g[None, :], k=0, row_of_tile=row_ffn, tm=tm_ffn, final=False, name="ffn1")

    sizes = (2 * d_qk + d_v, d_v, N_DIR * GDN_HEADS, N_DIR * GDN_HEADS, d_inner, d_inner + 2 * d_bc,
             N_DIR * n_mb_heads)
    o_qkv, o_za, o_a, o_beta, o_zb, o_xbc, o_dt, o_gates = [sum(sizes[:j]) for j in range(len(sizes) + 1)]

    def regroup(a, off_qkv, off_xbc):
        return jnp.concatenate([a[..., off_xbc:off_xbc + d_inner], a[..., off_qkv:off_qkv + 2 * d_qk + d_v],
                                a[..., off_xbc + d_inner:off_xbc + d_inner + 2 * d_bc]], axis=-1)

    n_small = N_DIR * (2 * GDN_HEADS + n_mb_heads)
    w_cg, w_small, w_plain = _regroup_w_in(
        w_in[0].T,
        cg_cols=((o_xbc, o_xbc + d_inner), (o_qkv, o_za), (o_xbc + d_inner, o_dt)),
        small_cols=((o_a, o_zb), (o_dt, o_gates)),
        plain_cols=((o_za, o_a), (o_zb, o_xbc), (o_gates, o_gates + 2 * d)))
    conv_all = jnp.concatenate([gdn_conv_w[0], mb_conv_w[0]], axis=1)
    bias_all = jnp.concatenate([jnp.zeros((1, 2 * d_qk + d_v), F32), mb_conv_b[0][None, :]], axis=1)
    conv_p = regroup(jnp.concatenate([conv_all, bias_all, jnp.zeros((2, conv_all.shape[1]), F32)], axis=0),
                     0, 2 * d_qk + d_v)
    col_q = d_inner
    col_b = d_inner + 2 * d_qk + d_v
    blocks = ((0, d_inner // 2, "x"), (d_inner // 2, d_inner, "x"), (col_q, col_q + d_qk, "q"),
              (col_q + d_qk, col_q + 2 * d_qk, "k"), (col_q + 2 * d_qk, col_b, "v"), (col_b, col_b + 2 * d_bc, "x"))
    cg, sg, sgt = _inproj(x1, mod3, norm_g[0, 1][None, :], w_cg, w_small, conv_p, tm=tm, n_ctx_tiles=n_ctx_tiles,
                          ctx_len=ctx_len, seq=seq, blocks=blocks)

    pad = jnp.zeros((LANE - n_small,), F32)
    zeros_g = jnp.zeros((N_DIR * GDN_HEADS,), F32)
    zeros_m = jnp.zeros((N_DIR * n_mb_heads,), F32)
    prow = jnp.stack([jnp.concatenate([gdn_A_log[0].reshape(-1), zeros_g, zeros_m, pad]),
                      jnp.concatenate([gdn_dt_bias[0].reshape(-1), zeros_g, zeros_m, pad]),
                      jnp.concatenate([zeros_g, zeros_g, mb_A_log[0].reshape(-1), pad]),
                      jnp.concatenate([zeros_g, zeros_g, mb_dt_bias[0].reshape(-1), pad])]
                     + [jnp.zeros((LANE,), F32)] * 4)
    pcol = prow.T
    ncc, nlc = ctx_len // SCAN_CHUNK, seq // SCAN_CHUNK
    o_f, o_b = _gdn_scan(cg, sg, sgt, prow, pcol, n_batch=n_batch, ncc=ncc, nlc=nlc, col_q=col_q)

    ex = (jnp.arange(n_mb_heads)[:, None] == (jnp.arange(d_inner) // MB_HEADDIM)[None, :]).astype(BF16)
    y_f, y_b = _ssd_scan(cg, sg, sgt, prow, pcol, ex, n_batch=n_batch, ncc=ncc, nlc=nlc, d_inner=d_inner, col_b=col_b)

    rows3 = jnp.stack([jnp.concatenate([jnp.tile(gdn_norm_g[0], GDN_HEADS), jnp.zeros((d_inner - d_v,), F32)]),
                       jnp.repeat(mb_D[0], MB_HEADDIM), mb_norm_g[0]] + [jnp.zeros((d_inner,), F32)] * 5)
    x2 = _mixout(x1, mod3, norm_g[0, 1][None, :], w_plain, o_f, o_b, y_f, y_b, cg, rows3,
                 w_branch_gdn[0].astype(BF16), w_branch_mb[0].astype(BF16), w_out[0].astype(BF16),
                 tm=tm, n_ctx_tiles=n_ctx_tiles, seq=seq)

    out = _ffn(None, x2, mod3, norm_g[0, 2][None, :], ffn_gu, ffn_down, final_g[None, :], k=2, row_of_tile=lambda i: 1 + (i * tm_ffn) // seq, tm=tm_ffn, final=True, name="ffn2")
    return out.reshape(n_batch, seq, d)
```

```python
import functools

import jax
import jax.numpy as jnp
import numpy as np
from jax import lax
from jax.experimental import pallas as pl
from jax.experimental.pallas import tpu as pltpu

F32 = jnp.float32
BF16 = jnp.bfloat16
HIGHEST = lax.Precision.HIGHEST

EPS = 1e-6
FFN_RES = 0.5
GRID_W = 64
CONV_K = 5
GDN_HEADS = 8
GDN_D = 128
MB_HEADDIM = 64
MB_GROUPS = 2
MB_STATE = 128
N_DIR = 2

LANE = 128
SCAN_CHUNK = 128
VMEM_LIMIT = 56 * 1024 * 1024


def _const_spec(shape):
    nd = len(shape)
    return pl.BlockSpec(shape, lambda *_: (0,) * nd, pipeline_mode=pl.Buffered(1))


def _silu(x):
    return x * jax.nn.sigmoid(x)


def _softplus(x):
    return jnp.maximum(x, 0.0) + jnp.log(1.0 + jnp.exp(-jnp.abs(x)))


def _bdot(a, b):
    return jnp.dot(a.astype(BF16), b.astype(BF16), preferred_element_type=F32)


def _bdot_nt(a, b):
    return lax.dot_general(a.astype(BF16), b.astype(BF16), (((1,), (1,)), ((), ())),
                           preferred_element_type=F32)


def _modulated_norm(x, g, shift, scale):
    ms = jnp.mean(x * x, axis=-1, keepdims=True)
    return x * lax.rsqrt(ms + EPS) * g * (1.0 + scale) + shift


def _adaln_kernel(c_ref, w_ref, b_ref, o_ref):
    s = _silu(c_ref[...])
    o_ref[...] = jnp.dot(s, w_ref[...], preferred_element_type=F32, precision=HIGHEST) + b_ref[...]


def _adaln(cc, w, b, tn=1152):
    rows, d = cc.shape
    n = w.shape[1]
    return pl.pallas_call(
        _adaln_kernel,
        out_shape=jax.ShapeDtypeStruct((rows, n), F32),
        grid=(n // tn,),
        in_specs=[pl.BlockSpec((rows, d), lambda j: (0, 0)),
                  pl.BlockSpec((d, tn), lambda j: (0, j)),
                  pl.BlockSpec((1, tn), lambda j: (0, j))],
        out_specs=pl.BlockSpec((rows, tn), lambda j: (0, j)),
        compiler_params=pltpu.CompilerParams(vmem_limit_bytes=VMEM_LIMIT),
        name="adaln",
    )(cc, w, b)


def _ffn_kernel(c_ref, x_ref, mod_ref, g_ref, wgu_ref, wd_ref, fg_ref, o_ref, *, k, d_ff, final, n_ctx_tiles):
    if n_ctx_tiles:
        x = jnp.where(pl.program_id(0) < n_ctx_tiles, c_ref[...], x_ref[...])
    else:
        x = x_ref[...]
    shift = mod_ref[0, 3 * k:3 * k + 1, :]
    scale = mod_ref[0, 3 * k + 1:3 * k + 2, :]
    gate = mod_ref[0, 3 * k + 2:3 * k + 3, :]
    hb = _modulated_norm(x, g_ref[...], shift, scale).astype(BF16)
    gt = jnp.dot(hb, wgu_ref[0, :, :d_ff], preferred_element_type=F32)
    up = jnp.dot(hb, wgu_ref[0, :, d_ff:], preferred_element_type=F32)
    act = (_silu(gt) * up).astype(BF16)
    y = x + FFN_RES * gate * jnp.dot(act, wd_ref[0], preferred_element_type=F32)
    if final:
        ms = jnp.mean(y * y, axis=-1, keepdims=True)
        y = y * lax.rsqrt(ms + EPS) * fg_ref[...]
    o_ref[...] = y


def _ffn(xc, x, mod3, g, wgu, wd, fg, *, k, row_of_tile, tm, final, name):
    t_lat, d = x.shape
    d_ff = wd.shape[1]
    which = k // 2
    w_spec = lambda w: pl.BlockSpec((1,) + w.shape[1:], lambda i: (which, 0, 0), pipeline_mode=pl.Buffered(1))
    if xc is None:
        nct = 0
        xc, c_spec = x, pl.BlockSpec((8, d), lambda i: (0, 0))
    else:
        nct = xc.shape[0] // tm
        c_spec = pl.BlockSpec((tm, d), lambda i: (jnp.minimum(i, nct - 1), 0))
    t = nct * tm + t_lat
    return pl.pallas_call(
        functools.partial(_ffn_kernel, k=k, d_ff=d_ff, final=final, n_ctx_tiles=nct),
        out_shape=jax.ShapeDtypeStruct((t, d), F32),
        grid=(t // tm,),
        in_specs=[c_spec, pl.BlockSpec((tm, d), lambda i: (jnp.maximum(i - nct, 0), 0)),
                  pl.BlockSpec((1,) + mod3.shape[1:], lambda i: (row_of_tile(i), 0, 0)),
                  _const_spec(g.shape), w_spec(wgu), w_spec(wd), _const_spec(fg.shape)],
        out_specs=pl.BlockSpec((tm, d), lambda i: (i, 0)),
        compiler_params=pltpu.CompilerParams(vmem_limit_bytes=VMEM_LIMIT),
        name=name,
    )(xc, x, mod3, g, wgu, wd, fg)


def _regroup_kernel(offs_ref, wt_ref, *rest, n_cg_blocks, n_small_in):
    del offs_ref
    small_in = rest[:n_small_in]
    cg_ref, small_ref, plain_ref = rest[n_small_in:]
    j = pl.program_id(0)
    blk = wt_ref[...].T.astype(BF16)

    @pl.when(j < n_cg_blocks)
    def _():
        cg_ref[...] = blk

    @pl.when(j >= n_cg_blocks)
    def _():
        plain_ref[...] = blk

    @pl.when(j == 0)
    def _():
        rows = [r[...] for r in small_in]
        pad = LANE - sum(r.shape[0] for r in rows)
        small_ref[...] = jnp.concatenate(rows + [jnp.zeros((pad, rows[0].shape[1]), F32)], axis=0).T


def _regroup_w_in(wt, cg_cols, small_cols, plain_cols, bw=512):
    d = wt.shape[1]
    width = lambda cols: sum(b - a for a, b in cols)
    assert all((b - a) % bw == 0 for a, b in cg_cols + plain_cols)
    starts = lambda cols: [s for a, b in cols for s in range(a, b, bw)]
    all_starts = starts(cg_cols) + starts(plain_cols)
    unit = 8
    assert all(s % unit == 0 for s in all_starts)
    offs = jnp.asarray([s // unit for s in all_starts], jnp.int32)
    ncg = len(starts(cg_cols))
    in_specs = [pl.BlockSpec((pl.Element(bw), pl.Element(d)), lambda j, offs: (offs[j] * unit, 0))]
    in_specs += [pl.BlockSpec((pl.Element(b - a), pl.Element(d)), lambda j, offs, a=a: (a, 0))
                 for a, b in small_cols]
    out_specs = (pl.BlockSpec((d, bw), lambda j, offs: (0, jnp.minimum(j, ncg - 1))),
                 pl.BlockSpec((d, LANE), lambda j, offs: (0, 0)),
                 pl.BlockSpec((d, bw), lambda j, offs: (0, jnp.maximum(j - ncg, 0))))
    return pl.pallas_call(
        functools.partial(_regroup_kernel, n_cg_blocks=ncg, n_small_in=len(small_cols)),
        out_shape=(jax.ShapeDtypeStruct((d, width(cg_cols)), BF16), jax.ShapeDtypeStruct((d, LANE), F32),
                   jax.ShapeDtypeStruct((d, width(plain_cols)), BF16)),
        grid_spec=pltpu.PrefetchScalarGridSpec(num_scalar_prefetch=1, grid=(offs.shape[0],),
                                               in_specs=in_specs, out_specs=out_specs),
        compiler_params=pltpu.CompilerParams(vmem_limit_bytes=VMEM_LIMIT,
                                             dimension_semantics=("arbitrary",)),
        name="regroup_w_in",
    )(offs, wt, *([wt] * len(small_cols)))


def _conv_shift_matrix(tm, period):
    t = np.arange(tm)[:, None]
    s = np.arange(tm)[None, :]
    half = CONV_K // 2
    mats = [(s == t + dlt) & (t // period == (t + dlt) // period)
            for dlt in range(-half, half + 1) if dlt]
    return np.concatenate(mats, axis=1).astype(np.float32)


CONV_SPLIT = 2
CONV_HALO = 16


def _conv_shift_parts(tm, period):
    taps = CONV_K - 1
    r = tm // CONV_SPLIT
    rest = _conv_shift_matrix(tm, period).reshape(tm, taps, tm)
    blocks = []
    for s in range(CONV_SPLIT):
        rows = slice(s * r, (s + 1) * r)
        blocks.append(rest[rows, :, rows].reshape(r, taps * r).copy())
        rest[rows, :, rows] = 0
    corr = [np.zeros((2 * CONV_HALO, taps * 2 * CONV_HALO), np.float32)]
    for b in range(1, CONV_SPLIT):
        win = slice(b * r - CONV_HALO, b * r + CONV_HALO)
        corr.append(rest[win, :, win].reshape(2 * CONV_HALO, taps * 2 * CONV_HALO).copy())
        rest[win, :, win] = 0
    assert not rest.any()
    return np.stack(blocks), np.stack(corr[-max(CONV_SPLIT - 1, 1):])


def _split_dot(a, b_ref):
    b = b_ref[...]
    a1 = a.astype(BF16)
    a2 = (a - a1.astype(F32)).astype(BF16)
    b1 = b.astype(BF16)
    b2 = (b - b1.astype(F32)).astype(BF16)
    dot = functools.partial(jnp.dot, preferred_element_type=F32)
    return dot(a1, b1) + (dot(a1, b2) + dot(a2, b1))


def _inproj_kernel(x_ref, mod_ref, g_ref, w_ref, ws_ref, cp_ref, sb_ref, sc_ref, cg_ref, sg_ref, sgt_ref, *,
                   blocks):
    x = x_ref[...]
    tm = x.shape[0]
    r = tm // CONV_SPLIT
    h = _modulated_norm(x, g_ref[...], mod_ref[0, 3:4, :], mod_ref[0, 4:5, :])
    hb = h.astype(BF16)
    small = _split_dot(h, ws_ref)
    sg_ref[...] = small
    sgt_ref[...] = small.T
    half = CONV_K // 2
    for c0, c1, kind in blocks:
        u = jnp.dot(hb, w_ref[:, c0:c1], preferred_element_type=F32)
        taps = [(u * cp_ref[half + dlt:half + dlt + 1, c0:c1]).astype(BF16)
                for dlt in range(-half, half + 1) if dlt]
        conv = [jnp.dot(sb_ref[0, s], jnp.concatenate([tp[s * r:(s + 1) * r] for tp in taps], axis=0),
                        preferred_element_type=F32) for s in range(CONV_SPLIT)]
        pieces = []
        for s in range(CONV_SPLIT):
            if s > 0:
                pieces[-1] = pieces[-1] + cross[:CONV_HALO]
                pieces.append(conv[s][:CONV_HALO] + cross[CONV_HALO:])
                pieces.append(conv[s][CONV_HALO:r - CONV_HALO] if s < CONV_SPLIT - 1 else conv[s][CONV_HALO:])
            else:
                pieces.append(conv[s][:r - CONV_HALO] if CONV_SPLIT > 1 else conv[s])
            if s < CONV_SPLIT - 1:
                pieces.append(conv[s][r - CONV_HALO:])
                b = (s + 1) * r
                cross = jnp.dot(sc_ref[0, s], jnp.concatenate([tp[b - CONV_HALO:b + CONV_HALO] for tp in taps],
                                                              axis=0), preferred_element_type=F32)
        acc = u * cp_ref[half:half + 1, c0:c1] + jnp.concatenate(pieces, axis=0)
        if kind == "x":
            acc = acc + cp_ref[CONV_K:CONV_K + 1, c0:c1]
        y = _silu(acc)
        if kind in ("q", "k"):
            parts = []
            for hh in range((c1 - c0) // GDN_D):
                yh = y[:, hh * GDN_D:(hh + 1) * GDN_D]
                inv = lax.rsqrt(jnp.sum(yh * yh, axis=-1, keepdims=True) + EPS)
                if kind == "q":
                    inv = inv * (GDN_D ** -0.5)
                parts.append(yh * inv)
            y = jnp.concatenate(parts, axis=1)
        cg_ref[:, c0:c1] = y


def _inproj(x1, mod3, g, w_cg, w_small, conv_p, *, tm, n_ctx_tiles, ctx_len, seq, blocks):
    t, d = x1.shape
    ncg = w_cg.shape[1]

    def row_of_tile(i):
        return jnp.where(i < n_ctx_tiles, 0, 1 + ((i - n_ctx_tiles) * tm) // seq)

    parts = [_conv_shift_parts(tm, ctx_len), _conv_shift_parts(tm, GRID_W)]
    sblk = jnp.asarray(np.stack([p[0] for p in parts]), dtype=BF16)
    scor = jnp.asarray(np.stack([p[1] for p in parts]), dtype=BF16)
    variant = lambda i: (jnp.where(i < n_ctx_tiles, 0, 1), 0, 0, 0)
    return pl.pallas_call(
        functools.partial(_inproj_kernel, blocks=blocks),
        out_shape=(jax.ShapeDtypeStruct((t, ncg), F32), jax.ShapeDtypeStruct((t, LANE), F32),
                   jax.ShapeDtypeStruct((LANE, t), F32)),
        grid=(t // tm,),
        in_specs=[pl.BlockSpec((tm, d), lambda i: (i, 0)),
                  pl.BlockSpec((1,) + mod3.shape[1:], lambda i: (row_of_tile(i), 0, 0)),
                  _const_spec(g.shape), _const_spec(w_cg.shape), _const_spec(w_small.shape),
                  _const_spec(conv_p.shape),
                  pl.BlockSpec((1,) + sblk.shape[1:], variant), pl.BlockSpec((1,) + scor.shape[1:], variant)],
        out_specs=(pl.BlockSpec((tm, ncg), lambda i: (i, 0)), pl.BlockSpec((tm, LANE), lambda i: (i, 0)),
                   pl.BlockSpec((LANE, tm), lambda i: (0, i))),
        compiler_params=pltpu.CompilerParams(vmem_limit_bytes=VMEM_LIMIT),
        name="inproj",
    )(x1, mod3, g, w_cg, w_small, conv_p, sblk, scor)


def _expand(x, e, pieces, e_left=False):
    acc = None
    r = x
    for _ in range(pieces):
        xp = r.astype(BF16)
        r = r - xp.astype(F32)
        term = jnp.dot(e, xp, preferred_element_type=F32) if e_left else jnp.dot(xp, e, preferred_element_type=F32)
        acc = term if acc is None else acc + term
    return acc


def _tri_masks(n, rev):
    ii = lax.broadcasted_iota(jnp.int32, (n, n), 0)
    jj = lax.broadcasted_iota(jnp.int32, (n, n), 1)
    if rev:
        return ii, jj, ii <= jj, ii < jj
    return ii, jj, ii >= jj, ii > jj


def _scan_block_index(b, s, rev, n_batch, ncc, nlc):
    lc = jnp.maximum(s - ncc, 0)
    if rev:
        ctx_blk = b * ncc + (ncc - 1 - jnp.minimum(s, ncc - 1))
        lat_blk = n_batch * ncc + b * nlc + (nlc - 1 - lc)
    else:
        ctx_blk = b * ncc + jnp.minimum(s, ncc - 1)
        lat_blk = n_batch * ncc + b * nlc + lc
    return jnp.where(s < ncc, ctx_blk, lat_blk)


def _scan_out_index(b, s, rev, ncc, nlc):
    lc = jnp.maximum(s - ncc, 0)
    return b * nlc + ((nlc - 1 - lc) if rev else lc)


def _gdn_kernel(qf, kf, vf, sgf, sgtf, qb, kb_, vb, sgb, sgtb, prow_ref, pcol_ref, of_ref, ob_ref, state_ref):
    @pl.when(pl.program_id(1) == 0)
    def _():
        state_ref[...] = jnp.zeros_like(state_ref)

    n = SCAN_CHUNK
    prow, pcol = prow_ref[...], pcol_ref[...]
    probs = []
    n_sub = qf.shape[0] // n
    for d, rev, refs, c in [(0, False, (qf, kf, vf, sgf, sgtf, of_ref), c) for c in range(n_sub)] + \
                           [(1, True, (qb, kb_, vb, sgb, sgtb, ob_ref), c) for c in range(n_sub)]:
        rows = slice(c * n, (c + 1) * n)
        q, k, v, sg = [r[rows, :] for r in refs[:4]]
        sgt = refs[4][:, rows]
        g_col = -jnp.exp(prow[0:1, :]) * _softplus(sg + prow[1:2, :])
        g_row = -jnp.exp(pcol[:, 0:1]) * _softplus(sgt + pcol[:, 1:2])
        beta_col = jax.nn.sigmoid(sg)
        ii, jj, incl, strict = _tri_masks(n, rev)
        gc_col = _expand(g_col, incl.astype(BF16), 3, e_left=True)
        gc_row = _expand(g_row, (ii >= jj if rev else ii <= jj).astype(BF16), 3)
        last = 0 if rev else n - 1
        for hh in range(GDN_HEADS):
            ci = d * GDN_HEADS + hh
            lo, hi = hh * GDN_D, (hh + 1) * GDN_D
            qh, kh, vh = q[:, lo:hi], k[:, lo:hi], v[:, lo:hi]
            gcol = gc_col[:, ci:ci + 1]
            grow = gc_row[ci:ci + 1, :]
            bcol = beta_col[:, N_DIR * GDN_HEADS + ci:N_DIR * GDN_HEADS + ci + 1]
            eg = jnp.exp(gcol)
            decay = jnp.where(incl, jnp.exp(jnp.where(incl, gcol - grow, 0.0)), 0.0)
            kb = kh * bcol
            g_last = gcol[last:last + 1, :]
            probs.append(dict(d=d, hh=hh, lo=lo, hi=hi, ii=ii, jj=jj, strict=strict, decay=decay, o_ref=refs[5],
                              rows=rows, order=(n_sub - 1 - c) if rev else c,
                              lhs=jnp.concatenate([kb, qh], axis=0).astype(BF16), kh=kh.astype(BF16),
                              rhs=jnp.concatenate([vh * bcol, kb * eg], axis=1).astype(BF16),
                              q_in=(qh * eg).astype(BF16), k_out=(kh * jnp.exp(g_last - gcol)).astype(BF16),
                              g_tot=jnp.exp(g_last)))
    kk = [_bdot_nt(p["lhs"], p["kh"]) for p in probs]
    for p, kkp in zip(probs, kk):
        p["a"] = jnp.where(p["strict"], kkp[:n] * p["decay"], 0.0)
        p["qk"] = (kkp[n:] * p["decay"]).astype(BF16)
    lg = 3
    for p in probs:
        blk = (p["ii"] >> lg) == (p["jj"] >> lg)
        p["a8"] = jnp.where(blk, p["a"], 0.0).astype(BF16)
        p["t"] = (p["ii"] == p["jj"]).astype(F32) - jnp.where(blk, p["a"], 0.0)
    x2 = [_bdot(p["a8"], p["a8"]).astype(BF16) for p in probs]
    x4 = [_bdot(x, x).astype(BF16) for x in x2]
    t1 = [p["t"] + _bdot(p["t"], x) for p, x in zip(probs, x2)]
    ts = [t + _bdot(t, x) for t, x in zip(t1, x4)]
    while (1 << lg) < n:
        s_blk = 1 << lg

        def pick(x, rev):
            return jnp.concatenate([x[b0 + (0 if rev else s_blk):b0 + (s_blk if rev else 2 * s_blk)]
                                    for b0 in range(0, n, 2 * s_blk)], axis=0)

        def merge(full, part, rev):
            rows = []
            for idx, b0 in enumerate(range(0, n, 2 * s_blk)):
                new = part[idx * s_blk:(idx + 1) * s_blk]
                old = (jnp.zeros((s_blk, n), F32) if full is None else
                       full[b0 + (s_blk if rev else 0):b0 + (2 * s_blk if rev else s_blk)])
                rows += [new, old] if rev else [old, new]
            return jnp.concatenate(rows, axis=0)

        bs = []
        for p in probs:
            rev = p["d"] == 1
            r = lax.broadcasted_iota(jnp.int32, (n // 2, n), 0)
            pi = ((r >> lg) << (lg + 1)) + (r & (s_blk - 1)) + (0 if rev else s_blk)
            pj = lax.broadcasted_iota(jnp.int32, (n // 2, n), 1)
            off = ((pi >> (lg + 1)) == (pj >> (lg + 1))) & ((pi >> lg) != (pj >> lg))
            bs.append(jnp.where(off, pick(p["a"], rev), 0.0).astype(BF16))
        tb = [t.astype(BF16) for t in ts]
        bt = [_bdot(b, t) for b, t in zip(bs, tb)]
        bt_full = [merge(None, x, p["d"] == 1).astype(BF16) for x, p in zip(bt, probs)]
        upd = [_bdot(pick(t, p["d"] == 1), x) for t, x, p in zip(ts, bt_full, probs)]
        ts = [merge(t, pick(t, p["d"] == 1) - u, p["d"] == 1) for t, u, p in zip(ts, upd, probs)]
        lg += 1
    sol = [_bdot(t, p["rhs"]) for t, p in zip(ts, probs)]
    state = {(d, hh): state_ref[d, hh] for d in range(N_DIR) for hh in range(GDN_HEADS)}
    for order in range(n_sub):
        cur = [(p, s) for p, s in zip(probs, sol) if p["order"] == order]
        stb = [state[p["d"], p["hh"]].astype(BF16) for p, _ in cur]
        ws = [jnp.dot(s[:, GDN_D:].astype(BF16), sb, preferred_element_type=F32) for (_, s), sb in zip(cur, stb)]
        v_new = [(s[:, :GDN_D] - w).astype(BF16) for (_, s), w in zip(cur, ws)]
        for (p, _), sb, vn in zip(cur, stb, v_new):
            p["o_ref"][p["rows"], p["lo"]:p["hi"]] = jnp.dot(
                jnp.concatenate([p["q_in"], p["qk"]], axis=1), jnp.concatenate([sb, vn], axis=0),
                preferred_element_type=F32)
            state[p["d"], p["hh"]] = state[p["d"], p["hh"]] * p["g_tot"] + lax.dot_general(
                p["k_out"], vn, (((0,), (0,)), ((), ())), preferred_element_type=F32)
    for (d, hh), sta in state.items():
        state_ref[d, hh] = sta


GDN_CHUNKS_PER_STEP = 2


def _gdn_scan(cg, sg, sgt, prow, pcol, *, n_batch, ncc, nlc, col_q):
    assert ncc % GDN_CHUNKS_PER_STEP == 0 and nlc % GDN_CHUNKS_PER_STEP == 0
    n = SCAN_CHUNK * GDN_CHUNKS_PER_STEP
    ncc, nlc = ncc // GDN_CHUNKS_PER_STEP, nlc // GDN_CHUNKS_PER_STEP
    dq = GDN_HEADS * GDN_D
    cq = col_q // dq

    def tok(rev, col):
        return pl.BlockSpec((n, dq), lambda b, s: (_scan_block_index(b, s, rev, n_batch, ncc, nlc), col))

    def small(rev):
        return pl.BlockSpec((n, LANE), lambda b, s: (_scan_block_index(b, s, rev, n_batch, ncc, nlc), 0))

    def small_t(rev):
        return pl.BlockSpec((LANE, n), lambda b, s: (0, _scan_block_index(b, s, rev, n_batch, ncc, nlc)))

    def out(rev):
        return pl.BlockSpec((n, dq), lambda b, s: (_scan_out_index(b, s, rev, ncc, nlc), 0))

    in_specs = []
    for rev in (False, True):
        in_specs += [tok(rev, cq), tok(rev, cq + 1), tok(rev, cq + 2), small(rev), small_t(rev)]
    in_specs += [_const_spec(prow.shape), _const_spec(pcol.shape)]
    t_lat = n_batch * nlc * n
    return pl.pallas_call(
        _gdn_kernel,
        out_shape=(jax.ShapeDtypeStruct((t_lat, dq), F32),) * 2,
        grid=(n_batch, ncc + nlc),
        in_specs=in_specs,
        out_specs=(out(False), out(True)),
        scratch_shapes=[pltpu.VMEM((N_DIR, GDN_HEADS, GDN_D, GDN_D), F32)],
        compiler_params=pltpu.CompilerParams(vmem_limit_bytes=VMEM_LIMIT,
                                             dimension_semantics=("arbitrary", "arbitrary")),
        name="gdn_scan",
    )(cg, cg, cg, sg, sgt, cg, cg, cg, sg, sgt, prow, pcol)


def _ssd_steps(xs_ref, bm, cm, sg, sgt, prow, pcol, ex_ref, state_ref, y_ref, d, rev):
    n = SCAN_CHUNK
    n_heads = xs_ref.shape[1] // MB_HEADDIM
    hpg = n_heads // MB_GROUPS
    gw = hpg * MB_HEADDIM
    pw = 2 * MB_HEADDIM
    ii, jj, incl, _ = _tri_masks(n, rev)
    dt_col = _softplus(sg + prow[3:4, :])
    dt_row = _softplus(sgt + pcol[:, 3:4])
    acs_col = _expand(dt_col * -jnp.exp(prow[2:3, :]), incl.astype(BF16), 3, e_left=True)
    acs_row = _expand(dt_row * -jnp.exp(pcol[:, 2:3]), (ii >= jj if rev else ii <= jj).astype(BF16), 3)
    last = 0 if rev else n - 1
    tot = acs_col[last:last + 1, :]
    f0 = 2 * N_DIR * GDN_HEADS + d * n_heads
    heads = lambda a: a[:, f0:f0 + n_heads]
    w_in = heads(dt_col * jnp.exp(tot - acs_col))
    w_out = heads(jnp.exp(acs_col))
    w_tot = heads(jnp.broadcast_to(jnp.exp(tot), (8, LANE)))
    yield
    lane = lax.broadcasted_iota(jnp.int32, (n, pw), 1)
    left = lane < MB_HEADDIM
    grp = []
    for g in range(MB_GROUPS):
        bg = bm[:, g * MB_STATE:(g + 1) * MB_STATE]
        cgm = cm[:, g * MB_STATE:(g + 1) * MB_STATE].astype(BF16)
        grp.append((cgm, _bdot_nt(cgm, bg), bg.T.astype(BF16)))
    pairs = [(g, pr) for g in range(MB_GROUPS) for pr in range(hpg // 2)]
    lo_of = lambda g, pr: (g * hpg + 2 * pr) * MB_HEADDIM
    spread = [ex_ref[:, lo_of(g, pr):lo_of(g, pr) + pw] for g, pr in pairs]
    e_in = [_expand(w_in, sp, 1) for sp in spread]
    e_out = [_expand(w_out, sp, 1) for sp in spread]
    e_tot = [_expand(w_tot, sp, 3)[0:1, :] for sp in spread]
    yield
    xps = [xs_ref[:, lo_of(g, pr):lo_of(g, pr) + pw] for g, pr in pairs]
    sts = [state_ref[d, g, :, lo_of(g, pr) - g * gw:lo_of(g, pr) - g * gw + pw] for g, pr in pairs]
    y_off = [jnp.dot(grp[g][0], st.astype(BF16), preferred_element_type=F32) * eo
             for (g, pr), st, eo in zip(pairs, sts, e_out)]
    yield
    upd = [jnp.dot(grp[g][2], (xp * ei).astype(BF16), preferred_element_type=F32)
           for (g, pr), xp, ei in zip(pairs, xps, e_in)]
    for (g, pr), st, et, u in zip(pairs, sts, e_tot, upd):
        sl = lo_of(g, pr) - g * gw
        state_ref[d, g, :, sl:sl + pw] = st * et + u
    yield
    m2s = []
    for g, pr in pairs:
        c0 = f0 + g * hpg + 2 * pr
        ms = []
        for hh in (0, 1):
            ccol = acs_col[:, c0 + hh:c0 + hh + 1]
            crow = acs_row[c0 + hh:c0 + hh + 1, :]
            seg = jnp.where(incl, jnp.exp(jnp.where(incl, ccol - crow, 0.0)), 0.0)
            ms.append(grp[g][1] * seg * dt_row[c0 + hh:c0 + hh + 1, :])
        m2s.append(jnp.concatenate(ms, axis=1).astype(BF16))
    xbds = [jnp.concatenate([jnp.where(left, xp, 0.0), jnp.where(left, 0.0, xp)], axis=0).astype(BF16)
            for xp in xps]
    yield
    for (g, pr), m2, xbd, yo in zip(pairs, m2s, xbds, y_off):
        lo = lo_of(g, pr)
        y_ref[:, lo:lo + pw] = jnp.dot(m2, xbd, preferred_element_type=F32) + yo


def _ssd_kernel(xf, bf, cf, sgf, sgtf, xb, bb, cb, sgb, sgtb, prow_ref, pcol_ref, ex_ref, yf_ref, yb_ref, state_ref):
    @pl.when(pl.program_id(1) == 0)
    def _():
        state_ref[...] = jnp.zeros_like(state_ref)

    prow, pcol = prow_ref[...], pcol_ref[...]
    live = [_ssd_steps(xf, bf[...], cf[...], sgf[...], sgtf[...], prow, pcol, ex_ref, state_ref, yf_ref, 0, False),
            _ssd_steps(xb, bb[...], cb[...], sgb[...], sgtb[...], prow, pcol, ex_ref, state_ref, yb_ref, 1, True)]
    while live:
        for steps in list(live):
            if next(steps, "done") == "done":
                live.remove(steps)


def _ssd_scan(cg, sg, sgt, prow, pcol, ex, *, n_batch, ncc, nlc, d_inner, col_b):
    n = SCAN_CHUNK
    bw = MB_GROUPS * MB_STATE
    cb_ = col_b // bw

    def blk(rev, width, col):
        return pl.BlockSpec((n, width), lambda b, s: (_scan_block_index(b, s, rev, n_batch, ncc, nlc), col))

    def small_t(rev):
        return pl.BlockSpec((LANE, n), lambda b, s: (0, _scan_block_index(b, s, rev, n_batch, ncc, nlc)))

    def out(rev):
        return pl.BlockSpec((n, d_inner), lambda b, s: (_scan_out_index(b, s, rev, ncc, nlc), 0))

    in_specs = []
    for rev in (False, True):
        in_specs += [blk(rev, d_inner, 0), blk(rev, bw, cb_), blk(rev, bw, cb_ + 1), blk(rev, LANE, 0), small_t(rev)]
    in_specs += [_const_spec(prow.shape), _const_spec(pcol.shape), _const_spec(ex.shape)]
    t_lat = n_batch * nlc * n
    return pl.pallas_call(
        _ssd_kernel,
        out_shape=(jax.ShapeDtypeStruct((t_lat, d_inner), F32),) * 2,
        grid=(n_batch, ncc + nlc),
        in_specs=in_specs,
        out_specs=(out(False), out(True)),
        scratch_shapes=[pltpu.VMEM((N_DIR, MB_GROUPS, MB_STATE, d_inner // MB_GROUPS), F32)],
        compiler_params=pltpu.CompilerParams(vmem_limit_bytes=VMEM_LIMIT,
                                             dimension_semantics=("arbitrary", "arbitrary")),
        name="ssd_scan",
    )(cg, cg, cg, sg, sgt, cg, cg, cg, sg, sgt, prow, pcol, ex)


def _mixout_kernel(x_ref, mod_ref, g_ref, wp_ref, of_ref, ob_ref, yf_ref, yb_ref, xs_ref,
                   rows_ref, wbg_ref, wbm_ref, wo_ref, o_ref, *, d_model, d_v, d_inner):
    x = x_ref[...]
    hb = _modulated_norm(x, g_ref[...], mod_ref[0, 3:4, :], mod_ref[0, 4:5, :]).astype(BF16)
    gate = mod_ref[0, 5:6, :]
    z = jnp.dot(hb, wp_ref[...], preferred_element_type=F32)
    za = z[:, :d_v]
    zb = z[:, d_v:d_v + d_inner]
    ga = z[:, d_v + d_inner:d_v + d_inner + d_model]
    gb = z[:, d_v + d_inner + d_model:]

    o = of_ref[...] + ob_ref[...]
    parts = []
    for hh in range(d_v // GDN_D):
        oh = o[:, hh * GDN_D:(hh + 1) * GDN_D]
        parts.append(oh * lax.rsqrt(jnp.mean(oh * oh, axis=-1, keepdims=True) + EPS))
    ya = jnp.concatenate(parts, axis=1) * rows_ref[0:1, :d_v] * _silu(za)

    yb = (yf_ref[...] + yb_ref[...] + rows_ref[1:2, :] * xs_ref[...]) * _silu(zb)
    gw = d_inner // MB_GROUPS
    parts = []
    for g in range(MB_GROUPS):
        yg = yb[:, g * gw:(g + 1) * gw]
        parts.append(yg * lax.rsqrt(jnp.mean(yg * yg, axis=-1, keepdims=True) + EPS))
    yb = jnp.concatenate(parts, axis=1) * rows_ref[2:3, :]

    merged = (jax.nn.sigmoid(ga) * _bdot(ya, wbg_ref[...]) + jax.nn.sigmoid(gb) * _bdot(yb, wbm_ref[...]))
    o_ref[...] = x + gate * _bdot(merged, wo_ref[...])


def _mixout(x1, mod3, g, w_plain, o_f, o_b, y_f, y_b, cg, rows, wbg, wbm, wo, *, tm, n_ctx_tiles, seq):
    t_lat, d_v = o_f.shape
    d_inner = y_f.shape[1]
    d = x1.shape[1]
    return pl.pallas_call(
        functools.partial(_mixout_kernel, d_model=d, d_v=d_v, d_inner=d_inner),
        out_shape=jax.ShapeDtypeStruct((t_lat, d), F32),
        grid=(t_lat // tm,),
        in_specs=[pl.BlockSpec((tm, d), lambda i: (i + n_ctx_tiles, 0)),
                  pl.BlockSpec((1,) + mod3.shape[1:], lambda i: (1 + (i * tm) // seq, 0, 0)),
                  _const_spec(g.shape), _const_spec(w_plain.shape),
                  pl.BlockSpec((tm, d_v), lambda i: (i, 0)), pl.BlockSpec((tm, d_v), lambda i: (i, 0)),
                  pl.BlockSpec((tm, d_inner), lambda i: (i, 0)), pl.BlockSpec((tm, d_inner), lambda i: (i, 0)),
                  pl.BlockSpec((tm, d_inner), lambda i: (i + n_ctx_tiles, 0)),
                  _const_spec(rows.shape), _const_spec(wbg.shape), _const_spec(wbm.shape), _const_spec(wo.shape)],
        out_specs=pl.BlockSpec((tm, d), lambda i: (i, 0)),
        compiler_params=pltpu.CompilerParams(vmem_limit_bytes=VMEM_LIMIT),
        name="mixout",
    )(x1, mod3, g, w_plain, o_f, o_b, y_f, y_b, cg, rows, wbg, wbm, wo)


def kernel(x, c, ctx, c_ctx, w_ada, b_ada, norm_g, ffn_w_gu, ffn_w_down, w_in, gdn_conv_w, gdn_A_log, gdn_dt_bias, gdn_norm_g, mb_conv_w, mb_conv_b, mb_A_log, mb_dt_bias, mb_D, mb_norm_g, w_branch_gdn, w_branch_mb, w_out, final_g):
    n_batch, seq, d = x.shape
    ctx_len = ctx.shape[1]
    assert w_ada.shape[0] == 1, "single-layer operation"
    d_qk = GDN_HEADS * GDN_D
    d_v = GDN_HEADS * GDN_D
    d_inner = mb_norm_g.shape[1]
    n_mb_heads = d_inner // MB_HEADDIM
    d_bc = MB_GROUPS * MB_STATE
    tm = 256
    t_ctx, t_lat = n_batch * ctx_len, n_batch * seq
    assert t_ctx % tm == 0 and seq % tm == 0 and tm % GRID_W == 0 and tm % ctx_len == 0
    assert ctx_len % SCAN_CHUNK == 0 and seq % SCAN_CHUNK == 0
    assert N_DIR * (2 * GDN_HEADS + n_mb_heads) <= LANE
    n_ctx_tiles = t_ctx // tm

    rows = 16
    cc = jnp.concatenate([c_ctx[None, :], c, jnp.zeros((rows - 1 - n_batch, d), F32)], axis=0)
    mod3 = _adaln(cc, w_ada[0], b_ada[0][None, :]).reshape(rows, 9, d)

    def row_all(i):
        return jnp.where(i < n_ctx_tiles, 0, 1 + ((i - n_ctx_tiles) * tm) // seq)

    tm_ffn = 2 * tm
    assert t_ctx % tm_ffn == 0 and seq % tm_ffn == 0

    def row_ffn(i):
        return jnp.where(i < t_ctx // tm_ffn, 0, 1 + ((i - t_ctx // tm_ffn) * tm_ffn) // seq)

    ffn_gu, ffn_down = ffn_w_gu[0].astype(BF16), ffn_w_down[0].astype(BF16)
    x1 = _ffn(ctx.reshape(t_ctx, d), x.reshape(t_lat, d), mod3, norm_g[0, 0][None, :], ffn_gu, ffn_down,
              final_g[None, :], k=0, row_of_tile=row_ffn, tm=tm_ffn, final=False, name="ffn1")

    sizes = (2 * d_qk + d_v, d_v, N_DIR * GDN_HEADS, N_DIR * GDN_HEADS, d_inner, d_inner + 2 * d_bc,
             N_DIR * n_mb_heads)
    o_qkv, o_za, o_a, o_beta, o_zb, o_xbc, o_dt, o_gates = [sum(sizes[:j]) for j in range(len(sizes) + 1)]

    def regroup(a, off_qkv, off_xbc):
        return jnp.concatenate([a[..., off_xbc:off_xbc + d_inner], a[..., off_qkv:off_qkv + 2 * d_qk + d_v],
                                a[..., off_xbc + d_inner:off_xbc + d_inner + 2 * d_bc]], axis=-1)

    n_small = N_DIR * (2 * GDN_HEADS + n_mb_heads)
    w_cg, w_small, w_plain = _regroup_w_in(
        w_in[0].T,
        cg_cols=((o_xbc, o_xbc + d_inner), (o_qkv, o_za), (o_xbc + d_inner, o_dt)),
        small_cols=((o_a, o_zb), (o_dt, o_gates)),
        plain_cols=((o_za, o_a), (o_zb, o_xbc), (o_gates, o_gates + 2 * d)))
    conv_all = jnp.concatenate([gdn_conv_w[0], mb_conv_w[0]], axis=1)
    bias_all = jnp.concatenate([jnp.zeros((1, 2 * d_qk + d_v), F32), mb_conv_b[0][None, :]], axis=1)
    conv_p = regroup(jnp.concatenate([conv_all, bias_all, jnp.zeros((2, conv_all.shape[1]), F32)], axis=0),
                     0, 2 * d_qk + d_v)
    col_q = d_inner
    col_b = d_inner + 2 * d_qk + d_v
    blocks = ((0, d_inner // 2, "x"), (d_inner // 2, d_inner, "x"), (col_q, col_q + d_qk, "q"),
              (col_q + d_qk, col_q + 2 * d_qk, "k"), (col_q + 2 * d_qk, col_b, "v"), (col_b, col_b + 2 * d_bc, "x"))
    cg, sg, sgt = _inproj(x1, mod3, norm_g[0, 1][None, :], w_cg, w_small, conv_p, tm=tm, n_ctx_tiles=n_ctx_tiles,
                          ctx_len=ctx_len, seq=seq, blocks=blocks)

    pad = jnp.zeros((LANE - n_small,), F32)
    zeros_g = jnp.zeros((N_DIR * GDN_HEADS,), F32)
    zeros_m = jnp.zeros((N_DIR * n_mb_heads,), F32)
    prow = jnp.stack([jnp.concatenate([gdn_A_log[0].reshape(-1), zeros_g, zeros_m, pad]),
                      jnp.concatenate([gdn_dt_bias[0].reshape(-1), zeros_g, zeros_m, pad]),
                      jnp.concatenate([zeros_g, zeros_g, mb_A_log[0].reshape(-1), pad]),
                      jnp.concatenate([zeros_g, zeros_g, mb_dt_bias[0].reshape(-1), pad])]
                     + [jnp.zeros((LANE,), F32)] * 4)
    pcol = prow.T
    ncc, nlc = ctx_len // SCAN_CHUNK, seq // SCAN_CHUNK
    o_f, o_b = _gdn_scan(cg, sg, sgt, prow, pcol, n_batch=n_batch, ncc=ncc, nlc=nlc, col_q=col_q)

    ex = (jnp.arange(n_mb_heads)[:, None] == (jnp.arange(d_inner) // MB_HEADDIM)[None, :]).astype(BF16)
    y_f, y_b = _ssd_scan(cg, sg, sgt, prow, pcol, ex, n_batch=n_batch, ncc=ncc, nlc=nlc, d_inner=d_inner, col_b=col_b)

    rows3 = jnp.stack([jnp.concatenate([jnp.tile(gdn_norm_g[0], GDN_HEADS), jnp.zeros((d_inner - d_v,), F32)]),
                       jnp.repeat(mb_D[0], MB_HEADDIM), mb_norm_g[0]] + [jnp.zeros((d_inner,), F32)] * 5)
    x2 = _mixout(x1, mod3, norm_g[0, 1][None, :], w_plain, o_f, o_b, y_f, y_b, cg, rows3,
                 w_branch_gdn[0].astype(BF16), w_branch_mb[0].astype(BF16), w_out[0].astype(BF16),
                 tm=tm, n_ctx_tiles=n_ctx_tiles, seq=seq)

    out = _ffn(None, x2, mod3, norm_g[0, 2][None, :], ffn_gu, ffn_down, final_g[None, :], k=2, row_of_tile=lambda i: 1 + (i * tm_ffn) // seq, tm=tm_ffn, final=True, name="ffn2")
    return out.reshape(n_batch, seq, d)
```

```python
import functools

import jax
import jax.numpy as jnp
import numpy as np
from jax import lax
from jax.experimental import pallas as pl
from jax.experimental.pallas import tpu as pltpu

F32 = jnp.float32
BF16 = jnp.bfloat16

EPS = 1e-6
FFN_RES = 0.5
GRID_W = 64
CONV_K = 5
GDN_HEADS = 8
GDN_D = 128
MB_HEADDIM = 64
MB_GROUPS = 2
MB_STATE = 128
N_DIR = 2

LANE = 128
SCAN_CHUNK = 128
VMEM_LIMIT = 56 * 1024 * 1024


def _const_spec(shape):
    nd = len(shape)
    return pl.BlockSpec(shape, lambda *_: (0,) * nd, pipeline_mode=pl.Buffered(1))


def _silu(x):
    return x * jax.nn.sigmoid(x)


def _softplus(x):
    return jnp.maximum(x, 0.0) + jnp.log(1.0 + jnp.exp(-jnp.abs(x)))


def _bdot(a, b):
    return jnp.dot(a.astype(BF16), b.astype(BF16), preferred_element_type=F32)


def _bdot_nt(a, b):
    return lax.dot_general(a.astype(BF16), b.astype(BF16), (((1,), (1,)), ((), ())),
                           preferred_element_type=F32)


def _modulated_norm(x, g, shift, scale):
    ms = jnp.mean(x * x, axis=-1, keepdims=True)
    return x * lax.rsqrt(ms + EPS) * g * (1.0 + scale) + shift


def _adaln_kernel(c_ref, w_ref, b_ref, o_ref):
    s = _silu(c_ref[...])
    o_ref[...] = _split_dot(s, w_ref) + b_ref[...]


def _adaln(cc, w, b, tn=1152):
    rows, d = cc.shape
    n = w.shape[1]
    return pl.pallas_call(
        _adaln_kernel,
        out_shape=jax.ShapeDtypeStruct((rows, n), F32),
        grid=(n // tn,),
        in_specs=[pl.BlockSpec((rows, d), lambda j: (0, 0)),
                  pl.BlockSpec((d, tn), lambda j: (0, j)),
                  pl.BlockSpec((1, tn), lambda j: (0, j))],
        out_specs=pl.BlockSpec((rows, tn), lambda j: (0, j)),
        compiler_params=pltpu.CompilerParams(vmem_limit_bytes=VMEM_LIMIT),
        name="adaln",
    )(cc, w, b)


def _ffn_kernel(c_ref, x_ref, mod_ref, g_ref, wgu_ref, wd_ref, fg_ref, o_ref, *, k, d_ff, final, n_ctx_tiles):
    if n_ctx_tiles:
        x = jnp.where(pl.program_id(0) < n_ctx_tiles, c_ref[...], x_ref[...])
    else:
        x = x_ref[...]
    shift = mod_ref[0, 3 * k:3 * k + 1, :]
    scale = mod_ref[0, 3 * k + 1:3 * k + 2, :]
    gate = mod_ref[0, 3 * k + 2:3 * k + 3, :]
    hb = _modulated_norm(x, g_ref[...], shift, scale).astype(BF16)
    gt = jnp.dot(hb, wgu_ref[0, :, :d_ff], preferred_element_type=F32)
    up = jnp.dot(hb, wgu_ref[0, :, d_ff:], preferred_element_type=F32)
    act = (_silu(gt) * up).astype(BF16)
    y = x + FFN_RES * gate * jnp.dot(act, wd_ref[0], preferred_element_type=F32)
    if final:
        ms = jnp.mean(y * y, axis=-1, keepdims=True)
        y = y * lax.rsqrt(ms + EPS) * fg_ref[...]
    o_ref[...] = y


def _ffn(xc, x, mod3, g, wgu, wd, fg, *, k, row_of_tile, tm, final, name):
    t_lat, d = x.shape
    d_ff = wd.shape[1]
    which = k // 2
    w_spec = lambda w: pl.BlockSpec((1,) + w.shape[1:], lambda i: (which, 0, 0), pipeline_mode=pl.Buffered(1))
    if xc is None:
        nct = 0
        xc, c_spec = x, pl.BlockSpec((8, d), lambda i: (0, 0))
    else:
        nct = xc.shape[0] // tm
        c_spec = pl.BlockSpec((tm, d), lambda i: (jnp.minimum(i, nct - 1), 0))
    t = nct * tm + t_lat
    return pl.pallas_call(
        functools.partial(_ffn_kernel, k=k, d_ff=d_ff, final=final, n_ctx_tiles=nct),
        out_shape=jax.ShapeDtypeStruct((t, d), F32),
        grid=(t // tm,),
        in_specs=[c_spec, pl.BlockSpec((tm, d), lambda i: (jnp.maximum(i - nct, 0), 0)),
                  pl.BlockSpec((1,) + mod3.shape[1:], lambda i: (row_of_tile(i), 0, 0)),
                  _const_spec(g.shape), w_spec(wgu), w_spec(wd), _const_spec(fg.shape)],
        out_specs=pl.BlockSpec((tm, d), lambda i: (i, 0)),
        compiler_params=pltpu.CompilerParams(vmem_limit_bytes=VMEM_LIMIT),
        name=name,
    )(xc, x, mod3, g, wgu, wd, fg)


def _regroup_kernel(offs_ref, wt_ref, *rest, n_cg_blocks, n_small_in):
    del offs_ref
    small_in = rest[:n_small_in]
    cg_ref, small_ref, plain_ref = rest[n_small_in:]
    j = pl.program_id(0)
    blk = wt_ref[...].T.astype(BF16)

    @pl.when(j < n_cg_blocks)
    def _():
        cg_ref[...] = blk

    @pl.when(j >= n_cg_blocks)
    def _():
        plain_ref[...] = blk

    @pl.when(j == 0)
    def _():
        rows = [r[...] for r in small_in]
        pad = LANE - sum(r.shape[0] for r in rows)
        small_ref[...] = jnp.concatenate(rows + [jnp.zeros((pad, rows[0].shape[1]), F32)], axis=0).T


def _regroup_w_in(wt, cg_cols, small_cols, plain_cols, bw=512):
    d = wt.shape[1]
    width = lambda cols: sum(b - a for a, b in cols)
    assert all((b - a) % bw == 0 for a, b in cg_cols + plain_cols)
    starts = lambda cols: [s for a, b in cols for s in range(a, b, bw)]
    all_starts = starts(cg_cols) + starts(plain_cols)
    unit = 8
    assert all(s % unit == 0 for s in all_starts)
    offs = jnp.asarray([s // unit for s in all_starts], jnp.int32)
    ncg = len(starts(cg_cols))
    in_specs = [pl.BlockSpec((pl.Element(bw), pl.Element(d)), lambda j, offs: (offs[j] * unit, 0))]
    in_specs += [pl.BlockSpec((pl.Element(b - a), pl.Element(d)), lambda j, offs, a=a: (a, 0))
                 for a, b in small_cols]
    out_specs = (pl.BlockSpec((d, bw), lambda j, offs: (0, jnp.minimum(j, ncg - 1))),
                 pl.BlockSpec((d, LANE), lambda j, offs: (0, 0)),
                 pl.BlockSpec((d, bw), lambda j, offs: (0, jnp.maximum(j - ncg, 0))))
    return pl.pallas_call(
        functools.partial(_regroup_kernel, n_cg_blocks=ncg, n_small_in=len(small_cols)),
        out_shape=(jax.ShapeDtypeStruct((d, width(cg_cols)), BF16), jax.ShapeDtypeStruct((d, LANE), F32),
                   jax.ShapeDtypeStruct((d, width(plain_cols)), BF16)),
        grid_spec=pltpu.PrefetchScalarGridSpec(num_scalar_prefetch=1, grid=(offs.shape[0],),
                                               in_specs=in_specs, out_specs=out_specs),
        compiler_params=pltpu.CompilerParams(vmem_limit_bytes=VMEM_LIMIT,
                                             dimension_semantics=("arbitrary",)),
        name="regroup_w_in",
    )(offs, wt, *([wt] * len(small_cols)))


def _conv_shift_matrix(tm, period):
    t = np.arange(tm)[:, None]
    s = np.arange(tm)[None, :]
    half = CONV_K // 2
    mats = [(s == t + dlt) & (t // period == (t + dlt) // period)
            for dlt in range(-half, half + 1) if dlt]
    return np.concatenate(mats, axis=1).astype(np.float32)


CONV_SPLIT = 4
CONV_HALO = 16


def _conv_shift_parts(tm, period):
    taps = CONV_K - 1
    r = tm // CONV_SPLIT
    rest = _conv_shift_matrix(tm, period).reshape(tm, taps, tm)
    blocks = []
    for s in range(CONV_SPLIT):
        rows = slice(s * r, (s + 1) * r)
        blocks.append(rest[rows, :, rows].reshape(r, taps * r).copy())
        rest[rows, :, rows] = 0
    corr = [np.zeros((2 * CONV_HALO, taps * 2 * CONV_HALO), np.float32)]
    for b in range(1, CONV_SPLIT):
        win = slice(b * r - CONV_HALO, b * r + CONV_HALO)
        corr.append(rest[win, :, win].reshape(2 * CONV_HALO, taps * 2 * CONV_HALO).copy())
        rest[win, :, win] = 0
    assert not rest.any()
    return np.stack(blocks), np.stack(corr[-max(CONV_SPLIT - 1, 1):])


def _split_dot(a, b_ref):
    b = b_ref[...]
    a1 = a.astype(BF16)
    a2 = (a - a1.astype(F32)).astype(BF16)
    b1 = b.astype(BF16)
    b2 = (b - b1.astype(F32)).astype(BF16)
    dot = functools.partial(jnp.dot, preferred_element_type=F32)
    return dot(a1, b1) + (dot(a1, b2) + dot(a2, b1))


def _inproj_kernel(x_ref, mod_ref, g_ref, w_ref, ws_ref, cp_ref, sb_ref, sc_ref, cg_ref, sg_ref, sgt_ref, *,
                   blocks):
    x = x_ref[...]
    tm = x.shape[0]
    r = tm // CONV_SPLIT
    h = _modulated_norm(x, g_ref[...], mod_ref[0, 3:4, :], mod_ref[0, 4:5, :])
    hb = h.astype(BF16)
    small = _split_dot(h, ws_ref)
    sg_ref[...] = small
    sgt_ref[...] = small.T
    half = CONV_K // 2
    for c0, c1, kind in blocks:
        u = jnp.dot(hb, w_ref[:, c0:c1], preferred_element_type=F32)
        taps = [(u * cp_ref[half + dlt:half + dlt + 1, c0:c1]).astype(BF16)
                for dlt in range(-half, half + 1) if dlt]
        conv = [jnp.dot(sb_ref[0, s], jnp.concatenate([tp[s * r:(s + 1) * r] for tp in taps], axis=0),
                        preferred_element_type=F32) for s in range(CONV_SPLIT)]
        pieces = []
        for s in range(CONV_SPLIT):
            if s > 0:
                pieces[-1] = pieces[-1] + cross[:CONV_HALO]
                pieces.append(conv[s][:CONV_HALO] + cross[CONV_HALO:])
                pieces.append(conv[s][CONV_HALO:r - CONV_HALO] if s < CONV_SPLIT - 1 else conv[s][CONV_HALO:])
            else:
                pieces.append(conv[s][:r - CONV_HALO] if CONV_SPLIT > 1 else conv[s])
            if s < CONV_SPLIT - 1:
                pieces.append(conv[s][r - CONV_HALO:])
                b = (s + 1) * r
                cross = jnp.dot(sc_ref[0, s], jnp.concatenate([tp[b - CONV_HALO:b + CONV_HALO] for tp in taps],
                                                              axis=0), preferred_element_type=F32)
        acc = u * cp_ref[half:half + 1, c0:c1] + jnp.concatenate(pieces, axis=0)
        if kind == "x":
            acc = acc + cp_ref[CONV_K:CONV_K + 1, c0:c1]
        y = _silu(acc)
        if kind in ("q", "k"):
            parts = []
            for hh in range((c1 - c0) // GDN_D):
                yh = y[:, hh * GDN_D:(hh + 1) * GDN_D]
                inv = lax.rsqrt(jnp.sum(yh * yh, axis=-1, keepdims=True) + EPS)
                if kind == "q":
                    inv = inv * (GDN_D ** -0.5)
                parts.append(yh * inv)
            y = jnp.concatenate(parts, axis=1)
        cg_ref[:, c0:c1] = y


def _inproj(x1, mod3, g, w_cg, w_small, conv_p, *, tm, n_ctx_tiles, ctx_len, seq, blocks):
    t, d = x1.shape
    ncg = w_cg.shape[1]

    def row_of_tile(i):
        return jnp.where(i < n_ctx_tiles, 0, 1 + ((i - n_ctx_tiles) * tm) // seq)

    parts = [_conv_shift_parts(tm, ctx_len), _conv_shift_parts(tm, GRID_W)]
    sblk = jnp.asarray(np.stack([p[0] for p in parts]), dtype=BF16)
    scor = jnp.asarray(np.stack([p[1] for p in parts]), dtype=BF16)
    variant = lambda i: (jnp.where(i < n_ctx_tiles, 0, 1), 0, 0, 0)
    return pl.pallas_call(
        functools.partial(_inproj_kernel, blocks=blocks),
        out_shape=(jax.ShapeDtypeStruct((t, ncg), F32), jax.ShapeDtypeStruct((t, LANE), F32),
                   jax.ShapeDtypeStruct((LANE, t), F32)),
        grid=(t // tm,),
        in_specs=[pl.BlockSpec((tm, d), lambda i: (i, 0)),
                  pl.BlockSpec((1,) + mod3.shape[1:], lambda i: (row_of_tile(i), 0, 0)),
                  _const_spec(g.shape), _const_spec(w_cg.shape), _const_spec(w_small.shape),
                  _const_spec(conv_p.shape),
                  pl.BlockSpec((1,) + sblk.shape[1:], variant), pl.BlockSpec((1,) + scor.shape[1:], variant)],
        out_specs=(pl.BlockSpec((tm, ncg), lambda i: (i, 0)), pl.BlockSpec((tm, LANE), lambda i: (i, 0)),
                   pl.BlockSpec((LANE, tm), lambda i: (0, i))),
        compiler_params=pltpu.CompilerParams(vmem_limit_bytes=VMEM_LIMIT),
        name="inproj",
    )(x1, mod3, g, w_cg, w_small, conv_p, sblk, scor)


def _expand(x, e, pieces, e_left=False):
    acc = None
    r = x
    for _ in range(pieces):
        xp = r.astype(BF16)
        r = r - xp.astype(F32)
        term = jnp.dot(e, xp, preferred_element_type=F32) if e_left else jnp.dot(xp, e, preferred_element_type=F32)
        acc = term if acc is None else acc + term
    return acc


def _tri_masks(n, rev):
    ii = lax.broadcasted_iota(jnp.int32, (n, n), 0)
    jj = lax.broadcasted_iota(jnp.int32, (n, n), 1)
    if rev:
        return ii, jj, ii <= jj, ii < jj
    return ii, jj, ii >= jj, ii > jj


def _scan_block_index(b, s, rev, n_batch, ncc, nlc):
    lc = jnp.maximum(s - ncc, 0)
    if rev:
        ctx_blk = b * ncc + (ncc - 1 - jnp.minimum(s, ncc - 1))
        lat_blk = n_batch * ncc + b * nlc + (nlc - 1 - lc)
    else:
        ctx_blk = b * ncc + jnp.minimum(s, ncc - 1)
        lat_blk = n_batch * ncc + b * nlc + lc
    return jnp.where(s < ncc, ctx_blk, lat_blk)


def _scan_out_index(b, s, rev, ncc, nlc):
    lc = jnp.maximum(s - ncc, 0)
    return b * nlc + ((nlc - 1 - lc) if rev else lc)


def _gdn_kernel(qf, kf, vf, sgf, sgtf, qb, kb_, vb, sgb, sgtb, prow_ref, pcol_ref, of_ref, ob_ref, state_ref):
    @pl.when(pl.program_id(1) == 0)
    def _():
        state_ref[...] = jnp.zeros_like(state_ref)

    n = SCAN_CHUNK
    prow, pcol = prow_ref[...], pcol_ref[...]
    probs = []
    n_sub = qf.shape[0] // n
    for d, rev, refs, c in [(0, False, (qf, kf, vf, sgf, sgtf, of_ref), c) for c in range(n_sub)] + \
                           [(1, True, (qb, kb_, vb, sgb, sgtb, ob_ref), c) for c in range(n_sub)]:
        rows = slice(c * n, (c + 1) * n)
        q, k, v, sg = [r[rows, :] for r in refs[:4]]
        sgt = refs[4][:, rows]
        g_col = -jnp.exp(prow[0:1, :]) * _softplus(sg + prow[1:2, :])
        g_row = -jnp.exp(pcol[:, 0:1]) * _softplus(sgt + pcol[:, 1:2])
        beta_col = jax.nn.sigmoid(sg)
        ii, jj, incl, strict = _tri_masks(n, rev)
        gc_col = _expand(g_col, incl.astype(BF16), 3, e_left=True)
        gc_row = _expand(g_row, (ii >= jj if rev else ii <= jj).astype(BF16), 3)
        last = 0 if rev else n - 1
        for hh in range(GDN_HEADS):
            ci = d * GDN_HEADS + hh
            lo, hi = hh * GDN_D, (hh + 1) * GDN_D
            qh, kh, vh = q[:, lo:hi], k[:, lo:hi], v[:, lo:hi]
            gcol = gc_col[:, ci:ci + 1]
            grow = gc_row[ci:ci + 1, :]
            bcol = beta_col[:, N_DIR * GDN_HEADS + ci:N_DIR * GDN_HEADS + ci + 1]
            eg = jnp.exp(gcol)
            decay = jnp.where(incl, jnp.exp(jnp.where(incl, gcol - grow, 0.0)), 0.0)
            kb = kh * bcol
            g_last = gcol[last:last + 1, :]
            probs.append(dict(d=d, hh=hh, lo=lo, hi=hi, ii=ii, jj=jj, strict=strict, decay=decay, o_ref=refs[5],
                              rows=rows, order=(n_sub - 1 - c) if rev else c,
                              lhs=jnp.concatenate([kb, qh], axis=0).astype(BF16), kh=kh.astype(BF16),
                              rhs=jnp.concatenate([vh * bcol, kb * eg], axis=1).astype(BF16),
                              q_in=(qh * eg).astype(BF16), k_out=(kh * jnp.exp(g_last - gcol)).astype(BF16),
                              g_tot=jnp.exp(g_last)))
    kk = [_bdot_nt(p["lhs"], p["kh"]) for p in probs]
    for p, kkp in zip(probs, kk):
        p["a"] = jnp.where(p["strict"], kkp[:n] * p["decay"], 0.0)
        p["qk"] = (kkp[n:] * p["decay"]).astype(BF16)
    lg = 3
    for p in probs:
        blk = (p["ii"] >> lg) == (p["jj"] >> lg)
        p["a8"] = jnp.where(blk, p["a"], 0.0).astype(BF16)
        p["t"] = (p["ii"] == p["jj"]).astype(F32) - jnp.where(blk, p["a"], 0.0)
    x2 = [_bdot(p["a8"], p["a8"]).astype(BF16) for p in probs]
    x4 = [_bdot(x, x).astype(BF16) for x in x2]
    t1 = [p["t"] + _bdot(p["t"], x) for p, x in zip(probs, x2)]
    ts = [t + _bdot(t, x) for t, x in zip(t1, x4)]
    while (1 << lg) < n:
        s_blk = 1 << lg

        def pick(x, rev):
            return jnp.concatenate([x[b0 + (0 if rev else s_blk):b0 + (s_blk if rev else 2 * s_blk)]
                                    for b0 in range(0, n, 2 * s_blk)], axis=0)

        def merge(full, part, rev):
            rows = []
            for idx, b0 in enumerate(range(0, n, 2 * s_blk)):
                new = part[idx * s_blk:(idx + 1) * s_blk]
                old = (jnp.zeros((s_blk, n), F32) if full is None else
                       full[b0 + (s_blk if rev else 0):b0 + (2 * s_blk if rev else s_blk)])
                rows += [new, old] if rev else [old, new]
            return jnp.concatenate(rows, axis=0)

        bs = []
        for p in probs:
            rev = p["d"] == 1
            r = lax.broadcasted_iota(jnp.int32, (n // 2, n), 0)
            pi = ((r >> lg) << (lg + 1)) + (r & (s_blk - 1)) + (0 if rev else s_blk)
            pj = lax.broadcasted_iota(jnp.int32, (n // 2, n), 1)
            off = ((pi >> (lg + 1)) == (pj >> (lg + 1))) & ((pi >> lg) != (pj >> lg))
            bs.append(jnp.where(off, pick(p["a"], rev), 0.0).astype(BF16))
        tb = [t.astype(BF16) for t in ts]
        bt = [_bdot(b, t) for b, t in zip(bs, tb)]
        bt_full = [merge(None, x, p["d"] == 1).astype(BF16) for x, p in zip(bt, probs)]
        upd = [_bdot(pick(t, p["d"] == 1), x) for t, x, p in zip(ts, bt_full, probs)]
        ts = [merge(t, pick(t, p["d"] == 1) - u, p["d"] == 1) for t, u, p in zip(ts, upd, probs)]
        lg += 1
    sol = [_bdot(t, p["rhs"]) for t, p in zip(ts, probs)]
    state = {(d, hh): state_ref[d, hh] for d in range(N_DIR) for hh in range(GDN_HEADS)}
    for order in range(n_sub):
        cur = [(p, s) for p, s in zip(probs, sol) if p["order"] == order]
        stb = [state[p["d"], p["hh"]].astype(BF16) for p, _ in cur]
        ws = [jnp.dot(s[:, GDN_D:].astype(BF16), sb, preferred_element_type=F32) for (_, s), sb in zip(cur, stb)]
        v_new = [(s[:, :GDN_D] - w).astype(BF16) for (_, s), w in zip(cur, ws)]
        for (p, _), sb, vn in zip(cur, stb, v_new):
            p["o_ref"][p["rows"], p["lo"]:p["hi"]] = jnp.dot(
                jnp.concatenate([p["q_in"], p["qk"]], axis=1), jnp.concatenate([sb, vn], axis=0),
                preferred_element_type=F32)
            state[p["d"], p["hh"]] = state[p["d"], p["hh"]] * p["g_tot"] + lax.dot_general(
                p["k_out"], vn, (((0,), (0,)), ((), ())), preferred_element_type=F32)
    for (d, hh), sta in state.items():
        state_ref[d, hh] = sta


GDN_CHUNKS_PER_STEP = 2


def _gdn_scan(cg, sg, sgt, prow, pcol, *, n_batch, ncc, nlc, col_q):
    assert ncc % GDN_CHUNKS_PER_STEP == 0 and nlc % GDN_CHUNKS_PER_STEP == 0
    n = SCAN_CHUNK * GDN_CHUNKS_PER_STEP
    ncc, nlc = ncc // GDN_CHUNKS_PER_STEP, nlc // GDN_CHUNKS_PER_STEP
    dq = GDN_HEADS * GDN_D
    cq = col_q // dq

    def tok(rev, col):
        return pl.BlockSpec((n, dq), lambda b, s: (_scan_block_index(b, s, rev, n_batch, ncc, nlc), col))

    def small(rev):
        return pl.BlockSpec((n, LANE), lambda b, s: (_scan_block_index(b, s, rev, n_batch, ncc, nlc), 0))

    def small_t(rev):
        return pl.BlockSpec((LANE, n), lambda b, s: (0, _scan_block_index(b, s, rev, n_batch, ncc, nlc)))

    def out(rev):
        return pl.BlockSpec((n, dq), lambda b, s: (_scan_out_index(b, s, rev, ncc, nlc), 0))

    in_specs = []
    for rev in (False, True):
        in_specs += [tok(rev, cq), tok(rev, cq + 1), tok(rev, cq + 2), small(rev), small_t(rev)]
    in_specs += [_const_spec(prow.shape), _const_spec(pcol.shape)]
    t_lat = n_batch * nlc * n
    return pl.pallas_call(
        _gdn_kernel,
        out_shape=(jax.ShapeDtypeStruct((t_lat, dq), F32),) * 2,
        grid=(n_batch, ncc + nlc),
        in_specs=in_specs,
        out_specs=(out(False), out(True)),
        scratch_shapes=[pltpu.VMEM((N_DIR, GDN_HEADS, GDN_D, GDN_D), F32)],
        compiler_params=pltpu.CompilerParams(vmem_limit_bytes=VMEM_LIMIT,
                                             dimension_semantics=("arbitrary", "arbitrary")),
        name="gdn_scan",
    )(cg, cg, cg, sg, sgt, cg, cg, cg, sg, sgt, prow, pcol)


def _ssd_steps(xs_ref, bm, cm, sg, sgt, prow, pcol, ex_ref, state_ref, y_ref, d, rev):
    n = SCAN_CHUNK
    n_heads = xs_ref.shape[1] // MB_HEADDIM
    hpg = n_heads // MB_GROUPS
    gw = hpg * MB_HEADDIM
    pw = 2 * MB_HEADDIM
    ii, jj, incl, _ = _tri_masks(n, rev)
    dt_col = _softplus(sg + prow[3:4, :])
    dt_row = _softplus(sgt + pcol[:, 3:4])
    acs_col = _expand(dt_col * -jnp.exp(prow[2:3, :]), incl.astype(BF16), 3, e_left=True)
    acs_row = _expand(dt_row * -jnp.exp(pcol[:, 2:3]), (ii >= jj if rev else ii <= jj).astype(BF16), 3)
    last = 0 if rev else n - 1
    tot = acs_col[last:last + 1, :]
    f0 = 2 * N_DIR * GDN_HEADS + d * n_heads
    heads = lambda a: a[:, f0:f0 + n_heads]
    w_in = heads(dt_col * jnp.exp(tot - acs_col))
    w_out = heads(jnp.exp(acs_col))
    w_tot = heads(jnp.broadcast_to(jnp.exp(tot), (8, LANE)))
    yield
    lane = lax.broadcasted_iota(jnp.int32, (n, pw), 1)
    left = lane < MB_HEADDIM
    grp = []
    for g in range(MB_GROUPS):
        bg = bm[:, g * MB_STATE:(g + 1) * MB_STATE]
        cgm = cm[:, g * MB_STATE:(g + 1) * MB_STATE].astype(BF16)
        grp.append((cgm, _bdot_nt(cgm, bg), bg.T.astype(BF16)))
    pairs = [(g, pr) for g in range(MB_GROUPS) for pr in range(hpg // 2)]
    lo_of = lambda g, pr: (g * hpg + 2 * pr) * MB_HEADDIM
    spread = [ex_ref[:, lo_of(g, pr):lo_of(g, pr) + pw] for g, pr in pairs]
    e_in = [_expand(w_in, sp, 1) for sp in spread]
    e_out = [_expand(w_out, sp, 1) for sp in spread]
    e_tot = [_expand(w_tot, sp, 3)[0:1, :] for sp in spread]
    yield
    xps = [xs_ref[:, lo_of(g, pr):lo_of(g, pr) + pw] for g, pr in pairs]
    sts = [state_ref[d, g, :, lo_of(g, pr) - g * gw:lo_of(g, pr) - g * gw + pw] for g, pr in pairs]
    y_off = [jnp.dot(grp[g][0], st.astype(BF16), preferred_element_type=F32) * eo
             for (g, pr), st, eo in zip(pairs, sts, e_out)]
    yield
    upd = [jnp.dot(grp[g][2], (xp * ei).astype(BF16), preferred_element_type=F32)
           for (g, pr), xp, ei in zip(pairs, xps, e_in)]
    for (g, pr), st, et, u in zip(pairs, sts, e_tot, upd):
        sl = lo_of(g, pr) - g * gw
        state_ref[d, g, :, sl:sl + pw] = st * et + u
    yield
    m2s = []
    for g, pr in pairs:
        c0 = f0 + g * hpg + 2 * pr
        ms = []
        for hh in (0, 1):
            ccol = acs_col[:, c0 + hh:c0 + hh + 1]
            crow = acs_row[c0 + hh:c0 + hh + 1, :]
            seg = jnp.where(incl, jnp.exp(jnp.where(incl, ccol - crow, 0.0)), 0.0)
            ms.append(grp[g][1] * seg * dt_row[c0 + hh:c0 + hh + 1, :])
        m2s.append(jnp.concatenate(ms, axis=1).astype(BF16))
    xbds = [jnp.concatenate([jnp.where(left, xp, 0.0), jnp.where(left, 0.0, xp)], axis=0).astype(BF16)
            for xp in xps]
    yield
    for (g, pr), m2, xbd, yo in zip(pairs, m2s, xbds, y_off):
        lo = lo_of(g, pr)
        y_ref[:, lo:lo + pw] = jnp.dot(m2, xbd, preferred_element_type=F32) + yo


def _ssd_kernel(xf, bf, cf, sgf, sgtf, xb, bb, cb, sgb, sgtb, prow_ref, pcol_ref, ex_ref, yf_ref, yb_ref, state_ref):
    @pl.when(pl.program_id(1) == 0)
    def _():
        state_ref[...] = jnp.zeros_like(state_ref)

    prow, pcol = prow_ref[...], pcol_ref[...]
    live = [_ssd_steps(xf, bf[...], cf[...], sgf[...], sgtf[...], prow, pcol, ex_ref, state_ref, yf_ref, 0, False),
            _ssd_steps(xb, bb[...], cb[...], sgb[...], sgtb[...], prow, pcol, ex_ref, state_ref, yb_ref, 1, True)]
    while live:
        for steps in list(live):
            if next(steps, "done") == "done":
                live.remove(steps)


def _ssd_scan(cg, sg, sgt, prow, pcol, ex, *, n_batch, ncc, nlc, d_inner, col_b):
    n = SCAN_CHUNK
    bw = MB_GROUPS * MB_STATE
    cb_ = col_b // bw

    def blk(rev, width, col):
        return pl.BlockSpec((n, width), lambda b, s: (_scan_block_index(b, s, rev, n_batch, ncc, nlc), col))

    def small_t(rev):
        return pl.BlockSpec((LANE, n), lambda b, s: (0, _scan_block_index(b, s, rev, n_batch, ncc, nlc)))

    def out(rev):
        return pl.BlockSpec((n, d_inner), lambda b, s: (_scan_out_index(b, s, rev, ncc, nlc), 0))

    in_specs = []
    for rev in (False, True):
        in_specs += [blk(rev, d_inner, 0), blk(rev, bw, cb_), blk(rev, bw, cb_ + 1), blk(rev, LANE, 0), small_t(rev)]
    in_specs += [_const_spec(prow.shape), _const_spec(pcol.shape), _const_spec(ex.shape)]
    t_lat = n_batch * nlc * n
    return pl.pallas_call(
        _ssd_kernel,
        out_shape=(jax.ShapeDtypeStruct((t_lat, d_inner), F32),) * 2,
        grid=(n_batch, ncc + nlc),
        in_specs=in_specs,
        out_specs=(out(False), out(True)),
        scratch_shapes=[pltpu.VMEM((N_DIR, MB_GROUPS, MB_STATE, d_inner // MB_GROUPS), F32)],
        compiler_params=pltpu.CompilerParams(vmem_limit_bytes=VMEM_LIMIT,
                                             dimension_semantics=("arbitrary", "arbitrary")),
        name="ssd_scan",
    )(cg, cg, cg, sg, sgt, cg, cg, cg, sg, sgt, prow, pcol, ex)


def _mixout_kernel(x_ref, mod_ref, g_ref, wp_ref, of_ref, ob_ref, yf_ref, yb_ref, xs_ref,
                   rows_ref, wbg_ref, wbm_ref, wo_ref, o_ref, *, d_model, d_v, d_inner):
    x = x_ref[...]
    hb = _modulated_norm(x, g_ref[...], mod_ref[0, 3:4, :], mod_ref[0, 4:5, :]).astype(BF16)
    gate = mod_ref[0, 5:6, :]
    z = jnp.dot(hb, wp_ref[...], preferred_element_type=F32)
    za = z[:, :d_v]
    zb = z[:, d_v:d_v + d_inner]
    ga = z[:, d_v + d_inner:d_v + d_inner + d_model]
    gb = z[:, d_v + d_inner + d_model:]

    o = of_ref[...] + ob_ref[...]
    parts = []
    for hh in range(d_v // GDN_D):
        oh = o[:, hh * GDN_D:(hh + 1) * GDN_D]
        parts.append(oh * lax.rsqrt(jnp.mean(oh * oh, axis=-1, keepdims=True) + EPS))
    ya = jnp.concatenate(parts, axis=1) * rows_ref[0:1, :d_v] * _silu(za)

    yb = (yf_ref[...] + yb_ref[...] + rows_ref[1:2, :] * xs_ref[...]) * _silu(zb)
    gw = d_inner // MB_GROUPS
    parts = []
    for g in range(MB_GROUPS):
        yg = yb[:, g * gw:(g + 1) * gw]
        parts.append(yg * lax.rsqrt(jnp.mean(yg * yg, axis=-1, keepdims=True) + EPS))
    yb = jnp.concatenate(parts, axis=1) * rows_ref[2:3, :]

    merged = (jax.nn.sigmoid(ga) * _bdot(ya, wbg_ref[...]) + jax.nn.sigmoid(gb) * _bdot(yb, wbm_ref[...]))
    o_ref[...] = x + gate * _bdot(merged, wo_ref[...])


def _mixout(x1, mod3, g, w_plain, o_f, o_b, y_f, y_b, cg, rows, wbg, wbm, wo, *, tm, n_ctx_tiles, seq):
    t_lat, d_v = o_f.shape
    d_inner = y_f.shape[1]
    d = x1.shape[1]
    return pl.pallas_call(
        functools.partial(_mixout_kernel, d_model=d, d_v=d_v, d_inner=d_inner),
        out_shape=jax.ShapeDtypeStruct((t_lat, d), F32),
        grid=(t_lat // tm,),
        in_specs=[pl.BlockSpec((tm, d), lambda i: (i + n_ctx_tiles, 0)),
                  pl.BlockSpec((1,) + mod3.shape[1:], lambda i: (1 + (i * tm) // seq, 0, 0)),
                  _const_spec(g.shape), _const_spec(w_plain.shape),
                  pl.BlockSpec((tm, d_v), lambda i: (i, 0)), pl.BlockSpec((tm, d_v), lambda i: (i, 0)),
                  pl.BlockSpec((tm, d_inner), lambda i: (i, 0)), pl.BlockSpec((tm, d_inner), lambda i: (i, 0)),
                  pl.BlockSpec((tm, d_inner), lambda i: (i + n_ctx_tiles, 0)),
                  _const_spec(rows.shape), _const_spec(wbg.shape), _const_spec(wbm.shape), _const_spec(wo.shape)],
        out_specs=pl.BlockSpec((tm, d), lambda i: (i, 0)),
        compiler_params=pltpu.CompilerParams(vmem_limit_bytes=VMEM_LIMIT),
        name="mixout",
    )(x1, mod3, g, w_plain, o_f, o_b, y_f, y_b, cg, rows, wbg, wbm, wo)


def kernel(x, c, ctx, c_ctx, w_ada, b_ada, norm_g, ffn_w_gu, ffn_w_down, w_in, gdn_conv_w, gdn_A_log, gdn_dt_bias, gdn_norm_g, mb_conv_w, mb_conv_b, mb_A_log, mb_dt_bias, mb_D, mb_norm_g, w_branch_gdn, w_branch_mb, w_out, final_g):
    n_batch, seq, d = x.shape
    ctx_len = ctx.shape[1]
    assert w_ada.shape[0] == 1, "single-layer operation"
    d_qk = GDN_HEADS * GDN_D
    d_v = GDN_HEADS * GDN_D
    d_inner = mb_norm_g.shape[1]
    n_mb_heads = d_inner // MB_HEADDIM
    d_bc = MB_GROUPS * MB_STATE
    tm = 256
    t_ctx, t_lat = n_batch * ctx_len, n_batch * seq
    assert t_ctx % tm == 0 and seq % tm == 0 and tm % GRID_W == 0 and tm % ctx_len == 0
    assert ctx_len % SCAN_CHUNK == 0 and seq % SCAN_CHUNK == 0
    assert N_DIR * (2 * GDN_HEADS + n_mb_heads) <= LANE
    n_ctx_tiles = t_ctx // tm

    rows = 16
    cc = jnp.concatenate([c_ctx[None, :], c, jnp.zeros((rows - 1 - n_batch, d), F32)], axis=0)
    mod3 = _adaln(cc, w_ada[0], b_ada[0][None, :]).reshape(rows, 9, d)

    def row_all(i):
        return jnp.where(i < n_ctx_tiles, 0, 1 + ((i - n_ctx_tiles) * tm) // seq)

    tm_ffn = 2 * tm
    assert t_ctx % tm_ffn == 0 and seq % tm_ffn == 0 and tm_ffn % ctx_len == 0 and tm_ffn % GRID_W == 0

    def row_ffn(i):
        return jnp.where(i < t_ctx // tm_ffn, 0, 1 + ((i - t_ctx // tm_ffn) * tm_ffn) // seq)

    ffn_gu, ffn_down = ffn_w_gu[0].astype(BF16), ffn_w_down[0].astype(BF16)
    x1 = _ffn(ctx.reshape(t_ctx, d), x.reshape(t_lat, d), mod3, norm_g[0, 0][None, :], ffn_gu, ffn_down,
              final_g[None, :], k=0, row_of_tile=row_ffn, tm=tm_ffn, final=False, name="ffn1")

    sizes = (2 * d_qk + d_v, d_v, N_DIR * GDN_HEADS, N_DIR * GDN_HEADS, d_inner, d_inner + 2 * d_bc,
             N_DIR * n_mb_heads)
    o_qkv, o_za, o_a, o_beta, o_zb, o_xbc, o_dt, o_gates = [sum(sizes[:j]) for j in range(len(sizes) + 1)]

    def regroup(a, off_qkv, off_xbc):
        return jnp.concatenate([a[..., off_xbc:off_xbc + d_inner], a[..., off_qkv:off_qkv + 2 * d_qk + d_v],
                                a[..., off_xbc + d_inner:off_xbc + d_inner + 2 * d_bc]], axis=-1)

    n_small = N_DIR * (2 * GDN_HEADS + n_mb_heads)
    w_cg, w_small, w_plain = _regroup_w_in(
        w_in[0].T,
        cg_cols=((o_xbc, o_xbc + d_inner), (o_qkv, o_za), (o_xbc + d_inner, o_dt)),
        small_cols=((o_a, o_zb), (o_dt, o_gates)),
        plain_cols=((o_za, o_a), (o_zb, o_xbc), (o_gates, o_gates + 2 * d)))
    conv_all = jnp.concatenate([gdn_conv_w[0], mb_conv_w[0]], axis=1)
    bias_all = jnp.concatenate([jnp.zeros((1, 2 * d_qk + d_v), F32), mb_conv_b[0][None, :]], axis=1)
    conv_p = regroup(jnp.concatenate([conv_all, bias_all, jnp.zeros((2, conv_all.shape[1]), F32)], axis=0),
                     0, 2 * d_qk + d_v)
    col_q = d_inner
    col_b = d_inner + 2 * d_qk + d_v
    blocks = ((0, d_inner // 2, "x"), (d_inner // 2, d_inner, "x"), (col_q, col_q + d_qk, "q"),
              (col_q + d_qk, col_q + 2 * d_qk, "k"), (col_q + 2 * d_qk, col_b, "v"), (col_b, col_b + 2 * d_bc, "x"))
    cg, sg, sgt = _inproj(x1, mod3, norm_g[0, 1][None, :], w_cg, w_small, conv_p, tm=tm_ffn,
                          n_ctx_tiles=t_ctx // tm_ffn,
                          ctx_len=ctx_len, seq=seq, blocks=blocks)

    pad = jnp.zeros((LANE - n_small,), F32)
    zeros_g = jnp.zeros((N_DIR * GDN_HEADS,), F32)
    zeros_m = jnp.zeros((N_DIR * n_mb_heads,), F32)
    prow = jnp.stack([jnp.concatenate([gdn_A_log[0].reshape(-1), zeros_g, zeros_m, pad]),
                      jnp.concatenate([gdn_dt_bias[0].reshape(-1), zeros_g, zeros_m, pad]),
                      jnp.concatenate([zeros_g, zeros_g, mb_A_log[0].reshape(-1), pad]),
                      jnp.concatenate([zeros_g, zeros_g, mb_dt_bias[0].reshape(-1), pad])]
                     + [jnp.zeros((LANE,), F32)] * 4)
    pcol = prow.T
    ncc, nlc = ctx_len // SCAN_CHUNK, seq // SCAN_CHUNK
    o_f, o_b = _gdn_scan(cg, sg, sgt, prow, pcol, n_batch=n_batch, ncc=ncc, nlc=nlc, col_q=col_q)

    ex = (jnp.arange(n_mb_heads)[:, None] == (jnp.arange(d_inner) // MB_HEADDIM)[None, :]).astype(BF16)
    y_f, y_b = _ssd_scan(cg, sg, sgt, prow, pcol, ex, n_batch=n_batch, ncc=ncc, nlc=nlc, d_inner=d_inner, col_b=col_b)

    rows3 = jnp.stack([jnp.concatenate([jnp.tile(gdn_norm_g[0], GDN_HEADS), jnp.zeros((d_inner - d_v,), F32)]),
                       jnp.repeat(mb_D[0], MB_HEADDIM), mb_norm_g[0]] + [jnp.zeros((d_inner,), F32)] * 5)
    x2 = _mixout(x1, mod3, norm_g[0, 1][None, :], w_plain, o_f, o_b, y_f, y_b, cg, rows3,
                 w_branch_gdn[0].astype(BF16), w_branch_mb[0].astype(BF16), w_out[0].astype(BF16),
                 tm=tm, n_ctx_tiles=n_ctx_tiles, seq=seq)

    out = _ffn(None, x2, mod3, norm_g[0, 2][None, :], ffn_gu, ffn_down, final_g[None, :], k=2, row_of_tile=lambda i: 1 + (i * tm_ffn) // seq, tm=tm_ffn, final=True, name="ffn2")
    return out.reshape(n_batch, seq, d)
```

```python
import functools

import jax
import jax.numpy as jnp
import numpy as np
from jax import lax
from jax.experimental import pallas as pl
from jax.experimental.pallas import tpu as pltpu

F32 = jnp.float32
BF16 = jnp.bfloat16

EPS = 1e-6
FFN_RES = 0.5
GRID_W = 64
CONV_K = 5
GDN_HEADS = 8
GDN_D = 128
MB_HEADDIM = 64
MB_GROUPS = 2
MB_STATE = 128
N_DIR = 2

LANE = 128
SCAN_CHUNK = 128
VMEM_LIMIT = 56 * 1024 * 1024


def _const_spec(shape):
    nd = len(shape)
    return pl.BlockSpec(shape, lambda *_: (0,) * nd, pipeline_mode=pl.Buffered(1))


def _silu(x):
    return x * jax.nn.sigmoid(x)


def _softplus(x):
    return jnp.maximum(x, 0.0) + jnp.log(1.0 + jnp.exp(-jnp.abs(x)))


def _bdot(a, b):
    return jnp.dot(a.astype(BF16), b.astype(BF16), preferred_element_type=F32)


def _bdot_nt(a, b):
    return lax.dot_general(a.astype(BF16), b.astype(BF16), (((1,), (1,)), ((), ())),
                           preferred_element_type=F32)


def _modulated_norm(x, g, shift, scale):
    ms = jnp.mean(x * x, axis=-1, keepdims=True)
    return x * lax.rsqrt(ms + EPS) * g * (1.0 + scale) + shift


def _adaln_kernel(c_ref, w_ref, b_ref, o_ref):
    s = _silu(c_ref[...])
    o_ref[...] = _split_dot(s, w_ref) + b_ref[...]


def _adaln(cc, w, b, tn=1152):
    rows, d = cc.shape
    n = w.shape[1]
    return pl.pallas_call(
        _adaln_kernel,
        out_shape=jax.ShapeDtypeStruct((rows, n), F32),
        grid=(n // tn,),
        in_specs=[pl.BlockSpec((rows, d), lambda j: (0, 0)),
                  pl.BlockSpec((d, tn), lambda j: (0, j)),
                  pl.BlockSpec((1, tn), lambda j: (0, j))],
        out_specs=pl.BlockSpec((rows, tn), lambda j: (0, j)),
        compiler_params=pltpu.CompilerParams(vmem_limit_bytes=VMEM_LIMIT),
        name="adaln",
    )(cc, w, b)


def _ffn_kernel(c_ref, x_ref, mod_ref, g_ref, wgu_ref, wd_ref, fg_ref, o_ref, *, k, d_ff, final, n_ctx_tiles):
    if n_ctx_tiles:
        x = jnp.where(pl.program_id(0) < n_ctx_tiles, c_ref[...], x_ref[...])
    else:
        x = x_ref[...]
    shift = mod_ref[0, 3 * k:3 * k + 1, :]
    scale = mod_ref[0, 3 * k + 1:3 * k + 2, :]
    gate = mod_ref[0, 3 * k + 2:3 * k + 3, :]
    hb = _modulated_norm(x, g_ref[...], shift, scale).astype(BF16)
    gt = jnp.dot(hb, wgu_ref[0, :, :d_ff], preferred_element_type=F32)
    up = jnp.dot(hb, wgu_ref[0, :, d_ff:], preferred_element_type=F32)
    act = (_silu(gt) * up).astype(BF16)
    y = x + FFN_RES * gate * jnp.dot(act, wd_ref[0], preferred_element_type=F32)
    if final:
        ms = jnp.mean(y * y, axis=-1, keepdims=True)
        y = y * lax.rsqrt(ms + EPS) * fg_ref[...]
    o_ref[...] = y


def _ffn(xc, x, mod3, g, wgu, wd, fg, *, k, row_of_tile, tm, final, name):
    t_lat, d = x.shape
    d_ff = wd.shape[1]
    which = k // 2
    w_spec = lambda w: pl.BlockSpec((1,) + w.shape[1:], lambda i: (which, 0, 0), pipeline_mode=pl.Buffered(1))
    if xc is None:
        nct = 0
        xc, c_spec = x, pl.BlockSpec((8, d), lambda i: (0, 0))
    else:
        nct = xc.shape[0] // tm
        c_spec = pl.BlockSpec((tm, d), lambda i: (jnp.minimum(i, nct - 1), 0))
    t = nct * tm + t_lat
    return pl.pallas_call(
        functools.partial(_ffn_kernel, k=k, d_ff=d_ff, final=final, n_ctx_tiles=nct),
        out_shape=jax.ShapeDtypeStruct((t, d), F32),
        grid=(t // tm,),
        in_specs=[c_spec, pl.BlockSpec((tm, d), lambda i: (jnp.maximum(i - nct, 0), 0)),
                  pl.BlockSpec((1,) + mod3.shape[1:], lambda i: (row_of_tile(i), 0, 0)),
                  _const_spec(g.shape), w_spec(wgu), w_spec(wd), _const_spec(fg.shape)],
        out_specs=pl.BlockSpec((tm, d), lambda i: (i, 0)),
        compiler_params=pltpu.CompilerParams(vmem_limit_bytes=VMEM_LIMIT),
        name=name,
    )(xc, x, mod3, g, wgu, wd, fg)


def _regroup_kernel(offs_ref, wt_ref, *rest, n_cg_blocks, n_small_in):
    del offs_ref
    small_in = rest[:n_small_in]
    cg_ref, small_ref, plain_ref = rest[n_small_in:]
    j = pl.program_id(0)
    blk = wt_ref[...].T.astype(BF16)

    @pl.when(j < n_cg_blocks)
    def _():
        cg_ref[...] = blk

    @pl.when(j >= n_cg_blocks)
    def _():
        plain_ref[...] = blk

    @pl.when(j == 0)
    def _():
        rows = [r[...] for r in small_in]
        pad = LANE - sum(r.shape[0] for r in rows)
        small_ref[...] = jnp.concatenate(rows + [jnp.zeros((pad, rows[0].shape[1]), F32)], axis=0).T


def _regroup_w_in(wt, cg_cols, small_cols, plain_cols, bw=512):
    d = wt.shape[1]
    width = lambda cols: sum(b - a for a, b in cols)
    assert all((b - a) % bw == 0 for a, b in cg_cols + plain_cols)
    starts = lambda cols: [s for a, b in cols for s in range(a, b, bw)]
    all_starts = starts(cg_cols) + starts(plain_cols)
    unit = 8
    assert all(s % unit == 0 for s in all_starts)
    offs = jnp.asarray([s // unit for s in all_starts], jnp.int32)
    ncg = len(starts(cg_cols))
    in_specs = [pl.BlockSpec((pl.Element(bw), pl.Element(d)), lambda j, offs: (offs[j] * unit, 0))]
    in_specs += [pl.BlockSpec((pl.Element(b - a), pl.Element(d)), lambda j, offs, a=a: (a, 0))
                 for a, b in small_cols]
    out_specs = (pl.BlockSpec((d, bw), lambda j, offs: (0, jnp.minimum(j, ncg - 1))),
                 pl.BlockSpec((d, LANE), lambda j, offs: (0, 0)),
                 pl.BlockSpec((d, bw), lambda j, offs: (0, jnp.maximum(j - ncg, 0))))
    return pl.pallas_call(
        functools.partial(_regroup_kernel, n_cg_blocks=ncg, n_small_in=len(small_cols)),
        out_shape=(jax.ShapeDtypeStruct((d, width(cg_cols)), BF16), jax.ShapeDtypeStruct((d, LANE), F32),
                   jax.ShapeDtypeStruct((d, width(plain_cols)), BF16)),
        grid_spec=pltpu.PrefetchScalarGridSpec(num_scalar_prefetch=1, grid=(offs.shape[0],),
                                               in_specs=in_specs, out_specs=out_specs),
        compiler_params=pltpu.CompilerParams(vmem_limit_bytes=VMEM_LIMIT,
                                             dimension_semantics=("arbitrary",)),
        name="regroup_w_in",
    )(offs, wt, *([wt] * len(small_cols)))


def _conv_shift_matrix(tm, period):
    t = np.arange(tm)[:, None]
    s = np.arange(tm)[None, :]
    half = CONV_K // 2
    mats = [(s == t + dlt) & (t // period == (t + dlt) // period)
            for dlt in range(-half, half + 1) if dlt]
    return np.concatenate(mats, axis=1).astype(np.float32)


CONV_SPLIT = 4
CONV_HALO = 16


def _conv_shift_parts(tm, period):
    taps = CONV_K - 1
    r = tm // CONV_SPLIT
    rest = _conv_shift_matrix(tm, period).reshape(tm, taps, tm)
    blocks = []
    for s in range(CONV_SPLIT):
        rows = slice(s * r, (s + 1) * r)
        blocks.append(rest[rows, :, rows].reshape(r, taps * r).copy())
        rest[rows, :, rows] = 0
    corr = [np.zeros((2 * CONV_HALO, taps * 2 * CONV_HALO), np.float32)]
    for b in range(1, CONV_SPLIT):
        win = slice(b * r - CONV_HALO, b * r + CONV_HALO)
        corr.append(rest[win, :, win].reshape(2 * CONV_HALO, taps * 2 * CONV_HALO).copy())
        rest[win, :, win] = 0
    assert not rest.any()
    return np.stack(blocks), np.stack(corr[-max(CONV_SPLIT - 1, 1):])


def _split_dot(a, b_ref):
    b = b_ref[...]
    a1 = a.astype(BF16)
    a2 = (a - a1.astype(F32)).astype(BF16)
    b1 = b.astype(BF16)
    b2 = (b - b1.astype(F32)).astype(BF16)
    dot = functools.partial(jnp.dot, preferred_element_type=F32)
    return dot(a1, b1) + (dot(a1, b2) + dot(a2, b1))


def _inproj_kernel(x_ref, mod_ref, g_ref, w_ref, ws_ref, cp_ref, sb_ref, sc_ref, cg_ref, sg_ref, sgt_ref, *,
                   blocks):
    x = x_ref[...]
    tm = x.shape[0]
    r = tm // CONV_SPLIT
    h = _modulated_norm(x, g_ref[...], mod_ref[0, 3:4, :], mod_ref[0, 4:5, :])
    hb = h.astype(BF16)
    small = _split_dot(h, ws_ref)
    sg_ref[...] = small
    sgt_ref[...] = small.T
    half = CONV_K // 2
    for c0, c1, kind in blocks:
        u = jnp.dot(hb, w_ref[:, c0:c1], preferred_element_type=F32)
        taps = [(u * cp_ref[half + dlt:half + dlt + 1, c0:c1]).astype(BF16)
                for dlt in range(-half, half + 1) if dlt]
        conv = [jnp.dot(sb_ref[0, s], jnp.concatenate([tp[s * r:(s + 1) * r] for tp in taps], axis=0),
                        preferred_element_type=F32) for s in range(CONV_SPLIT)]
        pieces = []
        for s in range(CONV_SPLIT):
            if s > 0:
                pieces[-1] = pieces[-1] + cross[:CONV_HALO]
                pieces.append(conv[s][:CONV_HALO] + cross[CONV_HALO:])
                pieces.append(conv[s][CONV_HALO:r - CONV_HALO] if s < CONV_SPLIT - 1 else conv[s][CONV_HALO:])
            else:
                pieces.append(conv[s][:r - CONV_HALO] if CONV_SPLIT > 1 else conv[s])
            if s < CONV_SPLIT - 1:
                pieces.append(conv[s][r - CONV_HALO:])
                b = (s + 1) * r
                cross = jnp.dot(sc_ref[0, s], jnp.concatenate([tp[b - CONV_HALO:b + CONV_HALO] for tp in taps],
                                                              axis=0), preferred_element_type=F32)
        acc = u * cp_ref[half:half + 1, c0:c1] + jnp.concatenate(pieces, axis=0)
        if kind == "x":
            acc = acc + cp_ref[CONV_K:CONV_K + 1, c0:c1]
        y = _silu(acc)
        if kind in ("q", "k"):
            parts = []
            for hh in range((c1 - c0) // GDN_D):
                yh = y[:, hh * GDN_D:(hh + 1) * GDN_D]
                inv = lax.rsqrt(jnp.sum(yh * yh, axis=-1, keepdims=True) + EPS)
                if kind == "q":
                    inv = inv * (GDN_D ** -0.5)
                parts.append(yh * inv)
            y = jnp.concatenate(parts, axis=1)
        cg_ref[:, c0:c1] = y


def _inproj(x1, mod3, g, w_cg, w_small, conv_p, *, tm, n_ctx_tiles, ctx_len, seq, blocks):
    t, d = x1.shape
    ncg = w_cg.shape[1]

    def row_of_tile(i):
        return jnp.where(i < n_ctx_tiles, 0, 1 + ((i - n_ctx_tiles) * tm) // seq)

    parts = [_conv_shift_parts(tm, ctx_len), _conv_shift_parts(tm, GRID_W)]
    sblk = jnp.asarray(np.stack([p[0] for p in parts]), dtype=BF16)
    scor = jnp.asarray(np.stack([p[1] for p in parts]), dtype=BF16)
    variant = lambda i: (jnp.where(i < n_ctx_tiles, 0, 1), 0, 0, 0)
    return pl.pallas_call(
        functools.partial(_inproj_kernel, blocks=blocks),
        out_shape=(jax.ShapeDtypeStruct((t, ncg), F32), jax.ShapeDtypeStruct((t, LANE), F32),
                   jax.ShapeDtypeStruct((LANE, t), F32)),
        grid=(t // tm,),
        in_specs=[pl.BlockSpec((tm, d), lambda i: (i, 0)),
                  pl.BlockSpec((1,) + mod3.shape[1:], lambda i: (row_of_tile(i), 0, 0)),
                  _const_spec(g.shape), _const_spec(w_cg.shape), _const_spec(w_small.shape),
                  _const_spec(conv_p.shape),
                  pl.BlockSpec((1,) + sblk.shape[1:], variant), pl.BlockSpec((1,) + scor.shape[1:], variant)],
        out_specs=(pl.BlockSpec((tm, ncg), lambda i: (i, 0)), pl.BlockSpec((tm, LANE), lambda i: (i, 0)),
                   pl.BlockSpec((LANE, tm), lambda i: (0, i))),
        compiler_params=pltpu.CompilerParams(vmem_limit_bytes=VMEM_LIMIT),
        name="inproj",
    )(x1, mod3, g, w_cg, w_small, conv_p, sblk, scor)


def _expand(x, e, pieces, e_left=False):
    acc = None
    r = x
    for _ in range(pieces):
        xp = r.astype(BF16)
        r = r - xp.astype(F32)
        term = jnp.dot(e, xp, preferred_element_type=F32) if e_left else jnp.dot(xp, e, preferred_element_type=F32)
        acc = term if acc is None else acc + term
    return acc


def _tri_masks(n, rev):
    ii = lax.broadcasted_iota(jnp.int32, (n, n), 0)
    jj = lax.broadcasted_iota(jnp.int32, (n, n), 1)
    if rev:
        return ii, jj, ii <= jj, ii < jj
    return ii, jj, ii >= jj, ii > jj


def _scan_block_index(b, s, rev, n_batch, ncc, nlc):
    lc = jnp.maximum(s - ncc, 0)
    if rev:
        ctx_blk = b * ncc + (ncc - 1 - jnp.minimum(s, ncc - 1))
        lat_blk = n_batch * ncc + b * nlc + (nlc - 1 - lc)
    else:
        ctx_blk = b * ncc + jnp.minimum(s, ncc - 1)
        lat_blk = n_batch * ncc + b * nlc + lc
    return jnp.where(s < ncc, ctx_blk, lat_blk)


def _scan_out_index(b, s, rev, ncc, nlc):
    lc = jnp.maximum(s - ncc, 0)
    return b * nlc + ((nlc - 1 - lc) if rev else lc)


def _gdn_kernel(qf, kf, vf, sgf, sgtf, qb, kb_, vb, sgb, sgtb, prow_ref, pcol_ref, of_ref, ob_ref, state_ref):
    @pl.when(pl.program_id(1) == 0)
    def _():
        state_ref[...] = jnp.zeros_like(state_ref)

    n = SCAN_CHUNK
    prow, pcol = prow_ref[...], pcol_ref[...]
    probs = []
    n_sub = qf.shape[0] // n
    for d, rev, refs, c in [(0, False, (qf, kf, vf, sgf, sgtf, of_ref), c) for c in range(n_sub)] + \
                           [(1, True, (qb, kb_, vb, sgb, sgtb, ob_ref), c) for c in range(n_sub)]:
        rows = slice(c * n, (c + 1) * n)
        q, k, v, sg = [r[rows, :] for r in refs[:4]]
        n_g = N_DIR * GDN_HEADS
        sgt = refs[4][0:n_g, rows]
        g_col = -jnp.exp(prow[0:1, :]) * _softplus(sg + prow[1:2, :])
        g_row = -jnp.exp(pcol[0:n_g, 0:1]) * _softplus(sgt + pcol[0:n_g, 1:2])
        beta_col = jax.nn.sigmoid(sg)
        ii, jj, incl, strict = _tri_masks(n, rev)
        gc_col = _expand(g_col, incl.astype(BF16), 3, e_left=True)
        gc_row = _expand(g_row, (ii >= jj if rev else ii <= jj).astype(BF16), 3)
        last = 0 if rev else n - 1
        for hh in range(GDN_HEADS):
            ci = d * GDN_HEADS + hh
            lo, hi = hh * GDN_D, (hh + 1) * GDN_D
            qh, kh, vh = q[:, lo:hi], k[:, lo:hi], v[:, lo:hi]
            gcol = gc_col[:, ci:ci + 1]
            grow = gc_row[ci:ci + 1, :]
            bcol = beta_col[:, N_DIR * GDN_HEADS + ci:N_DIR * GDN_HEADS + ci + 1]
            eg = jnp.exp(gcol)
            decay = jnp.exp(jnp.where(incl, gcol - grow, -jnp.inf))
            kb = kh * bcol
            g_last = gcol[last:last + 1, :]
            probs.append(dict(d=d, hh=hh, lo=lo, hi=hi, ii=ii, jj=jj, strict=strict, decay=decay, o_ref=refs[5],
                              rows=rows, order=(n_sub - 1 - c) if rev else c,
                              lhs=jnp.concatenate([kb, qh], axis=0).astype(BF16), kh=kh.astype(BF16),
                              rhs=jnp.concatenate([vh * bcol, kb * eg], axis=1).astype(BF16),
                              q_in=(qh * eg).astype(BF16), k_out=(kh * jnp.exp(g_last - gcol)).astype(BF16),
                              g_tot=jnp.exp(g_last)))
    kk = [_bdot_nt(p["lhs"], p["kh"]) for p in probs]
    for p, kkp in zip(probs, kk):
        p["a"] = jnp.where(p["strict"], kkp[:n] * p["decay"], 0.0)
        p["qk"] = (kkp[n:] * p["decay"]).astype(BF16)
    lg = 3
    for p in probs:
        blk = (p["ii"] >> lg) == (p["jj"] >> lg)
        p["a8"] = jnp.where(blk, p["a"], 0.0).astype(BF16)
        p["t"] = (p["ii"] == p["jj"]).astype(F32) - jnp.where(blk, p["a"], 0.0)
    x2 = [_bdot(p["a8"], p["a8"]).astype(BF16) for p in probs]
    x4 = [_bdot(x, x).astype(BF16) for x in x2]
    t1 = [p["t"] + _bdot(p["t"], x) for p, x in zip(probs, x2)]
    ts = [t + _bdot(t, x) for t, x in zip(t1, x4)]
    while (1 << lg) < n:
        s_blk = 1 << lg

        def pick(x, rev):
            return jnp.concatenate([x[b0 + (0 if rev else s_blk):b0 + (s_blk if rev else 2 * s_blk)]
                                    for b0 in range(0, n, 2 * s_blk)], axis=0)

        def merge(full, part, rev):
            rows = []
            for idx, b0 in enumerate(range(0, n, 2 * s_blk)):
                new = part[idx * s_blk:(idx + 1) * s_blk]
                old = (jnp.zeros((s_blk, n), F32) if full is None else
                       full[b0 + (s_blk if rev else 0):b0 + (2 * s_blk if rev else s_blk)])
                rows += [new, old] if rev else [old, new]
            return jnp.concatenate(rows, axis=0)

        bs = []
        for p in probs:
            rev = p["d"] == 1
            r = lax.broadcasted_iota(jnp.int32, (n // 2, n), 0)
            pi = ((r >> lg) << (lg + 1)) + (r & (s_blk - 1)) + (0 if rev else s_blk)
            pj = lax.broadcasted_iota(jnp.int32, (n // 2, n), 1)
            off = ((pi >> (lg + 1)) == (pj >> (lg + 1))) & ((pi >> lg) != (pj >> lg))
            bs.append(jnp.where(off, pick(p["a"], rev), 0.0).astype(BF16))
        tb = [t.astype(BF16) for t in ts]
        bt = [_bdot(b, t) for b, t in zip(bs, tb)]
        bt_full = [merge(None, x, p["d"] == 1).astype(BF16) for x, p in zip(bt, probs)]
        upd = [_bdot(pick(t, p["d"] == 1), x) for t, x, p in zip(ts, bt_full, probs)]
        ts = [merge(t, pick(t, p["d"] == 1) - u, p["d"] == 1) for t, u, p in zip(ts, upd, probs)]
        lg += 1
    sol = [_bdot(t, p["rhs"]) for t, p in zip(ts, probs)]
    state = {(d, hh): state_ref[d, hh] for d in range(N_DIR) for hh in range(GDN_HEADS)}
    for order in range(n_sub):
        cur = [(p, s) for p, s in zip(probs, sol) if p["order"] == order]
        stb = [state[p["d"], p["hh"]].astype(BF16) for p, _ in cur]
        ws = [jnp.dot(s[:, GDN_D:].astype(BF16), sb, preferred_element_type=F32) for (_, s), sb in zip(cur, stb)]
        v_new = [(s[:, :GDN_D] - w).astype(BF16) for (_, s), w in zip(cur, ws)]
        for (p, _), sb, vn in zip(cur, stb, v_new):
            p["o_ref"][p["rows"], p["lo"]:p["hi"]] = jnp.dot(
                jnp.concatenate([p["q_in"], p["qk"]], axis=1), jnp.concatenate([sb, vn], axis=0),
                preferred_element_type=F32)
            state[p["d"], p["hh"]] = state[p["d"], p["hh"]] * p["g_tot"] + lax.dot_general(
                p["k_out"], vn, (((0,), (0,)), ((), ())), preferred_element_type=F32)
    for (d, hh), sta in state.items():
        state_ref[d, hh] = sta


GDN_CHUNKS_PER_STEP = 2


def _gdn_scan(cg, sg, sgt, prow, pcol, *, n_batch, ncc, nlc, col_q):
    assert ncc % GDN_CHUNKS_PER_STEP == 0 and nlc % GDN_CHUNKS_PER_STEP == 0
    n = SCAN_CHUNK * GDN_CHUNKS_PER_STEP
    ncc, nlc = ncc // GDN_CHUNKS_PER_STEP, nlc // GDN_CHUNKS_PER_STEP
    dq = GDN_HEADS * GDN_D
    cq = col_q // dq

    def tok(rev, col):
        return pl.BlockSpec((n, dq), lambda b, s: (_scan_block_index(b, s, rev, n_batch, ncc, nlc), col))

    def small(rev):
        return pl.BlockSpec((n, LANE), lambda b, s: (_scan_block_index(b, s, rev, n_batch, ncc, nlc), 0))

    def small_t(rev):
        return pl.BlockSpec((LANE, n), lambda b, s: (0, _scan_block_index(b, s, rev, n_batch, ncc, nlc)))

    def out(rev):
        return pl.BlockSpec((n, dq), lambda b, s: (_scan_out_index(b, s, rev, ncc, nlc), 0))

    in_specs = []
    for rev in (False, True):
        in_specs += [tok(rev, cq), tok(rev, cq + 1), tok(rev, cq + 2), small(rev), small_t(rev)]
    in_specs += [_const_spec(prow.shape), _const_spec(pcol.shape)]
    t_lat = n_batch * nlc * n
    return pl.pallas_call(
        _gdn_kernel,
        out_shape=(jax.ShapeDtypeStruct((t_lat, dq), F32),) * 2,
        grid=(n_batch, ncc + nlc),
        in_specs=in_specs,
        out_specs=(out(False), out(True)),
        scratch_shapes=[pltpu.VMEM((N_DIR, GDN_HEADS, GDN_D, GDN_D), F32)],
        compiler_params=pltpu.CompilerParams(vmem_limit_bytes=VMEM_LIMIT,
                                             dimension_semantics=("arbitrary", "arbitrary")),
        name="gdn_scan",
    )(cg, cg, cg, sg, sgt, cg, cg, cg, sg, sgt, prow, pcol)


def _ssd_steps(xs_ref, bm, cm, sg, sgt, prow, pcol, ex_ref, state_ref, y_ref, d, rev):
    n = SCAN_CHUNK
    n_heads = xs_ref.shape[1] // MB_HEADDIM
    hpg = n_heads // MB_GROUPS
    gw = hpg * MB_HEADDIM
    pw = 2 * MB_HEADDIM
    ii, jj, incl, _ = _tri_masks(n, rev)
    f0 = 2 * N_DIR * GDN_HEADS + d * n_heads
    heads = lambda a: a[:, f0:f0 + n_heads]
    dt_col = _softplus(sg + prow[3:4, :])
    dt_row = _softplus(sgt[f0:f0 + n_heads, :] + pcol[f0:f0 + n_heads, 3:4])
    acs_col = _expand(dt_col * -jnp.exp(prow[2:3, :]), incl.astype(BF16), 3, e_left=True)
    acs_row = _expand(dt_row * -jnp.exp(pcol[f0:f0 + n_heads, 2:3]),
                      (ii >= jj if rev else ii <= jj).astype(BF16), 3)
    last = 0 if rev else n - 1
    tot = acs_col[last:last + 1, :]
    w_in = heads(dt_col * jnp.exp(tot - acs_col))
    w_out = heads(jnp.exp(acs_col))
    w_tot = heads(jnp.broadcast_to(jnp.exp(tot), (8, LANE)))
    yield
    lane = lax.broadcasted_iota(jnp.int32, (n, pw), 1)
    left = lane < MB_HEADDIM
    grp = []
    for g in range(MB_GROUPS):
        bg = bm[:, g * MB_STATE:(g + 1) * MB_STATE]
        cgm = cm[:, g * MB_STATE:(g + 1) * MB_STATE].astype(BF16)
        grp.append((cgm, _bdot_nt(cgm, bg), bg.T.astype(BF16)))
    pairs = [(g, pr) for g in range(MB_GROUPS) for pr in range(hpg // 2)]
    lo_of = lambda g, pr: (g * hpg + 2 * pr) * MB_HEADDIM
    spread = [ex_ref[:, lo_of(g, pr):lo_of(g, pr) + pw] for g, pr in pairs]
    e_in = [_expand(w_in, sp, 1) for sp in spread]
    e_out = [_expand(w_out, sp, 1) for sp in spread]
    e_tot = [_expand(w_tot, sp, 3)[0:1, :] for sp in spread]
    yield
    xps = [xs_ref[:, lo_of(g, pr):lo_of(g, pr) + pw] for g, pr in pairs]
    sts = [state_ref[d, g, :, lo_of(g, pr) - g * gw:lo_of(g, pr) - g * gw + pw] for g, pr in pairs]
    y_off = [jnp.dot(grp[g][0], st.astype(BF16), preferred_element_type=F32) * eo
             for (g, pr), st, eo in zip(pairs, sts, e_out)]
    yield
    upd = [jnp.dot(grp[g][2], (xp * ei).astype(BF16), preferred_element_type=F32)
           for (g, pr), xp, ei in zip(pairs, xps, e_in)]
    for (g, pr), st, et, u in zip(pairs, sts, e_tot, upd):
        sl = lo_of(g, pr) - g * gw
        state_ref[d, g, :, sl:sl + pw] = st * et + u
    yield
    m2s = []
    for g, pr in pairs:
        ms = []
        for hd in (g * hpg + 2 * pr, g * hpg + 2 * pr + 1):
            ccol = acs_col[:, f0 + hd:f0 + hd + 1]
            crow = acs_row[hd:hd + 1, :]
            seg = jnp.exp(jnp.where(incl, ccol - crow, -jnp.inf))
            ms.append(grp[g][1] * seg * dt_row[hd:hd + 1, :])
        m2s.append(jnp.concatenate(ms, axis=1).astype(BF16))
    xbds = [jnp.concatenate([jnp.where(left, xp, 0.0), jnp.where(left, 0.0, xp)], axis=0).astype(BF16)
            for xp in xps]
    yield
    for (g, pr), m2, xbd, yo in zip(pairs, m2s, xbds, y_off):
        lo = lo_of(g, pr)
        y_ref[:, lo:lo + pw] = jnp.dot(m2, xbd, preferred_element_type=F32) + yo


def _ssd_kernel(xf, bf, cf, sgf, sgtf, xb, bb, cb, sgb, sgtb, prow_ref, pcol_ref, ex_ref, yf_ref, yb_ref, state_ref):
    @pl.when(pl.program_id(1) == 0)
    def _():
        state_ref[...] = jnp.zeros_like(state_ref)

    prow, pcol = prow_ref[...], pcol_ref[...]
    live = [_ssd_steps(xf, bf[...], cf[...], sgf[...], sgtf[...], prow, pcol, ex_ref, state_ref, yf_ref, 0, False),
            _ssd_steps(xb, bb[...], cb[...], sgb[...], sgtb[...], prow, pcol, ex_ref, state_ref, yb_ref, 1, True)]
    while live:
        for steps in list(live):
            if next(steps, "done") == "done":
                live.remove(steps)


def _ssd_scan(cg, sg, sgt, prow, pcol, ex, *, n_batch, ncc, nlc, d_inner, col_b):
    n = SCAN_CHUNK
    bw = MB_GROUPS * MB_STATE
    cb_ = col_b // bw

    def blk(rev, width, col):
        return pl.BlockSpec((n, width), lambda b, s: (_scan_block_index(b, s, rev, n_batch, ncc, nlc), col))

    def small_t(rev):
        return pl.BlockSpec((LANE, n), lambda b, s: (0, _scan_block_index(b, s, rev, n_batch, ncc, nlc)))

    def out(rev):
        return pl.BlockSpec((n, d_inner), lambda b, s: (_scan_out_index(b, s, rev, ncc, nlc), 0))

    in_specs = []
    for rev in (False, True):
        in_specs += [blk(rev, d_inner, 0), blk(rev, bw, cb_), blk(rev, bw, cb_ + 1), blk(rev, LANE, 0), small_t(rev)]
    in_specs += [_const_spec(prow.shape), _const_spec(pcol.shape), _const_spec(ex.shape)]
    t_lat = n_batch * nlc * n
    return pl.pallas_call(
        _ssd_kernel,
        out_shape=(jax.ShapeDtypeStruct((t_lat, d_inner), F32),) * 2,
        grid=(n_batch, ncc + nlc),
        in_specs=in_specs,
        out_specs=(out(False), out(True)),
        scratch_shapes=[pltpu.VMEM((N_DIR, MB_GROUPS, MB_STATE, d_inner // MB_GROUPS), F32)],
        compiler_params=pltpu.CompilerParams(vmem_limit_bytes=VMEM_LIMIT,
                                             dimension_semantics=("arbitrary", "arbitrary")),
        name="ssd_scan",
    )(cg, cg, cg, sg, sgt, cg, cg, cg, sg, sgt, prow, pcol, ex)


def _mixout_kernel(x_ref, mod_ref, g_ref, wp_ref, of_ref, ob_ref, yf_ref, yb_ref, xs_ref,
                   rows_ref, wbg_ref, wbm_ref, wo_ref, o_ref, *, d_model, d_v, d_inner):
    x = x_ref[...]
    hb = _modulated_norm(x, g_ref[...], mod_ref[0, 3:4, :], mod_ref[0, 4:5, :]).astype(BF16)
    gate = mod_ref[0, 5:6, :]
    z = jnp.dot(hb, wp_ref[...], preferred_element_type=F32)
    za = z[:, :d_v]
    zb = z[:, d_v:d_v + d_inner]
    ga = z[:, d_v + d_inner:d_v + d_inner + d_model]
    gb = z[:, d_v + d_inner + d_model:]

    o = of_ref[...] + ob_ref[...]
    parts = []
    for hh in range(d_v // GDN_D):
        oh = o[:, hh * GDN_D:(hh + 1) * GDN_D]
        parts.append(oh * lax.rsqrt(jnp.mean(oh * oh, axis=-1, keepdims=True) + EPS))
    ya = jnp.concatenate(parts, axis=1) * rows_ref[0:1, :d_v] * _silu(za)

    yb = (yf_ref[...] + yb_ref[...] + rows_ref[1:2, :] * xs_ref[...]) * _silu(zb)
    gw = d_inner // MB_GROUPS
    parts = []
    for g in range(MB_GROUPS):
        yg = yb[:, g * gw:(g + 1) * gw]
        parts.append(yg * lax.rsqrt(jnp.mean(yg * yg, axis=-1, keepdims=True) + EPS))
    yb = jnp.concatenate(parts, axis=1) * rows_ref[2:3, :]

    merged = (jax.nn.sigmoid(ga) * _bdot(ya, wbg_ref[...]) + jax.nn.sigmoid(gb) * _bdot(yb, wbm_ref[...]))
    o_ref[...] = x + gate * _bdot(merged, wo_ref[...])


def _mixout(x1, mod3, g, w_plain, o_f, o_b, y_f, y_b, cg, rows, wbg, wbm, wo, *, tm, n_ctx_tiles, seq):
    t_lat, d_v = o_f.shape
    d_inner = y_f.shape[1]
    d = x1.shape[1]
    return pl.pallas_call(
        functools.partial(_mixout_kernel, d_model=d, d_v=d_v, d_inner=d_inner),
        out_shape=jax.ShapeDtypeStruct((t_lat, d), F32),
        grid=(t_lat // tm,),
        in_specs=[pl.BlockSpec((tm, d), lambda i: (i + n_ctx_tiles, 0)),
                  pl.BlockSpec((1,) + mod3.shape[1:], lambda i: (1 + (i * tm) // seq, 0, 0)),
                  _const_spec(g.shape), _const_spec(w_plain.shape),
                  pl.BlockSpec((tm, d_v), lambda i: (i, 0)), pl.BlockSpec((tm, d_v), lambda i: (i, 0)),
                  pl.BlockSpec((tm, d_inner), lambda i: (i, 0)), pl.BlockSpec((tm, d_inner), lambda i: (i, 0)),
                  pl.BlockSpec((tm, d_inner), lambda i: (i + n_ctx_tiles, 0)),
                  _const_spec(rows.shape), _const_spec(wbg.shape), _const_spec(wbm.shape), _const_spec(wo.shape)],
        out_specs=pl.BlockSpec((tm, d), lambda i: (i, 0)),
        compiler_params=pltpu.CompilerParams(vmem_limit_bytes=VMEM_LIMIT),
        name="mixout",
    )(x1, mod3, g, w_plain, o_f, o_b, y_f, y_b, cg, rows, wbg, wbm, wo)


def kernel(x, c, ctx, c_ctx, w_ada, b_ada, norm_g, ffn_w_gu, ffn_w_down, w_in, gdn_conv_w, gdn_A_log, gdn_dt_bias, gdn_norm_g, mb_conv_w, mb_conv_b, mb_A_log, mb_dt_bias, mb_D, mb_norm_g, w_branch_gdn, w_branch_mb, w_out, final_g):
    n_batch, seq, d = x.shape
    ctx_len = ctx.shape[1]
    assert w_ada.shape[0] == 1, "single-layer operation"
    d_qk = GDN_HEADS * GDN_D
    d_v = GDN_HEADS * GDN_D
    d_inner = mb_norm_g.shape[1]
    n_mb_heads = d_inner // MB_HEADDIM
    d_bc = MB_GROUPS * MB_STATE
    tm = 256
    t_ctx, t_lat = n_batch * ctx_len, n_batch * seq
    assert t_ctx % tm == 0 and seq % tm == 0 and tm % GRID_W == 0 and tm % ctx_len == 0
    assert ctx_len % SCAN_CHUNK == 0 and seq % SCAN_CHUNK == 0
    assert N_DIR * (2 * GDN_HEADS + n_mb_heads) <= LANE
    n_ctx_tiles = t_ctx // tm

    rows = 16
    cc = jnp.concatenate([c_ctx[None, :], c, jnp.zeros((rows - 1 - n_batch, d), F32)], axis=0)
    mod3 = _adaln(cc, w_ada[0], b_ada[0][None, :]).reshape(rows, 9, d)

    def row_all(i):
        return jnp.where(i < n_ctx_tiles, 0, 1 + ((i - n_ctx_tiles) * tm) // seq)

    tm_ffn = 2 * tm
    assert t_ctx % tm_ffn == 0 and seq % tm_ffn == 0 and tm_ffn % ctx_len == 0 and tm_ffn % GRID_W == 0

    def row_ffn(i):
        return jnp.where(i < t_ctx // tm_ffn, 0, 1 + ((i - t_ctx // tm_ffn) * tm_ffn) // seq)

    ffn_gu, ffn_down = ffn_w_gu[0].astype(BF16), ffn_w_down[0].astype(BF16)
    x1 = _ffn(ctx.reshape(t_ctx, d), x.reshape(t_lat, d), mod3, norm_g[0, 0][None, :], ffn_gu, ffn_down,
              final_g[None, :], k=0, row_of_tile=row_ffn, tm=tm_ffn, final=False, name="ffn1")

    sizes = (2 * d_qk + d_v, d_v, N_DIR * GDN_HEADS, N_DIR * GDN_HEADS, d_inner, d_inner + 2 * d_bc,
             N_DIR * n_mb_heads)
    o_qkv, o_za, o_a, o_beta, o_zb, o_xbc, o_dt, o_gates = [sum(sizes[:j]) for j in range(len(sizes) + 1)]

    def regroup(a, off_qkv, off_xbc):
        return jnp.concatenate([a[..., off_xbc:off_xbc + d_inner], a[..., off_qkv:off_qkv + 2 * d_qk + d_v],
                                a[..., off_xbc + d_inner:off_xbc + d_inner + 2 * d_bc]], axis=-1)

    n_small = N_DIR * (2 * GDN_HEADS + n_mb_heads)
    w_cg, w_small, w_plain = _regroup_w_in(
        w_in[0].T,
        cg_cols=((o_xbc, o_xbc + d_inner), (o_qkv, o_za), (o_xbc + d_inner, o_dt)),
        small_cols=((o_a, o_zb), (o_dt, o_gates)),
        plain_cols=((o_za, o_a), (o_zb, o_xbc), (o_gates, o_gates + 2 * d)))
    conv_all = jnp.concatenate([gdn_conv_w[0], mb_conv_w[0]], axis=1)
    bias_all = jnp.concatenate([jnp.zeros((1, 2 * d_qk + d_v), F32), mb_conv_b[0][None, :]], axis=1)
    conv_p = regroup(jnp.concatenate([conv_all, bias_all, jnp.zeros((2, conv_all.shape[1]), F32)], axis=0),
                     0, 2 * d_qk + d_v)
    col_q = d_inner
    col_b = d_inner + 2 * d_qk + d_v
    blocks = ((0, d_inner // 2, "x"), (d_inner // 2, d_inner, "x"), (col_q, col_q + d_qk, "q"),
              (col_q + d_qk, col_q + 2 * d_qk, "k"), (col_q + 2 * d_qk, col_b, "v"), (col_b, col_b + 2 * d_bc, "x"))
    cg, sg, sgt = _inproj(x1, mod3, norm_g[0, 1][None, :], w_cg, w_small, conv_p, tm=tm_ffn,
                          n_ctx_tiles=t_ctx // tm_ffn,
                          ctx_len=ctx_len, seq=seq, blocks=blocks)

    pad = jnp.zeros((LANE - n_small,), F32)
    zeros_g = jnp.zeros((N_DIR * GDN_HEADS,), F32)
    zeros_m = jnp.zeros((N_DIR * n_mb_heads,), F32)
    prow = jnp.stack([jnp.concatenate([gdn_A_log[0].reshape(-1), zeros_g, zeros_m, pad]),
                      jnp.concatenate([gdn_dt_bias[0].reshape(-1), zeros_g, zeros_m, pad]),
                      jnp.concatenate([zeros_g, zeros_g, mb_A_log[0].reshape(-1), pad]),
                      jnp.concatenate([zeros_g, zeros_g, mb_dt_bias[0].reshape(-1), pad])]
                     + [jnp.zeros((LANE,), F32)] * 4)
    pcol = prow.T
    ncc, nlc = ctx_len // SCAN_CHUNK, seq // SCAN_CHUNK
    o_f, o_b = _gdn_scan(cg, sg, sgt, prow, pcol, n_batch=n_batch, ncc=ncc, nlc=nlc, col_q=col_q)

    ex = (jnp.arange(n_mb_heads)[:, None] == (jnp.arange(d_inner) // MB_HEADDIM)[None, :]).astype(BF16)
    y_f, y_b = _ssd_scan(cg, sg, sgt, prow, pcol, ex, n_batch=n_batch, ncc=ncc, nlc=nlc, d_inner=d_inner, col_b=col_b)

    rows3 = jnp.stack([jnp.concatenate([jnp.tile(gdn_norm_g[0], GDN_HEADS), jnp.zeros((d_inner - d_v,), F32)]),
                       jnp.repeat(mb_D[0], MB_HEADDIM), mb_norm_g[0]] + [jnp.zeros((d_inner,), F32)] * 5)
    x2 = _mixout(x1, mod3, norm_g[0, 1][None, :], w_plain, o_f, o_b, y_f, y_b, cg, rows3,
                 w_branch_gdn[0].astype(BF16), w_branch_mb[0].astype(BF16), w_out[0].astype(BF16),
                 tm=tm, n_ctx_tiles=n_ctx_tiles, seq=seq)

    out = _ffn(None, x2, mod3, norm_g[0, 2][None, :], ffn_gu, ffn_down, final_g[None, :], k=2, row_of_tile=lambda i: 1 + (i * tm_ffn) // seq, tm=tm_ffn, final=True, name="ffn2")
    return out.reshape(n_batch, seq, d)
```

```python
import functools

import jax
import jax.numpy as jnp
import numpy as np
from jax import lax
from jax.experimental import pallas as pl
from jax.experimental.pallas import tpu as pltpu

F32 = jnp.float32
BF16 = jnp.bfloat16

EPS = 1e-6
FFN_RES = 0.5
GRID_W = 64
CONV_K = 5
GDN_HEADS = 8
GDN_D = 128
MB_HEADDIM = 64
MB_GROUPS = 2
MB_STATE = 128
N_DIR = 2

LANE = 128
SCAN_CHUNK = 128
VMEM_LIMIT = 56 * 1024 * 1024


def _const_spec(shape):
    nd = len(shape)
    return pl.BlockSpec(shape, lambda *_: (0,) * nd, pipeline_mode=pl.Buffered(1))


def _silu(x):
    return x * jax.nn.sigmoid(x)


def _softplus(x):
    return jnp.maximum(x, 0.0) + jnp.log(1.0 + jnp.exp(-jnp.abs(x)))


def _bdot(a, b):
    return jnp.dot(a.astype(BF16), b.astype(BF16), preferred_element_type=F32)


def _bdot_nt(a, b):
    return lax.dot_general(a.astype(BF16), b.astype(BF16), (((1,), (1,)), ((), ())),
                           preferred_element_type=F32)


def _modulated_norm(x, g, shift, scale):
    ms = jnp.mean(x * x, axis=-1, keepdims=True)
    return x * lax.rsqrt(ms + EPS) * g * (1.0 + scale) + shift


def _adaln_kernel(c_ref, w_ref, b_ref, o_ref):
    s = _silu(c_ref[...])
    o_ref[...] = _split_dot(s, w_ref) + b_ref[...]


def _adaln(cc, w, b, tn=1152):
    rows, d = cc.shape
    n = w.shape[1]
    return pl.pallas_call(
        _adaln_kernel,
        out_shape=jax.ShapeDtypeStruct((rows, n), F32),
        grid=(n // tn,),
        in_specs=[pl.BlockSpec((rows, d), lambda j: (0, 0)),
                  pl.BlockSpec((d, tn), lambda j: (0, j)),
                  pl.BlockSpec((1, tn), lambda j: (0, j))],
        out_specs=pl.BlockSpec((rows, tn), lambda j: (0, j)),
        compiler_params=pltpu.CompilerParams(vmem_limit_bytes=VMEM_LIMIT),
        name="adaln",
    )(cc, w, b)


def _ffn_kernel(c_ref, x_ref, mod_ref, g_ref, wgu_ref, wd_ref, fg_ref, o_ref, *, k, d_ff, final, n_ctx_tiles):
    if n_ctx_tiles:
        x = jnp.where(pl.program_id(0) < n_ctx_tiles, c_ref[...], x_ref[...])
    else:
        x = x_ref[...]
    shift = mod_ref[0, 3 * k:3 * k + 1, :]
    scale = mod_ref[0, 3 * k + 1:3 * k + 2, :]
    gate = mod_ref[0, 3 * k + 2:3 * k + 3, :]
    hb = _modulated_norm(x, g_ref[...], shift, scale).astype(BF16)
    gt = jnp.dot(hb, wgu_ref[0, :, :d_ff], preferred_element_type=F32)
    up = jnp.dot(hb, wgu_ref[0, :, d_ff:], preferred_element_type=F32)
    act = (_silu(gt) * up).astype(BF16)
    y = x + FFN_RES * gate * jnp.dot(act, wd_ref[0], preferred_element_type=F32)
    if final:
        ms = jnp.mean(y * y, axis=-1, keepdims=True)
        y = y * lax.rsqrt(ms + EPS) * fg_ref[...]
    o_ref[...] = y


def _ffn(xc, x, mod3, g, wgu, wd, fg, *, k, row_of_tile, tm, final, name):
    t_lat, d = x.shape
    d_ff = wd.shape[1]
    which = k // 2
    w_spec = lambda w: pl.BlockSpec((1,) + w.shape[1:], lambda i: (which, 0, 0), pipeline_mode=pl.Buffered(1))
    if xc is None:
        nct = 0
        xc, c_spec = x, pl.BlockSpec((8, d), lambda i: (0, 0))
    else:
        nct = xc.shape[0] // tm
        c_spec = pl.BlockSpec((tm, d), lambda i: (jnp.minimum(i, nct - 1), 0))
    t = nct * tm + t_lat
    return pl.pallas_call(
        functools.partial(_ffn_kernel, k=k, d_ff=d_ff, final=final, n_ctx_tiles=nct),
        out_shape=jax.ShapeDtypeStruct((t, d), F32),
        grid=(t // tm,),
        in_specs=[c_spec, pl.BlockSpec((tm, d), lambda i: (jnp.maximum(i - nct, 0), 0)),
                  pl.BlockSpec((1,) + mod3.shape[1:], lambda i: (row_of_tile(i), 0, 0)),
                  _const_spec(g.shape), w_spec(wgu), w_spec(wd), _const_spec(fg.shape)],
        out_specs=pl.BlockSpec((tm, d), lambda i: (i, 0)),
        compiler_params=pltpu.CompilerParams(vmem_limit_bytes=VMEM_LIMIT),
        name=name,
    )(xc, x, mod3, g, wgu, wd, fg)


def _regroup_kernel(offs_ref, wt_ref, *rest, n_cg_blocks, n_small_in):
    del offs_ref
    small_in = rest[:n_small_in]
    cg_ref, small_ref, plain_ref = rest[n_small_in:]
    j = pl.program_id(0)
    blk = wt_ref[...].T.astype(BF16)

    @pl.when(j < n_cg_blocks)
    def _():
        cg_ref[...] = blk

    @pl.when(j >= n_cg_blocks)
    def _():
        plain_ref[...] = blk

    @pl.when(j == 0)
    def _():
        rows = [r[...] for r in small_in]
        pad = LANE - sum(r.shape[0] for r in rows)
        small_ref[...] = jnp.concatenate(rows + [jnp.zeros((pad, rows[0].shape[1]), F32)], axis=0).T


def _regroup_w_in(wt, cg_cols, small_cols, plain_cols, bw=512):
    d = wt.shape[1]
    width = lambda cols: sum(b - a for a, b in cols)
    assert all((b - a) % bw == 0 for a, b in cg_cols + plain_cols)
    starts = lambda cols: [s for a, b in cols for s in range(a, b, bw)]
    all_starts = starts(cg_cols) + starts(plain_cols)
    unit = 8
    assert all(s % unit == 0 for s in all_starts)
    offs = jnp.asarray([s // unit for s in all_starts], jnp.int32)
    ncg = len(starts(cg_cols))
    in_specs = [pl.BlockSpec((pl.Element(bw), pl.Element(d)), lambda j, offs: (offs[j] * unit, 0))]
    in_specs += [pl.BlockSpec((pl.Element(b - a), pl.Element(d)), lambda j, offs, a=a: (a, 0))
                 for a, b in small_cols]
    out_specs = (pl.BlockSpec((d, bw), lambda j, offs: (0, jnp.minimum(j, ncg - 1))),
                 pl.BlockSpec((d, LANE), lambda j, offs: (0, 0)),
                 pl.BlockSpec((d, bw), lambda j, offs: (0, jnp.maximum(j - ncg, 0))))
    return pl.pallas_call(
        functools.partial(_regroup_kernel, n_cg_blocks=ncg, n_small_in=len(small_cols)),
        out_shape=(jax.ShapeDtypeStruct((d, width(cg_cols)), BF16), jax.ShapeDtypeStruct((d, LANE), F32),
                   jax.ShapeDtypeStruct((d, width(plain_cols)), BF16)),
        grid_spec=pltpu.PrefetchScalarGridSpec(num_scalar_prefetch=1, grid=(offs.shape[0],),
                                               in_specs=in_specs, out_specs=out_specs),
        compiler_params=pltpu.CompilerParams(vmem_limit_bytes=VMEM_LIMIT,
                                             dimension_semantics=("arbitrary",)),
        name="regroup_w_in",
    )(offs, wt, *([wt] * len(small_cols)))


def _conv_shift_matrix(tm, period):
    t = np.arange(tm)[:, None]
    s = np.arange(tm)[None, :]
    half = CONV_K // 2
    mats = [(s == t + dlt) & (t // period == (t + dlt) // period)
            for dlt in range(-half, half + 1) if dlt]
    return np.concatenate(mats, axis=1).astype(np.float32)


CONV_SPLIT = 4
CONV_HALO = 16


def _conv_shift_parts(tm, period):
    taps = CONV_K - 1
    r = tm // CONV_SPLIT
    rest = _conv_shift_matrix(tm, period).reshape(tm, taps, tm)
    blocks = []
    for s in range(CONV_SPLIT):
        rows = slice(s * r, (s + 1) * r)
        blocks.append(rest[rows, :, rows].reshape(r, taps * r).copy())
        rest[rows, :, rows] = 0
    corr = [np.zeros((2 * CONV_HALO, taps * 2 * CONV_HALO), np.float32)]
    for b in range(1, CONV_SPLIT):
        win = slice(b * r - CONV_HALO, b * r + CONV_HALO)
        corr.append(rest[win, :, win].reshape(2 * CONV_HALO, taps * 2 * CONV_HALO).copy())
        rest[win, :, win] = 0
    assert not rest.any()
    return np.stack(blocks), np.stack(corr[-max(CONV_SPLIT - 1, 1):])


def _split_dot(a, b_ref):
    b = b_ref[...]
    a1 = a.astype(BF16)
    a2 = (a - a1.astype(F32)).astype(BF16)
    b1 = b.astype(BF16)
    b2 = (b - b1.astype(F32)).astype(BF16)
    dot = functools.partial(jnp.dot, preferred_element_type=F32)
    return dot(a1, b1) + (dot(a1, b2) + dot(a2, b1))


def _inproj_kernel(x_ref, mod_ref, g_ref, w_ref, ws_ref, cp_ref, sb_ref, sc_ref, cg_ref, sg_ref, sgt_ref, *,
                   blocks):
    x = x_ref[...]
    tm = x.shape[0]
    r = tm // CONV_SPLIT
    h = _modulated_norm(x, g_ref[...], mod_ref[0, 3:4, :], mod_ref[0, 4:5, :])
    hb = h.astype(BF16)
    small = _split_dot(h, ws_ref)
    sg_ref[...] = small
    sgt_ref[...] = small.T
    half = CONV_K // 2
    for c0, c1, kind in blocks:
        u = jnp.dot(hb, w_ref[:, c0:c1], preferred_element_type=F32)
        taps = [(u * cp_ref[half + dlt:half + dlt + 1, c0:c1]).astype(BF16)
                for dlt in range(-half, half + 1) if dlt]
        conv = [jnp.dot(sb_ref[0, s], jnp.concatenate([tp[s * r:(s + 1) * r] for tp in taps], axis=0),
                        preferred_element_type=F32) for s in range(CONV_SPLIT)]
        pieces = []
        for s in range(CONV_SPLIT):
            if s > 0:
                pieces[-1] = pieces[-1] + cross[:CONV_HALO]
                pieces.append(conv[s][:CONV_HALO] + cross[CONV_HALO:])
                pieces.append(conv[s][CONV_HALO:r - CONV_HALO] if s < CONV_SPLIT - 1 else conv[s][CONV_HALO:])
            else:
                pieces.append(conv[s][:r - CONV_HALO] if CONV_SPLIT > 1 else conv[s])
            if s < CONV_SPLIT - 1:
                pieces.append(conv[s][r - CONV_HALO:])
                b = (s + 1) * r
                cross = jnp.dot(sc_ref[0, s], jnp.concatenate([tp[b - CONV_HALO:b + CONV_HALO] for tp in taps],
                                                              axis=0), preferred_element_type=F32)
        acc = u * cp_ref[half:half + 1, c0:c1] + jnp.concatenate(pieces, axis=0)
        if kind == "x":
            acc = acc + cp_ref[CONV_K:CONV_K + 1, c0:c1]
        y = _silu(acc)
        if kind in ("q", "k"):
            parts = []
            for hh in range((c1 - c0) // GDN_D):
                yh = y[:, hh * GDN_D:(hh + 1) * GDN_D]
                inv = lax.rsqrt(jnp.sum(yh * yh, axis=-1, keepdims=True) + EPS)
                if kind == "q":
                    inv = inv * (GDN_D ** -0.5)
                parts.append(yh * inv)
            y = jnp.concatenate(parts, axis=1)
        cg_ref[:, c0:c1] = y


def _inproj(x1, mod3, g, w_cg, w_small, conv_p, *, tm, n_ctx_tiles, ctx_len, seq, blocks):
    t, d = x1.shape
    ncg = w_cg.shape[1]

    def row_of_tile(i):
        return jnp.where(i < n_ctx_tiles, 0, 1 + ((i - n_ctx_tiles) * tm) // seq)

    parts = [_conv_shift_parts(tm, ctx_len), _conv_shift_parts(tm, GRID_W)]
    sblk = jnp.asarray(np.stack([p[0] for p in parts]), dtype=BF16)
    scor = jnp.asarray(np.stack([p[1] for p in parts]), dtype=BF16)
    variant = lambda i: (jnp.where(i < n_ctx_tiles, 0, 1), 0, 0, 0)
    return pl.pallas_call(
        functools.partial(_inproj_kernel, blocks=blocks),
        out_shape=(jax.ShapeDtypeStruct((t, ncg), F32), jax.ShapeDtypeStruct((t, LANE), F32),
                   jax.ShapeDtypeStruct((LANE, t), F32)),
        grid=(t // tm,),
        in_specs=[pl.BlockSpec((tm, d), lambda i: (i, 0)),
                  pl.BlockSpec((1,) + mod3.shape[1:], lambda i: (row_of_tile(i), 0, 0)),
                  _const_spec(g.shape), _const_spec(w_cg.shape), _const_spec(w_small.shape),
                  _const_spec(conv_p.shape),
                  pl.BlockSpec((1,) + sblk.shape[1:], variant), pl.BlockSpec((1,) + scor.shape[1:], variant)],
        out_specs=(pl.BlockSpec((tm, ncg), lambda i: (i, 0)), pl.BlockSpec((tm, LANE), lambda i: (i, 0)),
                   pl.BlockSpec((LANE, tm), lambda i: (0, i))),
        compiler_params=pltpu.CompilerParams(vmem_limit_bytes=VMEM_LIMIT),
        name="inproj",
    )(x1, mod3, g, w_cg, w_small, conv_p, sblk, scor)


def _expand(x, e, pieces, e_left=False):
    acc = None
    r = x
    for _ in range(pieces):
        xp = r.astype(BF16)
        r = r - xp.astype(F32)
        term = jnp.dot(e, xp, preferred_element_type=F32) if e_left else jnp.dot(xp, e, preferred_element_type=F32)
        acc = term if acc is None else acc + term
    return acc


def _tri_masks(n, rev):
    ii = lax.broadcasted_iota(jnp.int32, (n, n), 0)
    jj = lax.broadcasted_iota(jnp.int32, (n, n), 1)
    if rev:
        return ii, jj, ii <= jj, ii < jj
    return ii, jj, ii >= jj, ii > jj


def _scan_block_index(b, s, rev, n_batch, ncc, nlc):
    lc = jnp.maximum(s - ncc, 0)
    if rev:
        ctx_blk = b * ncc + (ncc - 1 - jnp.minimum(s, ncc - 1))
        lat_blk = n_batch * ncc + b * nlc + (nlc - 1 - lc)
    else:
        ctx_blk = b * ncc + jnp.minimum(s, ncc - 1)
        lat_blk = n_batch * ncc + b * nlc + lc
    return jnp.where(s < ncc, ctx_blk, lat_blk)


def _scan_out_index(b, s, rev, ncc, nlc):
    lc = jnp.maximum(s - ncc, 0)
    return b * nlc + ((nlc - 1 - lc) if rev else lc)


def _gdn_kernel(qf, kf, vf, sgf, sgtf, qb, kb_, vb, sgb, sgtb, prow_ref, pcol_ref, of_ref, ob_ref, state_ref):
    @pl.when(pl.program_id(1) == 0)
    def _():
        state_ref[...] = jnp.zeros_like(state_ref)

    n = SCAN_CHUNK
    prow, pcol = prow_ref[...], pcol_ref[...]
    probs = []
    n_sub = qf.shape[0] // n
    for d, rev, refs, c in [(0, False, (qf, kf, vf, sgf, sgtf, of_ref), c) for c in range(n_sub)] + \
                           [(1, True, (qb, kb_, vb, sgb, sgtb, ob_ref), c) for c in range(n_sub)]:
        rows = slice(c * n, (c + 1) * n)
        q, k, v, sg = [r[rows, :] for r in refs[:4]]
        n_g = N_DIR * GDN_HEADS
        sgt = refs[4][0:n_g, rows]
        g_col = -jnp.exp(prow[0:1, :]) * _softplus(sg + prow[1:2, :])
        g_row = -jnp.exp(pcol[0:n_g, 0:1]) * _softplus(sgt + pcol[0:n_g, 1:2])
        beta_col = jax.nn.sigmoid(sg)
        ii, jj, incl, strict = _tri_masks(n, rev)
        gc_col = _expand(g_col, incl.astype(BF16), 3, e_left=True)
        gc_row = _expand(g_row, (ii >= jj if rev else ii <= jj).astype(BF16), 3)
        last = 0 if rev else n - 1
        for hh in range(GDN_HEADS):
            ci = d * GDN_HEADS + hh
            lo, hi = hh * GDN_D, (hh + 1) * GDN_D
            qh, kh, vh = q[:, lo:hi], k[:, lo:hi], v[:, lo:hi]
            gcol = gc_col[:, ci:ci + 1]
            grow = gc_row[ci:ci + 1, :]
            bcol = beta_col[:, N_DIR * GDN_HEADS + ci:N_DIR * GDN_HEADS + ci + 1]
            eg = jnp.exp(gcol)
            decay = jnp.exp(jnp.where(incl, gcol - grow, -jnp.inf))
            kb = kh * bcol
            g_last = gcol[last:last + 1, :]
            probs.append(dict(d=d, hh=hh, lo=lo, hi=hi, ii=ii, jj=jj, strict=strict, decay=decay, o_ref=refs[5],
                              rows=rows, order=(n_sub - 1 - c) if rev else c,
                              lhs=jnp.concatenate([kb, qh], axis=0).astype(BF16), kh=kh.astype(BF16),
                              rhs=jnp.concatenate([vh * bcol, kb * eg], axis=1).astype(BF16),
                              q_in=(qh * eg).astype(BF16), k_out=(kh * jnp.exp(g_last - gcol)).astype(BF16),
                              g_tot=jnp.exp(g_last)))
    kk = [_bdot_nt(p["lhs"], p["kh"]) for p in probs]
    for p, kkp in zip(probs, kk):
        p["a"] = jnp.where(p["strict"], kkp[:n] * p["decay"], 0.0)
        p["qk"] = (kkp[n:] * p["decay"]).astype(BF16)
    lg = 3
    for p in probs:
        blk = (p["ii"] >> lg) == (p["jj"] >> lg)
        p["a8"] = jnp.where(blk, p["a"], 0.0).astype(BF16)
        p["t"] = (p["ii"] == p["jj"]).astype(F32) - jnp.where(blk, p["a"], 0.0)
    x2 = [_bdot(p["a8"], p["a8"]).astype(BF16) for p in probs]
    x4 = [_bdot(x, x).astype(BF16) for x in x2]
    t1 = [p["t"] + _bdot(p["t"], x) for p, x in zip(probs, x2)]
    ts = [t + _bdot(t, x) for t, x in zip(t1, x4)]
    while (1 << lg) < n:
        s_blk = 1 << lg

        def pick(x, rev):
            return jnp.concatenate([x[b0 + (0 if rev else s_blk):b0 + (s_blk if rev else 2 * s_blk)]
                                    for b0 in range(0, n, 2 * s_blk)], axis=0)

        def merge(full, part, rev):
            rows = []
            for idx, b0 in enumerate(range(0, n, 2 * s_blk)):
                new = part[idx * s_blk:(idx + 1) * s_blk]
                old = (jnp.zeros((s_blk, n), F32) if full is None else
                       full[b0 + (s_blk if rev else 0):b0 + (2 * s_blk if rev else s_blk)])
                rows += [new, old] if rev else [old, new]
            return jnp.concatenate(rows, axis=0)

        bs = []
        for p in probs:
            rev = p["d"] == 1
            r = lax.broadcasted_iota(jnp.int32, (n // 2, n), 0)
            pi = ((r >> lg) << (lg + 1)) + (r & (s_blk - 1)) + (0 if rev else s_blk)
            pj = lax.broadcasted_iota(jnp.int32, (n // 2, n), 1)
            off = ((pi >> (lg + 1)) == (pj >> (lg + 1))) & ((pi >> lg) != (pj >> lg))
            bs.append(jnp.where(off, pick(p["a"], rev), 0.0).astype(BF16))
        tb = [t.astype(BF16) for t in ts]
        bt = [_bdot(b, t) for b, t in zip(bs, tb)]
        bt_full = [merge(None, x, p["d"] == 1).astype(BF16) for x, p in zip(bt, probs)]
        upd = [_bdot(pick(t, p["d"] == 1), x) for t, x, p in zip(ts, bt_full, probs)]
        ts = [merge(t, pick(t, p["d"] == 1) - u, p["d"] == 1) for t, u, p in zip(ts, upd, probs)]
        lg += 1
    sol = [_bdot(t, p["rhs"]) for t, p in zip(ts, probs)]
    state = {(d, hh): state_ref[d, hh] for d in range(N_DIR) for hh in range(GDN_HEADS)}
    for order in range(n_sub):
        cur = [(p, s) for p, s in zip(probs, sol) if p["order"] == order]
        stb = [state[p["d"], p["hh"]].astype(BF16) for p, _ in cur]
        ws = [jnp.dot(s[:, GDN_D:].astype(BF16), sb, preferred_element_type=F32) for (_, s), sb in zip(cur, stb)]
        v_new = [(s[:, :GDN_D] - w).astype(BF16) for (_, s), w in zip(cur, ws)]
        for (p, _), sb, vn in zip(cur, stb, v_new):
            p["o_ref"][p["rows"], p["lo"]:p["hi"]] = jnp.dot(
                jnp.concatenate([p["q_in"], p["qk"]], axis=1), jnp.concatenate([sb, vn], axis=0),
                preferred_element_type=F32)
            state[p["d"], p["hh"]] = state[p["d"], p["hh"]] * p["g_tot"] + lax.dot_general(
                p["k_out"], vn, (((0,), (0,)), ((), ())), preferred_element_type=F32)
    for (d, hh), sta in state.items():
        state_ref[d, hh] = sta


GDN_CHUNKS_PER_STEP = 2


def _gdn_scan(cg, sg, sgt, prow, pcol, *, n_batch, ncc, nlc, col_q):
    assert ncc % GDN_CHUNKS_PER_STEP == 0 and nlc % GDN_CHUNKS_PER_STEP == 0
    n = SCAN_CHUNK * GDN_CHUNKS_PER_STEP
    ncc, nlc = ncc // GDN_CHUNKS_PER_STEP, nlc // GDN_CHUNKS_PER_STEP
    dq = GDN_HEADS * GDN_D
    cq = col_q // dq

    def tok(rev, col):
        return pl.BlockSpec((n, dq), lambda b, s: (_scan_block_index(b, s, rev, n_batch, ncc, nlc), col))

    def small(rev):
        return pl.BlockSpec((n, LANE), lambda b, s: (_scan_block_index(b, s, rev, n_batch, ncc, nlc), 0))

    def small_t(rev):
        return pl.BlockSpec((LANE, n), lambda b, s: (0, _scan_block_index(b, s, rev, n_batch, ncc, nlc)))

    def out(rev):
        return pl.BlockSpec((n, dq), lambda b, s: (_scan_out_index(b, s, rev, ncc, nlc), 0))

    in_specs = []
    for rev in (False, True):
        in_specs += [tok(rev, cq), tok(rev, cq + 1), tok(rev, cq + 2), small(rev), small_t(rev)]
    in_specs += [_const_spec(prow.shape), _const_spec(pcol.shape)]
    t_lat = n_batch * nlc * n
    return pl.pallas_call(
        _gdn_kernel,
        out_shape=(jax.ShapeDtypeStruct((t_lat, dq), F32),) * 2,
        grid=(n_batch, ncc + nlc),
        in_specs=in_specs,
        out_specs=(out(False), out(True)),
        scratch_shapes=[pltpu.VMEM((N_DIR, GDN_HEADS, GDN_D, GDN_D), F32)],
        compiler_params=pltpu.CompilerParams(vmem_limit_bytes=VMEM_LIMIT,
                                             dimension_semantics=("arbitrary", "arbitrary")),
        name="gdn_scan",
    )(cg, cg, cg, sg, sgt, cg, cg, cg, sg, sgt, prow, pcol)


def _ssd_steps(xs_ref, bm_ref, cm_ref, sg_ref, sgt_ref, prow, pcol, ex_ref, state_ref, y_ref, d, rev, rows, order):
    n = SCAN_CHUNK
    bm, cm, sg, sgt = bm_ref[rows, :], cm_ref[rows, :], sg_ref[rows, :], sgt_ref[:, rows]
    n_heads = xs_ref.shape[1] // MB_HEADDIM
    hpg = n_heads // MB_GROUPS
    gw = hpg * MB_HEADDIM
    pw = 2 * MB_HEADDIM
    ii, jj, incl, _ = _tri_masks(n, rev)
    f0 = 2 * N_DIR * GDN_HEADS + d * n_heads
    heads = lambda a: a[:, f0:f0 + n_heads]
    dt_col = _softplus(sg + prow[3:4, :])
    dt_row = _softplus(sgt[f0:f0 + n_heads, :] + pcol[f0:f0 + n_heads, 3:4])
    acs_col = _expand(dt_col * -jnp.exp(prow[2:3, :]), incl.astype(BF16), 3, e_left=True)
    acs_row = _expand(dt_row * -jnp.exp(pcol[f0:f0 + n_heads, 2:3]),
                      (ii >= jj if rev else ii <= jj).astype(BF16), 3)
    last = 0 if rev else n - 1
    tot = acs_col[last:last + 1, :]
    w_in = heads(dt_col * jnp.exp(tot - acs_col))
    w_out = heads(jnp.exp(acs_col))
    w_tot = heads(jnp.broadcast_to(jnp.exp(tot), (8, LANE)))
    yield
    lane = lax.broadcasted_iota(jnp.int32, (n, pw), 1)
    left = lane < MB_HEADDIM
    grp = []
    for g in range(MB_GROUPS):
        bg = bm[:, g * MB_STATE:(g + 1) * MB_STATE]
        cgm = cm[:, g * MB_STATE:(g + 1) * MB_STATE].astype(BF16)
        grp.append((cgm, _bdot_nt(cgm, bg), bg.T.astype(BF16)))
    pairs = [(g, pr) for g in range(MB_GROUPS) for pr in range(hpg // 2)]
    lo_of = lambda g, pr: (g * hpg + 2 * pr) * MB_HEADDIM
    spread = [ex_ref[:, lo_of(g, pr):lo_of(g, pr) + pw] for g, pr in pairs]
    e_in = [_expand(w_in, sp, 1) for sp in spread]
    e_out = [_expand(w_out, sp, 1) for sp in spread]
    e_tot = [_expand(w_tot, sp, 3)[0:1, :] for sp in spread]
    yield
    xps = [xs_ref[rows, lo_of(g, pr):lo_of(g, pr) + pw] for g, pr in pairs]
    for _ in range(2 * order):
        yield
    sts = [state_ref[d, g, :, lo_of(g, pr) - g * gw:lo_of(g, pr) - g * gw + pw] for g, pr in pairs]
    y_off = [jnp.dot(grp[g][0], st.astype(BF16), preferred_element_type=F32) * eo
             for (g, pr), st, eo in zip(pairs, sts, e_out)]
    yield
    upd = [jnp.dot(grp[g][2], (xp * ei).astype(BF16), preferred_element_type=F32)
           for (g, pr), xp, ei in zip(pairs, xps, e_in)]
    for (g, pr), st, et, u in zip(pairs, sts, e_tot, upd):
        sl = lo_of(g, pr) - g * gw
        state_ref[d, g, :, sl:sl + pw] = st * et + u
    yield
    m2s = []
    for g, pr in pairs:
        ms = []
        for hd in (g * hpg + 2 * pr, g * hpg + 2 * pr + 1):
            ccol = acs_col[:, f0 + hd:f0 + hd + 1]
            crow = acs_row[hd:hd + 1, :]
            seg = jnp.exp(jnp.where(incl, ccol - crow, -jnp.inf))
            ms.append(grp[g][1] * seg * dt_row[hd:hd + 1, :])
        m2s.append(jnp.concatenate(ms, axis=1).astype(BF16))
    xbds = [jnp.concatenate([jnp.where(left, xp, 0.0), jnp.where(left, 0.0, xp)], axis=0).astype(BF16)
            for xp in xps]
    yield
    for (g, pr), m2, xbd, yo in zip(pairs, m2s, xbds, y_off):
        lo = lo_of(g, pr)
        y_ref[rows, lo:lo + pw] = jnp.dot(m2, xbd, preferred_element_type=F32) + yo


def _ssd_kernel(xf, bf, cf, sgf, sgtf, xb, bb, cb, sgb, sgtb, prow_ref, pcol_ref, ex_ref, yf_ref, yb_ref, state_ref):
    @pl.when(pl.program_id(1) == 0)
    def _():
        state_ref[...] = jnp.zeros_like(state_ref)

    prow, pcol = prow_ref[...], pcol_ref[...]
    n_sub = xf.shape[0] // SCAN_CHUNK
    live = []
    for c in range(n_sub):
        rows = slice(c * SCAN_CHUNK, (c + 1) * SCAN_CHUNK)
        live.append(_ssd_steps(xf, bf, cf, sgf, sgtf, prow, pcol, ex_ref, state_ref, yf_ref, 0, False, rows, c))
        live.append(_ssd_steps(xb, bb, cb, sgb, sgtb, prow, pcol, ex_ref, state_ref, yb_ref, 1, True, rows,
                               n_sub - 1 - c))
    while live:
        for steps in list(live):
            if next(steps, "done") == "done":
                live.remove(steps)


SSD_CHUNKS_PER_STEP = 2


def _ssd_scan(cg, sg, sgt, prow, pcol, ex, *, n_batch, ncc, nlc, d_inner, col_b):
    assert ncc % SSD_CHUNKS_PER_STEP == 0 and nlc % SSD_CHUNKS_PER_STEP == 0
    n = SCAN_CHUNK * SSD_CHUNKS_PER_STEP
    ncc, nlc = ncc // SSD_CHUNKS_PER_STEP, nlc // SSD_CHUNKS_PER_STEP
    bw = MB_GROUPS * MB_STATE
    cb_ = col_b // bw

    def blk(rev, width, col):
        return pl.BlockSpec((n, width), lambda b, s: (_scan_block_index(b, s, rev, n_batch, ncc, nlc), col))

    def small_t(rev):
        return pl.BlockSpec((LANE, n), lambda b, s: (0, _scan_block_index(b, s, rev, n_batch, ncc, nlc)))

    def out(rev):
        return pl.BlockSpec((n, d_inner), lambda b, s: (_scan_out_index(b, s, rev, ncc, nlc), 0))

    in_specs = []
    for rev in (False, True):
        in_specs += [blk(rev, d_inner, 0), blk(rev, bw, cb_), blk(rev, bw, cb_ + 1), blk(rev, LANE, 0), small_t(rev)]
    in_specs += [_const_spec(prow.shape), _const_spec(pcol.shape), _const_spec(ex.shape)]
    t_lat = n_batch * nlc * n
    return pl.pallas_call(
        _ssd_kernel,
        out_shape=(jax.ShapeDtypeStruct((t_lat, d_inner), F32),) * 2,
        grid=(n_batch, ncc + nlc),
        in_specs=in_specs,
        out_specs=(out(False), out(True)),
        scratch_shapes=[pltpu.VMEM((N_DIR, MB_GROUPS, MB_STATE, d_inner // MB_GROUPS), F32)],
        compiler_params=pltpu.CompilerParams(vmem_limit_bytes=VMEM_LIMIT,
                                             dimension_semantics=("arbitrary", "arbitrary")),
        name="ssd_scan",
    )(cg, cg, cg, sg, sgt, cg, cg, cg, sg, sgt, prow, pcol, ex)


def _mixout_kernel(x_ref, mod_ref, g_ref, wp_ref, of_ref, ob_ref, yf_ref, yb_ref, xs_ref,
                   rows_ref, wbg_ref, wbm_ref, wo_ref, o_ref, *, d_model, d_v, d_inner):
    x = x_ref[...]
    hb = _modulated_norm(x, g_ref[...], mod_ref[0, 3:4, :], mod_ref[0, 4:5, :]).astype(BF16)
    gate = mod_ref[0, 5:6, :]
    z = jnp.dot(hb, wp_ref[...], preferred_element_type=F32)
    za = z[:, :d_v]
    zb = z[:, d_v:d_v + d_inner]
    ga = z[:, d_v + d_inner:d_v + d_inner + d_model]
    gb = z[:, d_v + d_inner + d_model:]

    o = of_ref[...] + ob_ref[...]
    parts = []
    for hh in range(d_v // GDN_D):
        oh = o[:, hh * GDN_D:(hh + 1) * GDN_D]
        parts.append(oh * lax.rsqrt(jnp.mean(oh * oh, axis=-1, keepdims=True) + EPS))
    ya = jnp.concatenate(parts, axis=1) * rows_ref[0:1, :d_v] * _silu(za)

    yb = (yf_ref[...] + yb_ref[...] + rows_ref[1:2, :] * xs_ref[...]) * _silu(zb)
    gw = d_inner // MB_GROUPS
    parts = []
    for g in range(MB_GROUPS):
        yg = yb[:, g * gw:(g + 1) * gw]
        parts.append(yg * lax.rsqrt(jnp.mean(yg * yg, axis=-1, keepdims=True) + EPS))
    yb = jnp.concatenate(parts, axis=1) * rows_ref[2:3, :]

    merged = (jax.nn.sigmoid(ga) * _bdot(ya, wbg_ref[...]) + jax.nn.sigmoid(gb) * _bdot(yb, wbm_ref[...]))
    o_ref[...] = x + gate * _bdot(merged, wo_ref[...])


def _mixout(x1, mod3, g, w_plain, o_f, o_b, y_f, y_b, cg, rows, wbg, wbm, wo, *, tm, n_ctx_tiles, seq):
    t_lat, d_v = o_f.shape
    d_inner = y_f.shape[1]
    d = x1.shape[1]
    return pl.pallas_call(
        functools.partial(_mixout_kernel, d_model=d, d_v=d_v, d_inner=d_inner),
        out_shape=jax.ShapeDtypeStruct((t_lat, d), F32),
        grid=(t_lat // tm,),
        in_specs=[pl.BlockSpec((tm, d), lambda i: (i + n_ctx_tiles, 0)),
                  pl.BlockSpec((1,) + mod3.shape[1:], lambda i: (1 + (i * tm) // seq, 0, 0)),
                  _const_spec(g.shape), _const_spec(w_plain.shape),
                  pl.BlockSpec((tm, d_v), lambda i: (i, 0)), pl.BlockSpec((tm, d_v), lambda i: (i, 0)),
                  pl.BlockSpec((tm, d_inner), lambda i: (i, 0)), pl.BlockSpec((tm, d_inner), lambda i: (i, 0)),
                  pl.BlockSpec((tm, d_inner), lambda i: (i + n_ctx_tiles, 0)),
                  _const_spec(rows.shape), _const_spec(wbg.shape), _const_spec(wbm.shape), _const_spec(wo.shape)],
        out_specs=pl.BlockSpec((tm, d), lambda i: (i, 0)),
        compiler_params=pltpu.CompilerParams(vmem_limit_bytes=VMEM_LIMIT),
        name="mixout",
    )(x1, mod3, g, w_plain, o_f, o_b, y_f, y_b, cg, rows, wbg, wbm, wo)


def kernel(x, c, ctx, c_ctx, w_ada, b_ada, norm_g, ffn_w_gu, ffn_w_down, w_in, gdn_conv_w, gdn_A_log, gdn_dt_bias, gdn_norm_g, mb_conv_w, mb_conv_b, mb_A_log, mb_dt_bias, mb_D, mb_norm_g, w_branch_gdn, w_branch_mb, w_out, final_g):
    n_batch, seq, d = x.shape
    ctx_len = ctx.shape[1]
    assert w_ada.shape[0] == 1, "single-layer operation"
    d_qk = GDN_HEADS * GDN_D
    d_v = GDN_HEADS * GDN_D
    d_inner = mb_norm_g.shape[1]
    n_mb_heads = d_inner // MB_HEADDIM
    d_bc = MB_GROUPS * MB_STATE
    tm = 256
    t_ctx, t_lat = n_batch * ctx_len, n_batch * seq
    assert t_ctx % tm == 0 and seq % tm == 0 and tm % GRID_W == 0 and tm % ctx_len == 0
    assert ctx_len % SCAN_CHUNK == 0 and seq % SCAN_CHUNK == 0
    assert N_DIR * (2 * GDN_HEADS + n_mb_heads) <= LANE
    n_ctx_tiles = t_ctx // tm

    rows = 16
    cc = jnp.concatenate([c_ctx[None, :], c, jnp.zeros((rows - 1 - n_batch, d), F32)], axis=0)
    mod3 = _adaln(cc, w_ada[0], b_ada[0][None, :]).reshape(rows, 9, d)

    def row_all(i):
        return jnp.where(i < n_ctx_tiles, 0, 1 + ((i - n_ctx_tiles) * tm) // seq)

    tm_ffn = 2 * tm
    assert t_ctx % tm_ffn == 0 and seq % tm_ffn == 0 and tm_ffn % ctx_len == 0 and tm_ffn % GRID_W == 0

    def row_ffn(i):
        return jnp.where(i < t_ctx // tm_ffn, 0, 1 + ((i - t_ctx // tm_ffn) * tm_ffn) // seq)

    ffn_gu, ffn_down = ffn_w_gu[0].astype(BF16), ffn_w_down[0].astype(BF16)
    x1 = _ffn(ctx.reshape(t_ctx, d), x.reshape(t_lat, d), mod3, norm_g[0, 0][None, :], ffn_gu, ffn_down,
              final_g[None, :], k=0, row_of_tile=row_ffn, tm=tm_ffn, final=False, name="ffn1")

    sizes = (2 * d_qk + d_v, d_v, N_DIR * GDN_HEADS, N_DIR * GDN_HEADS, d_inner, d_inner + 2 * d_bc,
             N_DIR * n_mb_heads)
    o_qkv, o_za, o_a, o_beta, o_zb, o_xbc, o_dt, o_gates = [sum(sizes[:j]) for j in range(len(sizes) + 1)]

    def regroup(a, off_qkv, off_xbc):
        return jnp.concatenate([a[..., off_xbc:off_xbc + d_inner], a[..., off_qkv:off_qkv + 2 * d_qk + d_v],
                                a[..., off_xbc + d_inner:off_xbc + d_inner + 2 * d_bc]], axis=-1)

    n_small = N_DIR * (2 * GDN_HEADS + n_mb_heads)
    w_cg, w_small, w_plain = _regroup_w_in(
        w_in[0].T,
        cg_cols=((o_xbc, o_xbc + d_inner), (o_qkv, o_za), (o_xbc + d_inner, o_dt)),
        small_cols=((o_a, o_zb), (o_dt, o_gates)),
        plain_cols=((o_za, o_a), (o_zb, o_xbc), (o_gates, o_gates + 2 * d)))
    conv_all = jnp.concatenate([gdn_conv_w[0], mb_conv_w[0]], axis=1)
    bias_all = jnp.concatenate([jnp.zeros((1, 2 * d_qk + d_v), F32), mb_conv_b[0][None, :]], axis=1)
    conv_p = regroup(jnp.concatenate([conv_all, bias_all, jnp.zeros((2, conv_all.shape[1]), F32)], axis=0),
                     0, 2 * d_qk + d_v)
    col_q = d_inner
    col_b = d_inner + 2 * d_qk + d_v
    blocks = ((0, d_inner // 2, "x"), (d_inner // 2, d_inner, "x"), (col_q, col_q + d_qk, "q"),
              (col_q + d_qk, col_q + 2 * d_qk, "k"), (col_q + 2 * d_qk, col_b, "v"), (col_b, col_b + 2 * d_bc, "x"))
    cg, sg, sgt = _inproj(x1, mod3, norm_g[0, 1][None, :], w_cg, w_small, conv_p, tm=tm_ffn,
                          n_ctx_tiles=t_ctx // tm_ffn,
                          ctx_len=ctx_len, seq=seq, blocks=blocks)

    pad = jnp.zeros((LANE - n_small,), F32)
    zeros_g = jnp.zeros((N_DIR * GDN_HEADS,), F32)
    zeros_m = jnp.zeros((N_DIR * n_mb_heads,), F32)
    prow = jnp.stack([jnp.concatenate([gdn_A_log[0].reshape(-1), zeros_g, zeros_m, pad]),
                      jnp.concatenate([gdn_dt_bias[0].reshape(-1), zeros_g, zeros_m, pad]),
                      jnp.concatenate([zeros_g, zeros_g, mb_A_log[0].reshape(-1), pad]),
                      jnp.concatenate([zeros_g, zeros_g, mb_dt_bias[0].reshape(-1), pad])]
                     + [jnp.zeros((LANE,), F32)] * 4)
    pcol = prow.T
    ncc, nlc = ctx_len // SCAN_CHUNK, seq // SCAN_CHUNK
    o_f, o_b = _gdn_scan(cg, sg, sgt, prow, pcol, n_batch=n_batch, ncc=ncc, nlc=nlc, col_q=col_q)

    ex = (jnp.arange(n_mb_heads)[:, None] == (jnp.arange(d_inner) // MB_HEADDIM)[None, :]).astype(BF16)
    y_f, y_b = _ssd_scan(cg, sg, sgt, prow, pcol, ex, n_batch=n_batch, ncc=ncc, nlc=nlc, d_inner=d_inner, col_b=col_b)

    rows3 = jnp.stack([jnp.concatenate([jnp.tile(gdn_norm_g[0], GDN_HEADS), jnp.zeros((d_inner - d_v,), F32)]),
                       jnp.repeat(mb_D[0], MB_HEADDIM), mb_norm_g[0]] + [jnp.zeros((d_inner,), F32)] * 5)
    x2 = _mixout(x1, mod3, norm_g[0, 1][None, :], w_plain, o_f, o_b, y_f, y_b, cg, rows3,
                 w_branch_gdn[0].astype(BF16), w_branch_mb[0].astype(BF16), w_out[0].astype(BF16),
                 tm=tm, n_ctx_tiles=n_ctx_tiles, seq=seq)

    out = _ffn(None, x2, mod3, norm_g[0, 2][None, :], ffn_gu, ffn_down, final_g[None, :], k=2, row_of_tile=lambda i: 1 + (i * tm_ffn) // seq, tm=tm_ffn, final=True, name="ffn2")
    return out.reshape(n_batch, seq, d)
```

```python
import functools

import jax
import jax.numpy as jnp
import numpy as np
from jax import lax
from jax.experimental import pallas as pl
from jax.experimental.pallas import tpu as pltpu

F32 = jnp.float32
BF16 = jnp.bfloat16

EPS = 1e-6
FFN_RES = 0.5
GRID_W = 64
CONV_K = 5
GDN_HEADS = 8
GDN_D = 128
MB_HEADDIM = 64
MB_GROUPS = 2
MB_STATE = 128
N_DIR = 2

LANE = 128
SCAN_CHUNK = 128
VMEM_LIMIT = 56 * 1024 * 1024


def _const_spec(shape):
    nd = len(shape)
    return pl.BlockSpec(shape, lambda *_: (0,) * nd, pipeline_mode=pl.Buffered(1))


def _silu(x):
    return x * jax.nn.sigmoid(x)


def _softplus(x):
    return jnp.maximum(x, 0.0) + jnp.log(1.0 + jnp.exp(-jnp.abs(x)))


def _bdot(a, b):
    return jnp.dot(a.astype(BF16), b.astype(BF16), preferred_element_type=F32)


def _bdot_nt(a, b):
    return lax.dot_general(a.astype(BF16), b.astype(BF16), (((1,), (1,)), ((), ())),
                           preferred_element_type=F32)


def _modulated_norm(x, g, shift, scale):
    ms = jnp.mean(x * x, axis=-1, keepdims=True)
    return x * lax.rsqrt(ms + EPS) * g * (1.0 + scale) + shift


def _adaln_kernel(c_ref, w_ref, b_ref, o_ref):
    s = _silu(c_ref[...])
    o_ref[...] = _split_dot(s, w_ref) + b_ref[...]


def _adaln(cc, w, b, tn=1152):
    rows, d = cc.shape
    n = w.shape[1]
    return pl.pallas_call(
        _adaln_kernel,
        out_shape=jax.ShapeDtypeStruct((rows, n), F32),
        grid=(n // tn,),
        in_specs=[pl.BlockSpec((rows, d), lambda j: (0, 0)),
                  pl.BlockSpec((d, tn), lambda j: (0, j)),
                  pl.BlockSpec((1, tn), lambda j: (0, j))],
        out_specs=pl.BlockSpec((rows, tn), lambda j: (0, j)),
        compiler_params=pltpu.CompilerParams(vmem_limit_bytes=VMEM_LIMIT),
        name="adaln",
    )(cc, w, b)


def _ffn_kernel(c_ref, x_ref, mod_ref, g_ref, wgu_ref, wd_ref, fg_ref, o_ref, *, k, d_ff, final, n_ctx_tiles):
    if n_ctx_tiles:
        x = jnp.where(pl.program_id(0) < n_ctx_tiles, c_ref[...], x_ref[...])
    else:
        x = x_ref[...]
    shift = mod_ref[0, 3 * k:3 * k + 1, :]
    scale = mod_ref[0, 3 * k + 1:3 * k + 2, :]
    gate = mod_ref[0, 3 * k + 2:3 * k + 3, :]
    hb = _modulated_norm(x, g_ref[...], shift, scale).astype(BF16)
    gt = jnp.dot(hb, wgu_ref[0, :, :d_ff], preferred_element_type=F32)
    up = jnp.dot(hb, wgu_ref[0, :, d_ff:], preferred_element_type=F32)
    act = (_silu(gt) * up).astype(BF16)
    y = x + FFN_RES * gate * jnp.dot(act, wd_ref[0], preferred_element_type=F32)
    if final:
        ms = jnp.mean(y * y, axis=-1, keepdims=True)
        y = y * lax.rsqrt(ms + EPS) * fg_ref[...]
    o_ref[...] = y


def _ffn(xc, x, mod3, g, wgu, wd, fg, *, k, row_of_tile, tm, final, name):
    t_lat, d = x.shape
    d_ff = wd.shape[1]
    which = k // 2
    w_spec = lambda w: pl.BlockSpec((1,) + w.shape[1:], lambda i: (which, 0, 0), pipeline_mode=pl.Buffered(1))
    if xc is None:
        nct = 0
        xc, c_spec = x, pl.BlockSpec((8, d), lambda i: (0, 0))
    else:
        nct = xc.shape[0] // tm
        c_spec = pl.BlockSpec((tm, d), lambda i: (jnp.minimum(i, nct - 1), 0))
    t = nct * tm + t_lat
    return pl.pallas_call(
        functools.partial(_ffn_kernel, k=k, d_ff=d_ff, final=final, n_ctx_tiles=nct),
        out_shape=jax.ShapeDtypeStruct((t, d), F32),
        grid=(t // tm,),
        in_specs=[c_spec, pl.BlockSpec((tm, d), lambda i: (jnp.maximum(i - nct, 0), 0)),
                  pl.BlockSpec((1,) + mod3.shape[1:], lambda i: (row_of_tile(i), 0, 0)),
                  _const_spec(g.shape), w_spec(wgu), w_spec(wd), _const_spec(fg.shape)],
        out_specs=pl.BlockSpec((tm, d), lambda i: (i, 0)),
        compiler_params=pltpu.CompilerParams(vmem_limit_bytes=VMEM_LIMIT),
        name=name,
    )(xc, x, mod3, g, wgu, wd, fg)


def _regroup_kernel(offs_ref, wt_ref, *rest, n_cg_blocks, n_small_in):
    del offs_ref
    small_in = rest[:n_small_in]
    cg_ref, small_ref, plain_ref = rest[n_small_in:]
    j = pl.program_id(0)
    blk = wt_ref[...].T.astype(BF16)

    @pl.when(j < n_cg_blocks)
    def _():
        cg_ref[...] = blk

    @pl.when(j >= n_cg_blocks)
    def _():
        plain_ref[...] = blk

    @pl.when(j == 0)
    def _():
        rows = [r[...] for r in small_in]
        pad = LANE - sum(r.shape[0] for r in rows)
        small_ref[...] = jnp.concatenate(rows + [jnp.zeros((pad, rows[0].shape[1]), F32)], axis=0).T


def _regroup_w_in(wt, cg_cols, small_cols, plain_cols, bw=512):
    d = wt.shape[1]
    width = lambda cols: sum(b - a for a, b in cols)
    assert all((b - a) % bw == 0 for a, b in cg_cols + plain_cols)
    starts = lambda cols: [s for a, b in cols for s in range(a, b, bw)]
    all_starts = starts(cg_cols) + starts(plain_cols)
    unit = 8
    assert all(s % unit == 0 for s in all_starts)
    offs = jnp.asarray([s // unit for s in all_starts], jnp.int32)
    ncg = len(starts(cg_cols))
    in_specs = [pl.BlockSpec((pl.Element(bw), pl.Element(d)), lambda j, offs: (offs[j] * unit, 0))]
    in_specs += [pl.BlockSpec((pl.Element(b - a), pl.Element(d)), lambda j, offs, a=a: (a, 0))
                 for a, b in small_cols]
    out_specs = (pl.BlockSpec((d, bw), lambda j, offs: (0, jnp.minimum(j, ncg - 1))),
                 pl.BlockSpec((d, LANE), lambda j, offs: (0, 0)),
                 pl.BlockSpec((d, bw), lambda j, offs: (0, jnp.maximum(j - ncg, 0))))
    return pl.pallas_call(
        functools.partial(_regroup_kernel, n_cg_blocks=ncg, n_small_in=len(small_cols)),
        out_shape=(jax.ShapeDtypeStruct((d, width(cg_cols)), BF16), jax.ShapeDtypeStruct((d, LANE), F32),
                   jax.ShapeDtypeStruct((d, width(plain_cols)), BF16)),
        grid_spec=pltpu.PrefetchScalarGridSpec(num_scalar_prefetch=1, grid=(offs.shape[0],),
                                               in_specs=in_specs, out_specs=out_specs),
        compiler_params=pltpu.CompilerParams(vmem_limit_bytes=VMEM_LIMIT,
                                             dimension_semantics=("arbitrary",)),
        name="regroup_w_in",
    )(offs, wt, *([wt] * len(small_cols)))


def _conv_shift_matrix(tm, period):
    t = np.arange(tm)[:, None]
    s = np.arange(tm)[None, :]
    half = CONV_K // 2
    mats = [(s == t + dlt) & (t // period == (t + dlt) // period)
            for dlt in range(-half, half + 1) if dlt]
    return np.concatenate(mats, axis=1).astype(np.float32)


CONV_SPLIT = 4
CONV_HALO = 16


def _conv_shift_parts(tm, period):
    taps = CONV_K - 1
    r = tm // CONV_SPLIT
    rest = _conv_shift_matrix(tm, period).reshape(tm, taps, tm)
    blocks = []
    for s in range(CONV_SPLIT):
        rows = slice(s * r, (s + 1) * r)
        blocks.append(rest[rows, :, rows].reshape(r, taps * r).copy())
        rest[rows, :, rows] = 0
    corr = [np.zeros((2 * CONV_HALO, taps * 2 * CONV_HALO), np.float32)]
    for b in range(1, CONV_SPLIT):
        win = slice(b * r - CONV_HALO, b * r + CONV_HALO)
        corr.append(rest[win, :, win].reshape(2 * CONV_HALO, taps * 2 * CONV_HALO).copy())
        rest[win, :, win] = 0
    assert not rest.any()
    return np.stack(blocks), np.stack(corr[-max(CONV_SPLIT - 1, 1):])


def _split_dot(a, b_ref):
    b = b_ref[...]
    a1 = a.astype(BF16)
    a2 = (a - a1.astype(F32)).astype(BF16)
    b1 = b.astype(BF16)
    b2 = (b - b1.astype(F32)).astype(BF16)
    dot = functools.partial(jnp.dot, preferred_element_type=F32)
    return dot(a1, b1) + (dot(a1, b2) + dot(a2, b1))


def _inproj_kernel(x_ref, mod_ref, g_ref, w_ref, ws_ref, cp_ref, sb_ref, sc_ref, cg_ref, sg_ref, sgt_ref, *,
                   blocks):
    x = x_ref[...]
    tm = x.shape[0]
    r = tm // CONV_SPLIT
    h = _modulated_norm(x, g_ref[...], mod_ref[0, 3:4, :], mod_ref[0, 4:5, :])
    hb = h.astype(BF16)
    small = _split_dot(h, ws_ref)
    sg_ref[...] = small
    sgt_ref[...] = small.T
    half = CONV_K // 2
    for c0, c1, kind in blocks:
        u = jnp.dot(hb, w_ref[:, c0:c1], preferred_element_type=F32)
        taps = [(u * cp_ref[half + dlt:half + dlt + 1, c0:c1]).astype(BF16)
                for dlt in range(-half, half + 1) if dlt]
        conv = [jnp.dot(sb_ref[0, s], jnp.concatenate([tp[s * r:(s + 1) * r] for tp in taps], axis=0),
                        preferred_element_type=F32) for s in range(CONV_SPLIT)]
        pieces = []
        for s in range(CONV_SPLIT):
            if s > 0:
                pieces[-1] = pieces[-1] + cross[:CONV_HALO]
                pieces.append(conv[s][:CONV_HALO] + cross[CONV_HALO:])
                pieces.append(conv[s][CONV_HALO:r - CONV_HALO] if s < CONV_SPLIT - 1 else conv[s][CONV_HALO:])
            else:
                pieces.append(conv[s][:r - CONV_HALO] if CONV_SPLIT > 1 else conv[s])
            if s < CONV_SPLIT - 1:
                pieces.append(conv[s][r - CONV_HALO:])
                b = (s + 1) * r
                cross = jnp.dot(sc_ref[0, s], jnp.concatenate([tp[b - CONV_HALO:b + CONV_HALO] for tp in taps],
                                                              axis=0), preferred_element_type=F32)
        acc = u * cp_ref[half:half + 1, c0:c1] + jnp.concatenate(pieces, axis=0)
        if kind == "x":
            acc = acc + cp_ref[CONV_K:CONV_K + 1, c0:c1]
        y = _silu(acc)
        if kind in ("q", "k"):
            parts = []
            for hh in range((c1 - c0) // GDN_D):
                yh = y[:, hh * GDN_D:(hh + 1) * GDN_D]
                inv = lax.rsqrt(jnp.sum(yh * yh, axis=-1, keepdims=True) + EPS)
                if kind == "q":
                    inv = inv * (GDN_D ** -0.5)
                parts.append(yh * inv)
            y = jnp.concatenate(parts, axis=1)
        cg_ref[:, c0:c1] = y


def _inproj(x1, mod3, g, w_cg, w_small, conv_p, *, tm, n_ctx_tiles, ctx_len, seq, blocks):
    t, d = x1.shape
    ncg = w_cg.shape[1]

    def row_of_tile(i):
        return jnp.where(i < n_ctx_tiles, 0, 1 + ((i - n_ctx_tiles) * tm) // seq)

    parts = [_conv_shift_parts(tm, ctx_len), _conv_shift_parts(tm, GRID_W)]
    sblk = jnp.asarray(np.stack([p[0] for p in parts]), dtype=BF16)
    scor = jnp.asarray(np.stack([p[1] for p in parts]), dtype=BF16)
    variant = lambda i: (jnp.where(i < n_ctx_tiles, 0, 1), 0, 0, 0)
    return pl.pallas_call(
        functools.partial(_inproj_kernel, blocks=blocks),
        out_shape=(jax.ShapeDtypeStruct((t, ncg), F32), jax.ShapeDtypeStruct((t, LANE), F32),
                   jax.ShapeDtypeStruct((LANE, t), F32)),
        grid=(t // tm,),
        in_specs=[pl.BlockSpec((tm, d), lambda i: (i, 0)),
                  pl.BlockSpec((1,) + mod3.shape[1:], lambda i: (row_of_tile(i), 0, 0)),
                  _const_spec(g.shape), _const_spec(w_cg.shape), _const_spec(w_small.shape),
                  _const_spec(conv_p.shape),
                  pl.BlockSpec((1,) + sblk.shape[1:], variant), pl.BlockSpec((1,) + scor.shape[1:], variant)],
        out_specs=(pl.BlockSpec((tm, ncg), lambda i: (i, 0)), pl.BlockSpec((tm, LANE), lambda i: (i, 0)),
                   pl.BlockSpec((LANE, tm), lambda i: (0, i))),
        compiler_params=pltpu.CompilerParams(vmem_limit_bytes=VMEM_LIMIT),
        name="inproj",
    )(x1, mod3, g, w_cg, w_small, conv_p, sblk, scor)


def _expand(x, e, pieces, e_left=False):
    acc = None
    r = x
    for _ in range(pieces):
        xp = r.astype(BF16)
        r = r - xp.astype(F32)
        term = jnp.dot(e, xp, preferred_element_type=F32) if e_left else jnp.dot(xp, e, preferred_element_type=F32)
        acc = term if acc is None else acc + term
    return acc


def _tri_masks(n, rev):
    ii = lax.broadcasted_iota(jnp.int32, (n, n), 0)
    jj = lax.broadcasted_iota(jnp.int32, (n, n), 1)
    if rev:
        return ii, jj, ii <= jj, ii < jj
    return ii, jj, ii >= jj, ii > jj


def _scan_block_index(b, s, rev, n_batch, ncc, nlc):
    lc = jnp.maximum(s - ncc, 0)
    if rev:
        ctx_blk = b * ncc + (ncc - 1 - jnp.minimum(s, ncc - 1))
        lat_blk = n_batch * ncc + b * nlc + (nlc - 1 - lc)
    else:
        ctx_blk = b * ncc + jnp.minimum(s, ncc - 1)
        lat_blk = n_batch * ncc + b * nlc + lc
    return jnp.where(s < ncc, ctx_blk, lat_blk)


def _scan_out_index(b, s, rev, ncc, nlc):
    lc = jnp.maximum(s - ncc, 0)
    return b * nlc + ((nlc - 1 - lc) if rev else lc)


def _gdn_kernel(qf, kf, vf, sgf, sgtf, qb, kb_, vb, sgb, sgtb, prow_ref, pcol_ref, of_ref, ob_ref, state_ref):
    @pl.when(pl.program_id(1) == 0)
    def _():
        state_ref[...] = jnp.zeros_like(state_ref)

    n = SCAN_CHUNK
    prow, pcol = prow_ref[...], pcol_ref[...]
    probs = []
    n_sub = qf.shape[0] // n
    for d, rev, refs, c in [(0, False, (qf, kf, vf, sgf, sgtf, of_ref), c) for c in range(n_sub)] + \
                           [(1, True, (qb, kb_, vb, sgb, sgtb, ob_ref), c) for c in range(n_sub)]:
        rows = slice(c * n, (c + 1) * n)
        q, k, v, sg = [r[rows, :] for r in refs[:4]]
        n_g = N_DIR * GDN_HEADS
        sgt = refs[4][0:n_g, rows]
        g_col = -jnp.exp(prow[0:1, :]) * _softplus(sg + prow[1:2, :])
        g_row = -jnp.exp(pcol[0:n_g, 0:1]) * _softplus(sgt + pcol[0:n_g, 1:2])
        beta_col = jax.nn.sigmoid(sg)
        ii, jj, incl, strict = _tri_masks(n, rev)
        gc_col = _expand(g_col, incl.astype(BF16), 3, e_left=True)
        gc_row = _expand(g_row, (ii >= jj if rev else ii <= jj).astype(BF16), 3)
        last = 0 if rev else n - 1
        for hh in range(GDN_HEADS):
            ci = d * GDN_HEADS + hh
            lo, hi = hh * GDN_D, (hh + 1) * GDN_D
            qh, kh, vh = q[:, lo:hi], k[:, lo:hi], v[:, lo:hi]
            gcol = gc_col[:, ci:ci + 1]
            grow = gc_row[ci:ci + 1, :]
            bcol = beta_col[:, N_DIR * GDN_HEADS + ci:N_DIR * GDN_HEADS + ci + 1]
            eg = jnp.exp(gcol)
            decay = jnp.exp(jnp.where(incl, gcol - grow, -jnp.inf))
            kb = kh * bcol
            g_last = gcol[last:last + 1, :]
            probs.append(dict(d=d, hh=hh, lo=lo, hi=hi, ii=ii, jj=jj, strict=strict, decay=decay, o_ref=refs[5],
                              rows=rows, order=(n_sub - 1 - c) if rev else c,
                              lhs=jnp.concatenate([kb, qh], axis=0).astype(BF16), kh=kh.astype(BF16),
                              rhs=jnp.concatenate([vh * bcol, kb * eg], axis=1).astype(BF16),
                              q_in=(qh * eg).astype(BF16), k_out=(kh * jnp.exp(g_last - gcol)).astype(BF16),
                              g_tot=jnp.exp(g_last)))
    kk = [_bdot_nt(p["lhs"], p["kh"]) for p in probs]
    for p, kkp in zip(probs, kk):
        p["a"] = jnp.where(p["strict"], kkp[:n] * p["decay"], 0.0)
        p["qk"] = (kkp[n:] * p["decay"]).astype(BF16)
    lg = 3
    for p in probs:
        blk = (p["ii"] >> lg) == (p["jj"] >> lg)
        p["a8"] = jnp.where(blk, p["a"], 0.0).astype(BF16)
        p["t"] = (p["ii"] == p["jj"]).astype(F32) - jnp.where(blk, p["a"], 0.0)
    x2 = [_bdot(p["a8"], p["a8"]).astype(BF16) for p in probs]
    x4 = [_bdot(x, x).astype(BF16) for x in x2]
    t1 = [p["t"] + _bdot(p["t"], x) for p, x in zip(probs, x2)]
    ts = [t + _bdot(t, x) for t, x in zip(t1, x4)]
    while (1 << lg) < n:
        s_blk = 1 << lg

        def pick(x, rev):
            return jnp.concatenate([x[b0 + (0 if rev else s_blk):b0 + (s_blk if rev else 2 * s_blk)]
                                    for b0 in range(0, n, 2 * s_blk)], axis=0)

        def merge(full, part, rev):
            rows = []
            for idx, b0 in enumerate(range(0, n, 2 * s_blk)):
                new = part[idx * s_blk:(idx + 1) * s_blk]
                old = (jnp.zeros((s_blk, n), F32) if full is None else
                       full[b0 + (s_blk if rev else 0):b0 + (2 * s_blk if rev else s_blk)])
                rows += [new, old] if rev else [old, new]
            return jnp.concatenate(rows, axis=0)

        bs = []
        for p in probs:
            rev = p["d"] == 1
            r = lax.broadcasted_iota(jnp.int32, (n // 2, n), 0)
            pi = ((r >> lg) << (lg + 1)) + (r & (s_blk - 1)) + (0 if rev else s_blk)
            pj = lax.broadcasted_iota(jnp.int32, (n // 2, n), 1)
            off = ((pi >> (lg + 1)) == (pj >> (lg + 1))) & ((pi >> lg) != (pj >> lg))
            bs.append(jnp.where(off, pick(p["a"], rev), 0.0).astype(BF16))
        tb = [t.astype(BF16) for t in ts]
        bt = [_bdot(b, t) for b, t in zip(bs, tb)]
        bt_full = [merge(None, x, p["d"] == 1).astype(BF16) for x, p in zip(bt, probs)]
        upd = [_bdot(pick(t, p["d"] == 1), x) for t, x, p in zip(ts, bt_full, probs)]
        ts = [merge(t, pick(t, p["d"] == 1) - u, p["d"] == 1) for t, u, p in zip(ts, upd, probs)]
        lg += 1
    sol = [_bdot(t, p["rhs"]) for t, p in zip(ts, probs)]
    state = {(d, hh): state_ref[d, hh] for d in range(N_DIR) for hh in range(GDN_HEADS)}
    for order in range(n_sub):
        cur = [(p, s) for p, s in zip(probs, sol) if p["order"] == order]
        stb = [state[p["d"], p["hh"]].astype(BF16) for p, _ in cur]
        ws = [jnp.dot(s[:, GDN_D:].astype(BF16), sb, preferred_element_type=F32) for (_, s), sb in zip(cur, stb)]
        v_new = [(s[:, :GDN_D] - w).astype(BF16) for (_, s), w in zip(cur, ws)]
        for (p, _), sb, vn in zip(cur, stb, v_new):
            p["o_ref"][p["rows"], p["lo"]:p["hi"]] = jnp.dot(
                jnp.concatenate([p["q_in"], p["qk"]], axis=1), jnp.concatenate([sb, vn], axis=0),
                preferred_element_type=F32)
            state[p["d"], p["hh"]] = state[p["d"], p["hh"]] * p["g_tot"] + lax.dot_general(
                p["k_out"], vn, (((0,), (0,)), ((), ())), preferred_element_type=F32)
    for (d, hh), sta in state.items():
        state_ref[d, hh] = sta


GDN_CHUNKS_PER_STEP = 2


def _gdn_scan(cg, sg, sgt, prow, pcol, *, n_batch, ncc, nlc, col_q):
    assert ncc % GDN_CHUNKS_PER_STEP == 0 and nlc % GDN_CHUNKS_PER_STEP == 0
    n = SCAN_CHUNK * GDN_CHUNKS_PER_STEP
    ncc, nlc = ncc // GDN_CHUNKS_PER_STEP, nlc // GDN_CHUNKS_PER_STEP
    dq = GDN_HEADS * GDN_D
    cq = col_q // dq

    def tok(rev, col):
        return pl.BlockSpec((n, dq), lambda b, s: (_scan_block_index(b, s, rev, n_batch, ncc, nlc), col))

    def small(rev):
        return pl.BlockSpec((n, LANE), lambda b, s: (_scan_block_index(b, s, rev, n_batch, ncc, nlc), 0))

    def small_t(rev):
        return pl.BlockSpec((LANE, n), lambda b, s: (0, _scan_block_index(b, s, rev, n_batch, ncc, nlc)))

    def out(rev):
        return pl.BlockSpec((n, dq), lambda b, s: (_scan_out_index(b, s, rev, ncc, nlc), 0))

    in_specs = []
    for rev in (False, True):
        in_specs += [tok(rev, cq), tok(rev, cq + 1), tok(rev, cq + 2), small(rev), small_t(rev)]
    in_specs += [_const_spec(prow.shape), _const_spec(pcol.shape)]
    t_lat = n_batch * nlc * n
    return pl.pallas_call(
        _gdn_kernel,
        out_shape=(jax.ShapeDtypeStruct((t_lat, dq), F32),) * 2,
        grid=(n_batch, ncc + nlc),
        in_specs=in_specs,
        out_specs=(out(False), out(True)),
        scratch_shapes=[pltpu.VMEM((N_DIR, GDN_HEADS, GDN_D, GDN_D), F32)],
        compiler_params=pltpu.CompilerParams(vmem_limit_bytes=VMEM_LIMIT,
                                             dimension_semantics=("arbitrary", "arbitrary")),
        name="gdn_scan",
    )(cg, cg, cg, sg, sgt, cg, cg, cg, sg, sgt, prow, pcol)


def _ssd_steps(xs_ref, bm_ref, cm_ref, sg_ref, sgt_ref, prow, pcol, ex_ref, state_ref, y_ref, d, rev, rows, order):
    n = SCAN_CHUNK
    bm, cm, sg, sgt = bm_ref[rows, :], cm_ref[rows, :], sg_ref[rows, :], sgt_ref[:, rows]
    n_heads = xs_ref.shape[1] // MB_HEADDIM
    hpg = n_heads // MB_GROUPS
    gw = hpg * MB_HEADDIM
    pw = 2 * MB_HEADDIM
    ii, jj, incl, _ = _tri_masks(n, rev)
    f0 = 2 * N_DIR * GDN_HEADS + d * n_heads
    heads = lambda a: a[:, f0:f0 + n_heads]
    dt_col = _softplus(sg + prow[3:4, :])
    dt_row = _softplus(sgt[f0:f0 + n_heads, :] + pcol[f0:f0 + n_heads, 3:4])
    acs_col = _expand(dt_col * -jnp.exp(prow[2:3, :]), incl.astype(BF16), 3, e_left=True)
    acs_row = _expand(dt_row * -jnp.exp(pcol[f0:f0 + n_heads, 2:3]),
                      (ii >= jj if rev else ii <= jj).astype(BF16), 3)
    last = 0 if rev else n - 1
    tot = acs_col[last:last + 1, :]
    w_in = heads(dt_col * jnp.exp(tot - acs_col))
    w_out = heads(jnp.exp(acs_col))
    w_tot = heads(jnp.broadcast_to(jnp.exp(tot), (8, LANE)))
    yield
    lane = lax.broadcasted_iota(jnp.int32, (n, pw), 1)
    left = lane < MB_HEADDIM
    grp = []
    for g in range(MB_GROUPS):
        bg = bm[:, g * MB_STATE:(g + 1) * MB_STATE]
        cgm = cm[:, g * MB_STATE:(g + 1) * MB_STATE].astype(BF16)
        grp.append((cgm, _bdot_nt(cgm, bg), bg.T.astype(BF16)))
    pairs = [(g, pr) for g in range(MB_GROUPS) for pr in range(hpg // 2)]
    lo_of = lambda g, pr: (g * hpg + 2 * pr) * MB_HEADDIM
    spread = [ex_ref[:, lo_of(g, pr):lo_of(g, pr) + pw] for g, pr in pairs]
    e_in = [_expand(w_in, sp, 1) for sp in spread]
    e_out = [_expand(w_out, sp, 1) for sp in spread]
    e_tot = [_expand(w_tot, sp, 3)[0:1, :] for sp in spread]
    yield
    xps = [xs_ref[rows, lo_of(g, pr):lo_of(g, pr) + pw] for g, pr in pairs]
    for _ in range(2 * order):
        yield
    sts = [state_ref[d, g, :, lo_of(g, pr) - g * gw:lo_of(g, pr) - g * gw + pw] for g, pr in pairs]
    y_off = [jnp.dot(grp[g][0], st.astype(BF16), preferred_element_type=F32) * eo
             for (g, pr), st, eo in zip(pairs, sts, e_out)]
    yield
    upd = [jnp.dot(grp[g][2], (xp * ei).astype(BF16), preferred_element_type=F32)
           for (g, pr), xp, ei in zip(pairs, xps, e_in)]
    for (g, pr), st, et, u in zip(pairs, sts, e_tot, upd):
        sl = lo_of(g, pr) - g * gw
        state_ref[d, g, :, sl:sl + pw] = st * et + u
    yield
    m2s = []
    for g, pr in pairs:
        ms = []
        for hd in (g * hpg + 2 * pr, g * hpg + 2 * pr + 1):
            ccol = acs_col[:, f0 + hd:f0 + hd + 1]
            crow = acs_row[hd:hd + 1, :]
            seg = jnp.exp(jnp.where(incl, ccol - crow, -jnp.inf))
            ms.append(grp[g][1] * seg * dt_row[hd:hd + 1, :])
        m2s.append(jnp.concatenate(ms, axis=1).astype(BF16))
    xbds = [jnp.concatenate([jnp.where(left, xp, 0.0), jnp.where(left, 0.0, xp)], axis=0).astype(BF16)
            for xp in xps]
    yield
    for (g, pr), m2, xbd, yo in zip(pairs, m2s, xbds, y_off):
        lo = lo_of(g, pr)
        y_ref[rows, lo:lo + pw] = jnp.dot(m2, xbd, preferred_element_type=F32) + yo


def _ssd_kernel(xf, bf, cf, sgf, sgtf, xb, bb, cb, sgb, sgtb, prow_ref, pcol_ref, ex_ref, yf_ref, yb_ref, state_ref):
    @pl.when(pl.program_id(1) == 0)
    def _():
        state_ref[...] = jnp.zeros_like(state_ref)

    prow, pcol = prow_ref[...], pcol_ref[...]
    n_sub = xf.shape[0] // SCAN_CHUNK
    live = []
    for c in range(n_sub):
        rows = slice(c * SCAN_CHUNK, (c + 1) * SCAN_CHUNK)
        live.append(_ssd_steps(xf, bf, cf, sgf, sgtf, prow, pcol, ex_ref, state_ref, yf_ref, 0, False, rows, c))
        live.append(_ssd_steps(xb, bb, cb, sgb, sgtb, prow, pcol, ex_ref, state_ref, yb_ref, 1, True, rows,
                               n_sub - 1 - c))
    while live:
        for steps in list(live):
            if next(steps, "done") == "done":
                live.remove(steps)


SSD_CHUNKS_PER_STEP = 2


def _ssd_scan(cg, sg, sgt, prow, pcol, ex, *, n_batch, ncc, nlc, d_inner, col_b):
    assert ncc % SSD_CHUNKS_PER_STEP == 0 and nlc % SSD_CHUNKS_PER_STEP == 0
    n = SCAN_CHUNK * SSD_CHUNKS_PER_STEP
    ncc, nlc = ncc // SSD_CHUNKS_PER_STEP, nlc // SSD_CHUNKS_PER_STEP
    bw = MB_GROUPS * MB_STATE
    cb_ = col_b // bw

    def blk(rev, width, col):
        return pl.BlockSpec((n, width), lambda b, s: (_scan_block_index(b, s, rev, n_batch, ncc, nlc), col))

    def small_t(rev):
        return pl.BlockSpec((LANE, n), lambda b, s: (0, _scan_block_index(b, s, rev, n_batch, ncc, nlc)))

    def out(rev):
        return pl.BlockSpec((n, d_inner), lambda b, s: (_scan_out_index(b, s, rev, ncc, nlc), 0))

    in_specs = []
    for rev in (False, True):
        in_specs += [blk(rev, d_inner, 0), blk(rev, bw, cb_), blk(rev, bw, cb_ + 1), blk(rev, LANE, 0), small_t(rev)]
    in_specs += [_const_spec(prow.shape), _const_spec(pcol.shape), _const_spec(ex.shape)]
    t_lat = n_batch * nlc * n
    return pl.pallas_call(
        _ssd_kernel,
        out_shape=(jax.ShapeDtypeStruct((t_lat, d_inner), F32),) * 2,
        grid=(n_batch, ncc + nlc),
        in_specs=in_specs,
        out_specs=(out(False), out(True)),
        scratch_shapes=[pltpu.VMEM((N_DIR, MB_GROUPS, MB_STATE, d_inner // MB_GROUPS), F32)],
        compiler_params=pltpu.CompilerParams(vmem_limit_bytes=VMEM_LIMIT,
                                             dimension_semantics=("arbitrary", "arbitrary")),
        name="ssd_scan",
    )(cg, cg, cg, sg, sgt, cg, cg, cg, sg, sgt, prow, pcol, ex)


def _mixout_kernel(x_ref, mod_ref, g_ref, wp_ref, of_ref, ob_ref, yf_ref, yb_ref, xs_ref,
                   rows_ref, wbg_ref, wbm_ref, wo_ref, o_ref, *, d_model, d_v, d_inner):
    x = x_ref[...]
    hb = _modulated_norm(x, g_ref[...], mod_ref[0, 3:4, :], mod_ref[0, 4:5, :]).astype(BF16)
    gate = mod_ref[0, 5:6, :]
    z = jnp.dot(hb, wp_ref[...], preferred_element_type=F32)
    za = z[:, :d_v]
    zb = z[:, d_v:d_v + d_inner]
    ga = z[:, d_v + d_inner:d_v + d_inner + d_model]
    gb = z[:, d_v + d_inner + d_model:]

    o = of_ref[...] + ob_ref[...]
    parts = []
    for hh in range(d_v // GDN_D):
        oh = o[:, hh * GDN_D:(hh + 1) * GDN_D]
        parts.append(oh * lax.rsqrt(jnp.mean(oh * oh, axis=-1, keepdims=True) + EPS))
    ya = jnp.concatenate(parts, axis=1) * rows_ref[0:1, :d_v] * _silu(za)

    yb = (yf_ref[...] + yb_ref[...] + rows_ref[1:2, :] * xs_ref[...]) * _silu(zb)
    gw = d_inner // MB_GROUPS
    parts = []
    for g in range(MB_GROUPS):
        yg = yb[:, g * gw:(g + 1) * gw]
        parts.append(yg * lax.rsqrt(jnp.mean(yg * yg, axis=-1, keepdims=True) + EPS))
    yb = jnp.concatenate(parts, axis=1) * rows_ref[2:3, :]

    merged = (jax.nn.sigmoid(ga) * _bdot(ya, wbg_ref[...]) + jax.nn.sigmoid(gb) * _bdot(yb, wbm_ref[...]))
    o_ref[...] = x + gate * _bdot(merged, wo_ref[...])


def _mixout(x1, mod3, g, w_plain, o_f, o_b, y_f, y_b, cg, rows, wbg, wbm, wo, *, tm, n_ctx_tiles, seq):
    t_lat, d_v = o_f.shape
    d_inner = y_f.shape[1]
    d = x1.shape[1]
    return pl.pallas_call(
        functools.partial(_mixout_kernel, d_model=d, d_v=d_v, d_inner=d_inner),
        out_shape=jax.ShapeDtypeStruct((t_lat, d), F32),
        grid=(t_lat // tm,),
        in_specs=[pl.BlockSpec((tm, d), lambda i: (i + n_ctx_tiles, 0)),
                  pl.BlockSpec((1,) + mod3.shape[1:], lambda i: (1 + (i * tm) // seq, 0, 0)),
                  _const_spec(g.shape), _const_spec(w_plain.shape),
                  pl.BlockSpec((tm, d_v), lambda i: (i, 0)), pl.BlockSpec((tm, d_v), lambda i: (i, 0)),
                  pl.BlockSpec((tm, d_inner), lambda i: (i, 0)), pl.BlockSpec((tm, d_inner), lambda i: (i, 0)),
                  pl.BlockSpec((tm, d_inner), lambda i: (i + n_ctx_tiles, 0)),
                  _const_spec(rows.shape), _const_spec(wbg.shape), _const_spec(wbm.shape), _const_spec(wo.shape)],
        out_specs=pl.BlockSpec((tm, d), lambda i: (i, 0)),
        compiler_params=pltpu.CompilerParams(vmem_limit_bytes=VMEM_LIMIT),
        name="mixout",
    )(x1, mod3, g, w_plain, o_f, o_b, y_f, y_b, cg, rows, wbg, wbm, wo)


def kernel(x, c, ctx, c_ctx, w_ada, b_ada, norm_g, ffn_w_gu, ffn_w_down, w_in, gdn_conv_w, gdn_A_log, gdn_dt_bias, gdn_norm_g, mb_conv_w, mb_conv_b, mb_A_log, mb_dt_bias, mb_D, mb_norm_g, w_branch_gdn, w_branch_mb, w_out, final_g):
    n_batch, seq, d = x.shape
    ctx_len = ctx.shape[1]
    assert w_ada.shape[0] == 1, "single-layer operation"
    d_qk = GDN_HEADS * GDN_D
    d_v = GDN_HEADS * GDN_D
    d_inner = mb_norm_g.shape[1]
    n_mb_heads = d_inner // MB_HEADDIM
    d_bc = MB_GROUPS * MB_STATE
    tm = 256
    t_ctx, t_lat = n_batch * ctx_len, n_batch * seq
    assert t_ctx % tm == 0 and seq % tm == 0
    assert ctx_len % SCAN_CHUNK == 0 and seq % SCAN_CHUNK == 0
    assert N_DIR * (2 * GDN_HEADS + n_mb_heads) <= LANE
    n_ctx_tiles = t_ctx // tm

    rows = 16
    cc = jnp.concatenate([c_ctx[None, :], c, jnp.zeros((rows - 1 - n_batch, d), F32)], axis=0)
    mod3 = _adaln(cc, w_ada[0], b_ada[0][None, :]).reshape(rows, 9, d)

    tm_ffn = 2 * tm
    assert t_ctx % tm_ffn == 0 and seq % tm_ffn == 0 and tm_ffn % ctx_len == 0 and tm_ffn % GRID_W == 0

    def row_ffn(i):
        return jnp.where(i < t_ctx // tm_ffn, 0, 1 + ((i - t_ctx // tm_ffn) * tm_ffn) // seq)

    ffn_gu, ffn_down = ffn_w_gu[0].astype(BF16), ffn_w_down[0].astype(BF16)
    x1 = _ffn(ctx.reshape(t_ctx, d), x.reshape(t_lat, d), mod3, norm_g[0, 0][None, :], ffn_gu, ffn_down,
              final_g[None, :], k=0, row_of_tile=row_ffn, tm=tm_ffn, final=False, name="ffn1")

    sizes = (2 * d_qk + d_v, d_v, N_DIR * GDN_HEADS, N_DIR * GDN_HEADS, d_inner, d_inner + 2 * d_bc,
             N_DIR * n_mb_heads)
    o_qkv, o_za, o_a, o_beta, o_zb, o_xbc, o_dt, o_gates = [sum(sizes[:j]) for j in range(len(sizes) + 1)]

    def regroup(a, off_qkv, off_xbc):
        return jnp.concatenate([a[..., off_xbc:off_xbc + d_inner], a[..., off_qkv:off_qkv + 2 * d_qk + d_v],
                                a[..., off_xbc + d_inner:off_xbc + d_inner + 2 * d_bc]], axis=-1)

    n_small = N_DIR * (2 * GDN_HEADS + n_mb_heads)
    w_cg, w_small, w_plain = _regroup_w_in(
        w_in[0].T,
        cg_cols=((o_xbc, o_xbc + d_inner), (o_qkv, o_za), (o_xbc + d_inner, o_dt)),
        small_cols=((o_a, o_zb), (o_dt, o_gates)),
        plain_cols=((o_za, o_a), (o_zb, o_xbc), (o_gates, o_gates + 2 * d)))
    conv_all = jnp.concatenate([gdn_conv_w[0], mb_conv_w[0]], axis=1)
    bias_all = jnp.concatenate([jnp.zeros((1, 2 * d_qk + d_v), F32), mb_conv_b[0][None, :]], axis=1)
    conv_p = regroup(jnp.concatenate([conv_all, bias_all, jnp.zeros((2, conv_all.shape[1]), F32)], axis=0),
                     0, 2 * d_qk + d_v)
    col_q = d_inner
    col_b = d_inner + 2 * d_qk + d_v
    blocks = ((0, d_inner // 2, "x"), (d_inner // 2, d_inner, "x"), (col_q, col_q + d_qk, "q"),
              (col_q + d_qk, col_q + 2 * d_qk, "k"), (col_q + 2 * d_qk, col_b, "v"), (col_b, col_b + 2 * d_bc, "x"))
    cg, sg, sgt = _inproj(x1, mod3, norm_g[0, 1][None, :], w_cg, w_small, conv_p, tm=tm_ffn,
                          n_ctx_tiles=t_ctx // tm_ffn,
                          ctx_len=ctx_len, seq=seq, blocks=blocks)

    pad = jnp.zeros((LANE - n_small,), F32)
    zeros_g = jnp.zeros((N_DIR * GDN_HEADS,), F32)
    zeros_m = jnp.zeros((N_DIR * n_mb_heads,), F32)
    prow = jnp.stack([jnp.concatenate([gdn_A_log[0].reshape(-1), zeros_g, zeros_m, pad]),
                      jnp.concatenate([gdn_dt_bias[0].reshape(-1), zeros_g, zeros_m, pad]),
                      jnp.concatenate([zeros_g, zeros_g, mb_A_log[0].reshape(-1), pad]),
                      jnp.concatenate([zeros_g, zeros_g, mb_dt_bias[0].reshape(-1), pad])]
                     + [jnp.zeros((LANE,), F32)] * 4)
    pcol = prow.T
    ncc, nlc = ctx_len // SCAN_CHUNK, seq // SCAN_CHUNK
    o_f, o_b = _gdn_scan(cg, sg, sgt, prow, pcol, n_batch=n_batch, ncc=ncc, nlc=nlc, col_q=col_q)

    ex = (jnp.arange(n_mb_heads)[:, None] == (jnp.arange(d_inner) // MB_HEADDIM)[None, :]).astype(BF16)
    y_f, y_b = _ssd_scan(cg, sg, sgt, prow, pcol, ex, n_batch=n_batch, ncc=ncc, nlc=nlc, d_inner=d_inner, col_b=col_b)

    rows3 = jnp.stack([jnp.concatenate([jnp.tile(gdn_norm_g[0], GDN_HEADS), jnp.zeros((d_inner - d_v,), F32)]),
                       jnp.repeat(mb_D[0], MB_HEADDIM), mb_norm_g[0]] + [jnp.zeros((d_inner,), F32)] * 5)
    x2 = _mixout(x1, mod3, norm_g[0, 1][None, :], w_plain, o_f, o_b, y_f, y_b, cg, rows3,
                 w_branch_gdn[0].astype(BF16), w_branch_mb[0].astype(BF16), w_out[0].astype(BF16),
                 tm=tm, n_ctx_tiles=n_ctx_tiles, seq=seq)

    out = _ffn(None, x2, mod3, norm_g[0, 2][None, :], ffn_gu, ffn_down, final_g[None, :], k=2, row_of_tile=lambda i: 1 + (i * tm_ffn) // seq, tm=tm_ffn, final=True, name="ffn2")
    return out.reshape(n_batch, seq, d)
```

```python
import functools

import jax
import jax.numpy as jnp
import numpy as np
from jax import lax
from jax.experimental import pallas as pl
from jax.experimental.pallas import tpu as pltpu

F32 = jnp.float32
BF16 = jnp.bfloat16

EPS = 1e-6
FFN_RES = 0.5
GRID_W = 64
CONV_K = 5
GDN_HEADS = 8
GDN_D = 128
MB_HEADDIM = 64
MB_GROUPS = 2
MB_STATE = 128
N_DIR = 2

LANE = 128
SCAN_CHUNK = 128
VMEM_LIMIT = 56 * 1024 * 1024


def _const_spec(shape):
    nd = len(shape)
    return pl.BlockSpec(shape, lambda *_: (0,) * nd, pipeline_mode=pl.Buffered(1))


def _silu(x):
    return x * jax.nn.sigmoid(x)


def _softplus(x):
    return jnp.maximum(x, 0.0) + jnp.log(1.0 + jnp.exp(-jnp.abs(x)))


def _bdot(a, b):
    return jnp.dot(a.astype(BF16), b.astype(BF16), preferred_element_type=F32)


def _bdot_nt(a, b):
    return lax.dot_general(a.astype(BF16), b.astype(BF16), (((1,), (1,)), ((), ())),
                           preferred_element_type=F32)


def _modulated_norm(x, g, shift, scale):
    ms = jnp.mean(x * x, axis=-1, keepdims=True)
    return x * lax.rsqrt(ms + EPS) * g * (1.0 + scale) + shift


def _adaln_kernel(c_ref, w_ref, b_ref, o_ref):
    s = _silu(c_ref[...])
    o_ref[...] = _split_dot(s, w_ref) + b_ref[...]


def _adaln(cc, w, b, tn=1152):
    rows, d = cc.shape
    n = w.shape[1]
    return pl.pallas_call(
        _adaln_kernel,
        out_shape=jax.ShapeDtypeStruct((rows, n), F32),
        grid=(n // tn,),
        in_specs=[pl.BlockSpec((rows, d), lambda j: (0, 0)),
                  pl.BlockSpec((d, tn), lambda j: (0, j)),
                  pl.BlockSpec((1, tn), lambda j: (0, j))],
        out_specs=pl.BlockSpec((rows, tn), lambda j: (0, j)),
        compiler_params=pltpu.CompilerParams(vmem_limit_bytes=VMEM_LIMIT),
        name="adaln",
    )(cc, w, b)


def _ffn_kernel(c_ref, x_ref, mod_ref, g_ref, wgu_ref, wd_ref, fg_ref, o_ref, *, k, d_ff, final, n_ctx_tiles):
    if n_ctx_tiles:
        x = jnp.where(pl.program_id(0) < n_ctx_tiles, c_ref[...], x_ref[...])
    else:
        x = x_ref[...]
    shift = mod_ref[0, 3 * k:3 * k + 1, :]
    scale = mod_ref[0, 3 * k + 1:3 * k + 2, :]
    gate = mod_ref[0, 3 * k + 2:3 * k + 3, :]
    hb = _modulated_norm(x, g_ref[...], shift, scale).astype(BF16)
    gt = jnp.dot(hb, wgu_ref[0, :, :d_ff], preferred_element_type=F32)
    up = jnp.dot(hb, wgu_ref[0, :, d_ff:], preferred_element_type=F32)
    act = (_silu(gt) * up).astype(BF16)
    y = x + FFN_RES * gate * jnp.dot(act, wd_ref[0], preferred_element_type=F32)
    if final:
        ms = jnp.mean(y * y, axis=-1, keepdims=True)
        y = y * lax.rsqrt(ms + EPS) * fg_ref[...]
    o_ref[...] = y


def _ffn(xc, x, mod3, g, wgu, wd, fg, *, k, row_of_tile, tm, final, name):
    t_lat, d = x.shape
    d_ff = wd.shape[1]
    which = k // 2
    w_spec = lambda w: pl.BlockSpec((1,) + w.shape[1:], lambda i: (which, 0, 0), pipeline_mode=pl.Buffered(1))
    if xc is None:
        nct = 0
        xc, c_spec = x, pl.BlockSpec((8, d), lambda i: (0, 0))
    else:
        nct = xc.shape[0] // tm
        c_spec = pl.BlockSpec((tm, d), lambda i: (jnp.minimum(i, nct - 1), 0))
    t = nct * tm + t_lat
    return pl.pallas_call(
        functools.partial(_ffn_kernel, k=k, d_ff=d_ff, final=final, n_ctx_tiles=nct),
        out_shape=jax.ShapeDtypeStruct((t, d), F32),
        grid=(t // tm,),
        in_specs=[c_spec, pl.BlockSpec((tm, d), lambda i: (jnp.maximum(i - nct, 0), 0)),
                  pl.BlockSpec((1,) + mod3.shape[1:], lambda i: (row_of_tile(i), 0, 0)),
                  _const_spec(g.shape), w_spec(wgu), w_spec(wd), _const_spec(fg.shape)],
        out_specs=pl.BlockSpec((tm, d), lambda i: (i, 0)),
        compiler_params=pltpu.CompilerParams(vmem_limit_bytes=VMEM_LIMIT),
        name=name,
    )(xc, x, mod3, g, wgu, wd, fg)


def _regroup_kernel(offs_ref, wt_ref, *rest, n_cg_blocks, n_small_in):
    del offs_ref
    small_in = rest[:n_small_in]
    cg_ref, small_ref, plain_ref = rest[n_small_in:]
    j = pl.program_id(0)
    blk = wt_ref[...].T.astype(BF16)

    @pl.when(j < n_cg_blocks)
    def _():
        cg_ref[...] = blk

    @pl.when(j >= n_cg_blocks)
    def _():
        plain_ref[...] = blk

    @pl.when(j == 0)
    def _():
        rows = [r[...] for r in small_in]
        pad = LANE - sum(r.shape[0] for r in rows)
        small_ref[...] = jnp.concatenate(rows + [jnp.zeros((pad, rows[0].shape[1]), F32)], axis=0).T


def _regroup_w_in(wt, cg_cols, small_cols, plain_cols, bw=512):
    d = wt.shape[1]
    width = lambda cols: sum(b - a for a, b in cols)
    assert all((b - a) % bw == 0 for a, b in cg_cols + plain_cols)
    starts = lambda cols: [s for a, b in cols for s in range(a, b, bw)]
    all_starts = starts(cg_cols) + starts(plain_cols)
    unit = 8
    assert all(s % unit == 0 for s in all_starts)
    offs = jnp.asarray([s // unit for s in all_starts], jnp.int32)
    ncg = len(starts(cg_cols))
    in_specs = [pl.BlockSpec((pl.Element(bw), pl.Element(d)), lambda j, offs: (offs[j] * unit, 0))]
    in_specs += [pl.BlockSpec((pl.Element(b - a), pl.Element(d)), lambda j, offs, a=a: (a, 0))
                 for a, b in small_cols]
    out_specs = (pl.BlockSpec((d, bw), lambda j, offs: (0, jnp.minimum(j, ncg - 1))),
                 pl.BlockSpec((d, LANE), lambda j, offs: (0, 0)),
                 pl.BlockSpec((d, bw), lambda j, offs: (0, jnp.maximum(j - ncg, 0))))
    return pl.pallas_call(
        functools.partial(_regroup_kernel, n_cg_blocks=ncg, n_small_in=len(small_cols)),
        out_shape=(jax.ShapeDtypeStruct((d, width(cg_cols)), BF16), jax.ShapeDtypeStruct((d, LANE), F32),
                   jax.ShapeDtypeStruct((d, width(plain_cols)), BF16)),
        grid_spec=pltpu.PrefetchScalarGridSpec(num_scalar_prefetch=1, grid=(offs.shape[0],),
                                               in_specs=in_specs, out_specs=out_specs),
        compiler_params=pltpu.CompilerParams(vmem_limit_bytes=VMEM_LIMIT,
                                             dimension_semantics=("arbitrary",)),
        name="regroup_w_in",
    )(offs, wt, *([wt] * len(small_cols)))


def _conv_shift_matrix(tm, period):
    t = np.arange(tm)[:, None]
    s = np.arange(tm)[None, :]
    half = CONV_K // 2
    mats = [(s == t + dlt) & (t // period == (t + dlt) // period)
            for dlt in range(-half, half + 1) if dlt]
    return np.concatenate(mats, axis=1).astype(np.float32)


CONV_SPLIT = 4
CONV_HALO = 16


def _conv_shift_parts(tm, period):
    taps = CONV_K - 1
    r = tm // CONV_SPLIT
    rest = _conv_shift_matrix(tm, period).reshape(tm, taps, tm)
    blocks = []
    for s in range(CONV_SPLIT):
        rows = slice(s * r, (s + 1) * r)
        blocks.append(rest[rows, :, rows].reshape(r, taps * r).copy())
        rest[rows, :, rows] = 0
    corr = [np.zeros((2 * CONV_HALO, taps * 2 * CONV_HALO), np.float32)]
    for b in range(1, CONV_SPLIT):
        win = slice(b * r - CONV_HALO, b * r + CONV_HALO)
        corr.append(rest[win, :, win].reshape(2 * CONV_HALO, taps * 2 * CONV_HALO).copy())
        rest[win, :, win] = 0
    assert not rest.any()
    return np.stack(blocks), np.stack(corr[-max(CONV_SPLIT - 1, 1):])


def _split_dot(a, b_ref):
    b = b_ref[...]
    a1 = a.astype(BF16)
    a2 = (a - a1.astype(F32)).astype(BF16)
    b1 = b.astype(BF16)
    b2 = (b - b1.astype(F32)).astype(BF16)
    dot = functools.partial(jnp.dot, preferred_element_type=F32)
    return dot(a1, b1) + (dot(a1, b2) + dot(a2, b1))


def _inproj_kernel(x_ref, mod_ref, g_ref, w_ref, ws_ref, cp_ref, sb_ref, sc_ref, cg_ref, sg_ref, sgt_ref, *,
                   blocks):
    x = x_ref[...]
    tm = x.shape[0]
    r = tm // CONV_SPLIT
    h = _modulated_norm(x, g_ref[...], mod_ref[0, 3:4, :], mod_ref[0, 4:5, :])
    hb = h.astype(BF16)
    small = _split_dot(h, ws_ref)
    sg_ref[...] = small
    sgt_ref[...] = small.T
    half = CONV_K // 2
    for c0, c1, kind in blocks:
        u = jnp.dot(hb, w_ref[:, c0:c1], preferred_element_type=F32)
        taps = [(u * cp_ref[half + dlt:half + dlt + 1, c0:c1]).astype(BF16)
                for dlt in range(-half, half + 1) if dlt]
        conv = [jnp.dot(sb_ref[0, s], jnp.concatenate([tp[s * r:(s + 1) * r] for tp in taps], axis=0),
                        preferred_element_type=F32) for s in range(CONV_SPLIT)]
        pieces = []
        for s in range(CONV_SPLIT):
            if s > 0:
                pieces[-1] = pieces[-1] + cross[:CONV_HALO]
                pieces.append(conv[s][:CONV_HALO] + cross[CONV_HALO:])
                pieces.append(conv[s][CONV_HALO:r - CONV_HALO] if s < CONV_SPLIT - 1 else conv[s][CONV_HALO:])
            else:
                pieces.append(conv[s][:r - CONV_HALO] if CONV_SPLIT > 1 else conv[s])
            if s < CONV_SPLIT - 1:
                pieces.append(conv[s][r - CONV_HALO:])
                b = (s + 1) * r
                cross = jnp.dot(sc_ref[0, s], jnp.concatenate([tp[b - CONV_HALO:b + CONV_HALO] for tp in taps],
                                                              axis=0), preferred_element_type=F32)
        acc = u * cp_ref[half:half + 1, c0:c1] + jnp.concatenate(pieces, axis=0)
        if kind == "x":
            acc = acc + cp_ref[CONV_K:CONV_K + 1, c0:c1]
        y = _silu(acc)
        if kind in ("q", "k"):
            parts = []
            for hh in range((c1 - c0) // GDN_D):
                yh = y[:, hh * GDN_D:(hh + 1) * GDN_D]
                inv = lax.rsqrt(jnp.sum(yh * yh, axis=-1, keepdims=True) + EPS)
                if kind == "q":
                    inv = inv * (GDN_D ** -0.5)
                parts.append(yh * inv)
            y = jnp.concatenate(parts, axis=1)
        cg_ref[:, c0:c1] = y


def _inproj(x1, mod3, g, w_cg, w_small, conv_p, *, tm, n_ctx_tiles, ctx_len, seq, blocks):
    t, d = x1.shape
    ncg = w_cg.shape[1]

    def row_of_tile(i):
        return jnp.where(i < n_ctx_tiles, 0, 1 + ((i - n_ctx_tiles) * tm) // seq)

    parts = [_conv_shift_parts(tm, ctx_len), _conv_shift_parts(tm, GRID_W)]
    sblk = jnp.asarray(np.stack([p[0] for p in parts]), dtype=BF16)
    scor = jnp.asarray(np.stack([p[1] for p in parts]), dtype=BF16)
    variant = lambda i: (jnp.where(i < n_ctx_tiles, 0, 1), 0, 0, 0)
    return pl.pallas_call(
        functools.partial(_inproj_kernel, blocks=blocks),
        out_shape=(jax.ShapeDtypeStruct((t, ncg), F32), jax.ShapeDtypeStruct((t, LANE), F32),
                   jax.ShapeDtypeStruct((LANE, t), F32)),
        grid=(t // tm,),
        in_specs=[pl.BlockSpec((tm, d), lambda i: (i, 0)),
                  pl.BlockSpec((1,) + mod3.shape[1:], lambda i: (row_of_tile(i), 0, 0)),
                  _const_spec(g.shape), _const_spec(w_cg.shape), _const_spec(w_small.shape),
                  _const_spec(conv_p.shape),
                  pl.BlockSpec((1,) + sblk.shape[1:], variant), pl.BlockSpec((1,) + scor.shape[1:], variant)],
        out_specs=(pl.BlockSpec((tm, ncg), lambda i: (i, 0)), pl.BlockSpec((tm, LANE), lambda i: (i, 0)),
                   pl.BlockSpec((LANE, tm), lambda i: (0, i))),
        compiler_params=pltpu.CompilerParams(vmem_limit_bytes=VMEM_LIMIT),
        name="inproj",
    )(x1, mod3, g, w_cg, w_small, conv_p, sblk, scor)


def _expand(x, e, pieces, e_left=False):
    acc = None
    r = x
    for _ in range(pieces):
        xp = r.astype(BF16)
        r = r - xp.astype(F32)
        term = jnp.dot(e, xp, preferred_element_type=F32) if e_left else jnp.dot(xp, e, preferred_element_type=F32)
        acc = term if acc is None else acc + term
    return acc


def _tri_masks(n, rev):
    ii = lax.broadcasted_iota(jnp.int32, (n, n), 0)
    jj = lax.broadcasted_iota(jnp.int32, (n, n), 1)
    if rev:
        return ii, jj, ii <= jj, ii < jj
    return ii, jj, ii >= jj, ii > jj


def _scan_block_index(b, s, rev, n_batch, ncc, nlc):
    lc = jnp.maximum(s - ncc, 0)
    if rev:
        ctx_blk = b * ncc + (ncc - 1 - jnp.minimum(s, ncc - 1))
        lat_blk = n_batch * ncc + b * nlc + (nlc - 1 - lc)
    else:
        ctx_blk = b * ncc + jnp.minimum(s, ncc - 1)
        lat_blk = n_batch * ncc + b * nlc + lc
    return jnp.where(s < ncc, ctx_blk, lat_blk)


def _scan_out_index(b, s, rev, ncc, nlc):
    lc = jnp.maximum(s - ncc, 0)
    return b * nlc + ((nlc - 1 - lc) if rev else lc)


def _gdn_kernel(qf, kf, vf, sgf, sgtf, qb, kb_, vb, sgb, sgtb, prow_ref, pcol_ref, of_ref, ob_ref, state_ref):
    @pl.when(pl.program_id(1) == 0)
    def _():
        state_ref[...] = jnp.zeros_like(state_ref)

    n = SCAN_CHUNK
    prow, pcol = prow_ref[...], pcol_ref[...]
    probs = []
    n_sub = qf.shape[0] // n
    for d, rev, refs, c in [(0, False, (qf, kf, vf, sgf, sgtf, of_ref), c) for c in range(n_sub)] + \
                           [(1, True, (qb, kb_, vb, sgb, sgtb, ob_ref), c) for c in range(n_sub)]:
        rows = slice(c * n, (c + 1) * n)
        q, k, v, sg = [r[rows, :] for r in refs[:4]]
        n_g = N_DIR * GDN_HEADS
        sgt = refs[4][0:n_g, rows]
        g_col = -jnp.exp(prow[0:1, :]) * _softplus(sg + prow[1:2, :])
        g_row = -jnp.exp(pcol[0:n_g, 0:1]) * _softplus(sgt + pcol[0:n_g, 1:2])
        beta_col = jax.nn.sigmoid(sg)
        ii, jj, incl, strict = _tri_masks(n, rev)
        gc_col = _expand(g_col, incl.astype(BF16), 3, e_left=True)
        gc_row = _expand(g_row, (ii >= jj if rev else ii <= jj).astype(BF16), 3)
        last = 0 if rev else n - 1
        for hh in range(GDN_HEADS):
            ci = d * GDN_HEADS + hh
            lo, hi = hh * GDN_D, (hh + 1) * GDN_D
            qh, kh, vh = q[:, lo:hi], k[:, lo:hi], v[:, lo:hi]
            gcol = gc_col[:, ci:ci + 1]
            grow = gc_row[ci:ci + 1, :]
            bcol = beta_col[:, N_DIR * GDN_HEADS + ci:N_DIR * GDN_HEADS + ci + 1]
            eg = jnp.exp(gcol)
            decay = jnp.exp(jnp.where(incl, gcol - grow, -jnp.inf))
            kb = kh * bcol
            g_last = gcol[last:last + 1, :]
            probs.append(dict(d=d, hh=hh, lo=lo, hi=hi, ii=ii, jj=jj, strict=strict, decay=decay, o_ref=refs[5],
                              rows=rows, order=(n_sub - 1 - c) if rev else c,
                              lhs=jnp.concatenate([kb, qh], axis=0).astype(BF16), kh=kh.astype(BF16),
                              rhs=jnp.concatenate([vh * bcol, kb * eg], axis=1).astype(BF16),
                              q_in=(qh * eg).astype(BF16), k_out=(kh * jnp.exp(g_last - gcol)).astype(BF16),
                              g_tot=jnp.exp(g_last)))
    kk = [_bdot_nt(p["lhs"], p["kh"]) for p in probs]
    for p, kkp in zip(probs, kk):
        p["a"] = jnp.where(p["strict"], kkp[:n] * p["decay"], 0.0)
        p["qk"] = (kkp[n:] * p["decay"]).astype(BF16)
    lg = 3
    for p in probs:
        blk = (p["ii"] >> lg) == (p["jj"] >> lg)
        p["a8"] = jnp.where(blk, p["a"], 0.0).astype(BF16)
        p["t"] = (p["ii"] == p["jj"]).astype(F32) - jnp.where(blk, p["a"], 0.0)
    x2 = [_bdot(p["a8"], p["a8"]).astype(BF16) for p in probs]
    x4 = [_bdot(x, x).astype(BF16) for x in x2]
    t1 = [p["t"] + _bdot(p["t"], x) for p, x in zip(probs, x2)]
    ts = [t + _bdot(t, x) for t, x in zip(t1, x4)]
    while (1 << lg) < n:
        s_blk = 1 << lg

        def pick(x, rev):
            return jnp.concatenate([x[b0 + (0 if rev else s_blk):b0 + (s_blk if rev else 2 * s_blk)]
                                    for b0 in range(0, n, 2 * s_blk)], axis=0)

        def merge(full, part, rev):
            rows = []
            for idx, b0 in enumerate(range(0, n, 2 * s_blk)):
                new = part[idx * s_blk:(idx + 1) * s_blk]
                old = (jnp.zeros((s_blk, n), F32) if full is None else
                       full[b0 + (s_blk if rev else 0):b0 + (2 * s_blk if rev else s_blk)])
                rows += [new, old] if rev else [old, new]
            return jnp.concatenate(rows, axis=0)

        bs = []
        for p in probs:
            rev = p["d"] == 1
            r = lax.broadcasted_iota(jnp.int32, (n // 2, n), 0)
            pi = ((r >> lg) << (lg + 1)) + (r & (s_blk - 1)) + (0 if rev else s_blk)
            pj = lax.broadcasted_iota(jnp.int32, (n // 2, n), 1)
            off = ((pi >> (lg + 1)) == (pj >> (lg + 1))) & ((pi >> lg) != (pj >> lg))
            bs.append(jnp.where(off, pick(p["a"], rev), 0.0).astype(BF16))
        tb = [t.astype(BF16) for t in ts]
        bt = [_bdot(b, t) for b, t in zip(bs, tb)]
        bt_full = [merge(None, x, p["d"] == 1).astype(BF16) for x, p in zip(bt, probs)]
        upd = [_bdot(pick(t, p["d"] == 1), x) for t, x, p in zip(ts, bt_full, probs)]
        ts = [merge(t, pick(t, p["d"] == 1) - u, p["d"] == 1) for t, u, p in zip(ts, upd, probs)]
        lg += 1
    sol = [_bdot(t, p["rhs"]) for t, p in zip(ts, probs)]
    state = {(d, hh): state_ref[d, hh] for d in range(N_DIR) for hh in range(GDN_HEADS)}
    for order in range(n_sub):
        cur = [(p, s) for p, s in zip(probs, sol) if p["order"] == order]
        stb = [state[p["d"], p["hh"]].astype(BF16) for p, _ in cur]
        ws = [jnp.dot(s[:, GDN_D:].astype(BF16), sb, preferred_element_type=F32) for (_, s), sb in zip(cur, stb)]
        v_new = [(s[:, :GDN_D] - w).astype(BF16) for (_, s), w in zip(cur, ws)]
        for (p, _), sb, vn in zip(cur, stb, v_new):
            p["o_ref"][p["rows"], p["lo"]:p["hi"]] = jnp.dot(
                jnp.concatenate([p["q_in"], p["qk"]], axis=1), jnp.concatenate([sb, vn], axis=0),
                preferred_element_type=F32)
            state[p["d"], p["hh"]] = state[p["d"], p["hh"]] * p["g_tot"] + lax.dot_general(
                p["k_out"], vn, (((0,), (0,)), ((), ())), preferred_element_type=F32)
    for (d, hh), sta in state.items():
        state_ref[d, hh] = sta


GDN_CHUNKS_PER_STEP = 2


def _gdn_scan(cg, sg, sgt, prow, pcol, *, n_batch, ncc, nlc, col_q):
    assert ncc % GDN_CHUNKS_PER_STEP == 0 and nlc % GDN_CHUNKS_PER_STEP == 0
    n = SCAN_CHUNK * GDN_CHUNKS_PER_STEP
    ncc, nlc = ncc // GDN_CHUNKS_PER_STEP, nlc // GDN_CHUNKS_PER_STEP
    dq = GDN_HEADS * GDN_D
    cq = col_q // dq

    def tok(rev, col):
        return pl.BlockSpec((n, dq), lambda b, s: (_scan_block_index(b, s, rev, n_batch, ncc, nlc), col))

    def small(rev):
        return pl.BlockSpec((n, LANE), lambda b, s: (_scan_block_index(b, s, rev, n_batch, ncc, nlc), 0))

    def small_t(rev):
        return pl.BlockSpec((LANE, n), lambda b, s: (0, _scan_block_index(b, s, rev, n_batch, ncc, nlc)))

    def out(rev):
        return pl.BlockSpec((n, dq), lambda b, s: (_scan_out_index(b, s, rev, ncc, nlc), 0))

    in_specs = []
    for rev in (False, True):
        in_specs += [tok(rev, cq), tok(rev, cq + 1), tok(rev, cq + 2), small(rev), small_t(rev)]
    in_specs += [_const_spec(prow.shape), _const_spec(pcol.shape)]
    t_lat = n_batch * nlc * n
    return pl.pallas_call(
        _gdn_kernel,
        out_shape=(jax.ShapeDtypeStruct((t_lat, dq), F32),) * 2,
        grid=(n_batch, ncc + nlc),
        in_specs=in_specs,
        out_specs=(out(False), out(True)),
        scratch_shapes=[pltpu.VMEM((N_DIR, GDN_HEADS, GDN_D, GDN_D), F32)],
        compiler_params=pltpu.CompilerParams(vmem_limit_bytes=VMEM_LIMIT,
                                             dimension_semantics=("arbitrary", "arbitrary")),
        name="gdn_scan",
    )(cg, cg, cg, sg, sgt, cg, cg, cg, sg, sgt, prow, pcol)


def _ssd_steps(xs_ref, bm_ref, cm_ref, sg_ref, sgt_ref, prow, pcol, ex_ref, state_ref, y_ref, d, rev, rows, order,
               skip_ref=None):
    n = SCAN_CHUNK
    bm, cm, sg, sgt = bm_ref[rows, :], cm_ref[rows, :], sg_ref[rows, :], sgt_ref[:, rows]
    n_heads = xs_ref.shape[1] // MB_HEADDIM
    hpg = n_heads // MB_GROUPS
    gw = hpg * MB_HEADDIM
    pw = 2 * MB_HEADDIM
    ii, jj, incl, _ = _tri_masks(n, rev)
    f0 = 2 * N_DIR * GDN_HEADS + d * n_heads
    heads = lambda a: a[:, f0:f0 + n_heads]
    dt_col = _softplus(sg + prow[3:4, :])
    dt_row = _softplus(sgt[f0:f0 + n_heads, :] + pcol[f0:f0 + n_heads, 3:4])
    acs_col = _expand(dt_col * -jnp.exp(prow[2:3, :]), incl.astype(BF16), 3, e_left=True)
    acs_row = _expand(dt_row * -jnp.exp(pcol[f0:f0 + n_heads, 2:3]),
                      (ii >= jj if rev else ii <= jj).astype(BF16), 3)
    last = 0 if rev else n - 1
    tot = acs_col[last:last + 1, :]
    w_in = heads(dt_col * jnp.exp(tot - acs_col))
    w_out = heads(jnp.exp(acs_col))
    w_tot = heads(jnp.broadcast_to(jnp.exp(tot), (8, LANE)))
    yield
    lane = lax.broadcasted_iota(jnp.int32, (n, pw), 1)
    left = lane < MB_HEADDIM
    grp = []
    for g in range(MB_GROUPS):
        bg = bm[:, g * MB_STATE:(g + 1) * MB_STATE]
        cgm = cm[:, g * MB_STATE:(g + 1) * MB_STATE].astype(BF16)
        grp.append((cgm, _bdot_nt(cgm, bg), bg.T.astype(BF16)))
    pairs = [(g, pr) for g in range(MB_GROUPS) for pr in range(hpg // 2)]
    lo_of = lambda g, pr: (g * hpg + 2 * pr) * MB_HEADDIM
    spread = [ex_ref[:, lo_of(g, pr):lo_of(g, pr) + pw] for g, pr in pairs]
    e_in = [_expand(w_in, sp, 1) for sp in spread]
    e_out = [_expand(w_out, sp, 1) for sp in spread]
    e_tot = [_expand(w_tot, sp, 3)[0:1, :] for sp in spread]
    yield
    xps = [xs_ref[rows, lo_of(g, pr):lo_of(g, pr) + pw] for g, pr in pairs]
    for _ in range(2 * order):
        yield
    sts = [state_ref[d, g, :, lo_of(g, pr) - g * gw:lo_of(g, pr) - g * gw + pw] for g, pr in pairs]
    y_off = [jnp.dot(grp[g][0], st.astype(BF16), preferred_element_type=F32) * eo
             for (g, pr), st, eo in zip(pairs, sts, e_out)]
    yield
    upd = [jnp.dot(grp[g][2], (xp * ei).astype(BF16), preferred_element_type=F32)
           for (g, pr), xp, ei in zip(pairs, xps, e_in)]
    for (g, pr), st, et, u in zip(pairs, sts, e_tot, upd):
        sl = lo_of(g, pr) - g * gw
        state_ref[d, g, :, sl:sl + pw] = st * et + u
    yield
    m2s = []
    for g, pr in pairs:
        ms = []
        for hd in (g * hpg + 2 * pr, g * hpg + 2 * pr + 1):
            ccol = acs_col[:, f0 + hd:f0 + hd + 1]
            crow = acs_row[hd:hd + 1, :]
            seg = jnp.exp(jnp.where(incl, ccol - crow, -jnp.inf))
            ms.append(grp[g][1] * seg * dt_row[hd:hd + 1, :])
        m2s.append(jnp.concatenate(ms, axis=1).astype(BF16))
    xbds = [jnp.concatenate([jnp.where(left, xp, 0.0), jnp.where(left, 0.0, xp)], axis=0).astype(BF16)
            for xp in xps]
    yield
    for (g, pr), m2, xbd, yo, xp in zip(pairs, m2s, xbds, y_off, xps):
        lo = lo_of(g, pr)
        y = jnp.dot(m2, xbd, preferred_element_type=F32) + yo
        if skip_ref is not None:
            y = y + skip_ref[1:2, lo:lo + pw] * xp
        y_ref[rows, lo:lo + pw] = y


def _ssd_kernel(xf, bf, cf, sgf, sgtf, xb, bb, cb, sgb, sgtb, prow_ref, pcol_ref, ex_ref, rows_ref, yf_ref, yb_ref,
                state_ref):
    @pl.when(pl.program_id(1) == 0)
    def _():
        state_ref[...] = jnp.zeros_like(state_ref)

    prow, pcol = prow_ref[...], pcol_ref[...]
    n_sub = xf.shape[0] // SCAN_CHUNK
    live = []
    for c in range(n_sub):
        rows = slice(c * SCAN_CHUNK, (c + 1) * SCAN_CHUNK)
        live.append(_ssd_steps(xf, bf, cf, sgf, sgtf, prow, pcol, ex_ref, state_ref, yf_ref, 0, False, rows, c,
                               skip_ref=rows_ref))
        live.append(_ssd_steps(xb, bb, cb, sgb, sgtb, prow, pcol, ex_ref, state_ref, yb_ref, 1, True, rows,
                               n_sub - 1 - c))
    while live:
        for steps in list(live):
            if next(steps, "done") == "done":
                live.remove(steps)


SSD_CHUNKS_PER_STEP = 2


def _ssd_scan(cg, sg, sgt, prow, pcol, ex, rows, *, n_batch, ncc, nlc, d_inner, col_b):
    assert ncc % SSD_CHUNKS_PER_STEP == 0 and nlc % SSD_CHUNKS_PER_STEP == 0
    n = SCAN_CHUNK * SSD_CHUNKS_PER_STEP
    ncc, nlc = ncc // SSD_CHUNKS_PER_STEP, nlc // SSD_CHUNKS_PER_STEP
    bw = MB_GROUPS * MB_STATE
    cb_ = col_b // bw

    def blk(rev, width, col):
        return pl.BlockSpec((n, width), lambda b, s: (_scan_block_index(b, s, rev, n_batch, ncc, nlc), col))

    def small_t(rev):
        return pl.BlockSpec((LANE, n), lambda b, s: (0, _scan_block_index(b, s, rev, n_batch, ncc, nlc)))

    def out(rev):
        return pl.BlockSpec((n, d_inner), lambda b, s: (_scan_out_index(b, s, rev, ncc, nlc), 0))

    in_specs = []
    for rev in (False, True):
        in_specs += [blk(rev, d_inner, 0), blk(rev, bw, cb_), blk(rev, bw, cb_ + 1), blk(rev, LANE, 0), small_t(rev)]
    in_specs += [_const_spec(prow.shape), _const_spec(pcol.shape), _const_spec(ex.shape), _const_spec(rows.shape)]
    t_lat = n_batch * nlc * n
    return pl.pallas_call(
        _ssd_kernel,
        out_shape=(jax.ShapeDtypeStruct((t_lat, d_inner), F32),) * 2,
        grid=(n_batch, ncc + nlc),
        in_specs=in_specs,
        out_specs=(out(False), out(True)),
        scratch_shapes=[pltpu.VMEM((N_DIR, MB_GROUPS, MB_STATE, d_inner // MB_GROUPS), F32)],
        compiler_params=pltpu.CompilerParams(vmem_limit_bytes=VMEM_LIMIT,
                                             dimension_semantics=("arbitrary", "arbitrary")),
        name="ssd_scan",
    )(cg, cg, cg, sg, sgt, cg, cg, cg, sg, sgt, prow, pcol, ex, rows)


def _mixout_kernel(x_ref, mod_ref, g_ref, wp_ref, of_ref, ob_ref, yf_ref, yb_ref,
                   rows_ref, wbg_ref, wbm_ref, wo_ref, o_ref, *, d_model, d_v, d_inner):
    x = x_ref[...]
    hb = _modulated_norm(x, g_ref[...], mod_ref[0, 3:4, :], mod_ref[0, 4:5, :]).astype(BF16)
    gate = mod_ref[0, 5:6, :]
    z = jnp.dot(hb, wp_ref[...], preferred_element_type=F32)
    za = z[:, :d_v]
    zb = z[:, d_v:d_v + d_inner]
    ga = z[:, d_v + d_inner:d_v + d_inner + d_model]
    gb = z[:, d_v + d_inner + d_model:]

    o = of_ref[...] + ob_ref[...]
    parts = []
    for hh in range(d_v // GDN_D):
        oh = o[:, hh * GDN_D:(hh + 1) * GDN_D]
        parts.append(oh * lax.rsqrt(jnp.mean(oh * oh, axis=-1, keepdims=True) + EPS))
    ya = jnp.concatenate(parts, axis=1) * rows_ref[0:1, :d_v] * _silu(za)

    yb = (yf_ref[...] + yb_ref[...]) * _silu(zb)
    gw = d_inner // MB_GROUPS
    parts = []
    for g in range(MB_GROUPS):
        yg = yb[:, g * gw:(g + 1) * gw]
        parts.append(yg * lax.rsqrt(jnp.mean(yg * yg, axis=-1, keepdims=True) + EPS))
    yb = jnp.concatenate(parts, axis=1) * rows_ref[2:3, :]

    merged = (jax.nn.sigmoid(ga) * _bdot(ya, wbg_ref[...]) + jax.nn.sigmoid(gb) * _bdot(yb, wbm_ref[...]))
    o_ref[...] = x + gate * _bdot(merged, wo_ref[...])


def _mixout(x1, mod3, g, w_plain, o_f, o_b, y_f, y_b, rows, wbg, wbm, wo, *, tm, n_ctx_tiles, seq):
    t_lat, d_v = o_f.shape
    d_inner = y_f.shape[1]
    d = x1.shape[1]
    return pl.pallas_call(
        functools.partial(_mixout_kernel, d_model=d, d_v=d_v, d_inner=d_inner),
        out_shape=jax.ShapeDtypeStruct((t_lat, d), F32),
        grid=(t_lat // tm,),
        in_specs=[pl.BlockSpec((tm, d), lambda i: (i + n_ctx_tiles, 0)),
                  pl.BlockSpec((1,) + mod3.shape[1:], lambda i: (1 + (i * tm) // seq, 0, 0)),
                  _const_spec(g.shape), _const_spec(w_plain.shape),
                  pl.BlockSpec((tm, d_v), lambda i: (i, 0)), pl.BlockSpec((tm, d_v), lambda i: (i, 0)),
                  pl.BlockSpec((tm, d_inner), lambda i: (i, 0)), pl.BlockSpec((tm, d_inner), lambda i: (i, 0)),
                  _const_spec(rows.shape), _const_spec(wbg.shape), _const_spec(wbm.shape), _const_spec(wo.shape)],
        out_specs=pl.BlockSpec((tm, d), lambda i: (i, 0)),
        compiler_params=pltpu.CompilerParams(vmem_limit_bytes=VMEM_LIMIT),
        name="mixout",
    )(x1, mod3, g, w_plain, o_f, o_b, y_f, y_b, rows, wbg, wbm, wo)


def kernel(x, c, ctx, c_ctx, w_ada, b_ada, norm_g, ffn_w_gu, ffn_w_down, w_in, gdn_conv_w, gdn_A_log, gdn_dt_bias, gdn_norm_g, mb_conv_w, mb_conv_b, mb_A_log, mb_dt_bias, mb_D, mb_norm_g, w_branch_gdn, w_branch_mb, w_out, final_g):
    n_batch, seq, d = x.shape
    ctx_len = ctx.shape[1]
    assert w_ada.shape[0] == 1, "single-layer operation"
    d_qk = GDN_HEADS * GDN_D
    d_v = GDN_HEADS * GDN_D
    d_inner = mb_norm_g.shape[1]
    n_mb_heads = d_inner // MB_HEADDIM
    d_bc = MB_GROUPS * MB_STATE
    tm = 256
    t_ctx, t_lat = n_batch * ctx_len, n_batch * seq
    assert t_ctx % tm == 0 and seq % tm == 0
    assert ctx_len % SCAN_CHUNK == 0 and seq % SCAN_CHUNK == 0
    assert N_DIR * (2 * GDN_HEADS + n_mb_heads) <= LANE
    n_ctx_tiles = t_ctx // tm

    rows = 16
    cc = jnp.concatenate([c_ctx[None, :], c, jnp.zeros((rows - 1 - n_batch, d), F32)], axis=0)
    mod3 = _adaln(cc, w_ada[0], b_ada[0][None, :]).reshape(rows, 9, d)

    tm_ffn = 2 * tm
    assert t_ctx % tm_ffn == 0 and seq % tm_ffn == 0 and tm_ffn % ctx_len == 0 and tm_ffn % GRID_W == 0

    def row_ffn(i):
        return jnp.where(i < t_ctx // tm_ffn, 0, 1 + ((i - t_ctx // tm_ffn) * tm_ffn) // seq)

    ffn_gu, ffn_down = ffn_w_gu[0].astype(BF16), ffn_w_down[0].astype(BF16)
    x1 = _ffn(ctx.reshape(t_ctx, d), x.reshape(t_lat, d), mod3, norm_g[0, 0][None, :], ffn_gu, ffn_down,
              final_g[None, :], k=0, row_of_tile=row_ffn, tm=tm_ffn, final=False, name="ffn1")

    sizes = (2 * d_qk + d_v, d_v, N_DIR * GDN_HEADS, N_DIR * GDN_HEADS, d_inner, d_inner + 2 * d_bc,
             N_DIR * n_mb_heads)
    o_qkv, o_za, o_a, o_beta, o_zb, o_xbc, o_dt, o_gates = [sum(sizes[:j]) for j in range(len(sizes) + 1)]

    def regroup(a, off_qkv, off_xbc):
        return jnp.concatenate([a[..., off_xbc:off_xbc + d_inner], a[..., off_qkv:off_qkv + 2 * d_qk + d_v],
                                a[..., off_xbc + d_inner:off_xbc + d_inner + 2 * d_bc]], axis=-1)

    n_small = N_DIR * (2 * GDN_HEADS + n_mb_heads)
    w_cg, w_small, w_plain = _regroup_w_in(
        w_in[0].T,
        cg_cols=((o_xbc, o_xbc + d_inner), (o_qkv, o_za), (o_xbc + d_inner, o_dt)),
        small_cols=((o_a, o_zb), (o_dt, o_gates)),
        plain_cols=((o_za, o_a), (o_zb, o_xbc), (o_gates, o_gates + 2 * d)))
    conv_all = jnp.concatenate([gdn_conv_w[0], mb_conv_w[0]], axis=1)
    bias_all = jnp.concatenate([jnp.zeros((1, 2 * d_qk + d_v), F32), mb_conv_b[0][None, :]], axis=1)
    conv_p = regroup(jnp.concatenate([conv_all, bias_all, jnp.zeros((2, conv_all.shape[1]), F32)], axis=0),
                     0, 2 * d_qk + d_v)
    col_q = d_inner
    col_b = d_inner + 2 * d_qk + d_v
    blocks = ((0, d_inner // 2, "x"), (d_inner // 2, d_inner, "x"), (col_q, col_q + d_qk, "q"),
              (col_q + d_qk, col_q + 2 * d_qk, "k"), (col_q + 2 * d_qk, col_b, "v"), (col_b, col_b + 2 * d_bc, "x"))
    cg, sg, sgt = _inproj(x1, mod3, norm_g[0, 1][None, :], w_cg, w_small, conv_p, tm=tm_ffn,
                          n_ctx_tiles=t_ctx // tm_ffn,
                          ctx_len=ctx_len, seq=seq, blocks=blocks)

    pad = jnp.zeros((LANE - n_small,), F32)
    zeros_g = jnp.zeros((N_DIR * GDN_HEADS,), F32)
    zeros_m = jnp.zeros((N_DIR * n_mb_heads,), F32)
    prow = jnp.stack([jnp.concatenate([gdn_A_log[0].reshape(-1), zeros_g, zeros_m, pad]),
                      jnp.concatenate([gdn_dt_bias[0].reshape(-1), zeros_g, zeros_m, pad]),
                      jnp.concatenate([zeros_g, zeros_g, mb_A_log[0].reshape(-1), pad]),
                      jnp.concatenate([zeros_g, zeros_g, mb_dt_bias[0].reshape(-1), pad])]
                     + [jnp.zeros((LANE,), F32)] * 4)
    pcol = prow.T
    ncc, nlc = ctx_len // SCAN_CHUNK, seq // SCAN_CHUNK
    o_f, o_b = _gdn_scan(cg, sg, sgt, prow, pcol, n_batch=n_batch, ncc=ncc, nlc=nlc, col_q=col_q)

    ex = (jnp.arange(n_mb_heads)[:, None] == (jnp.arange(d_inner) // MB_HEADDIM)[None, :]).astype(BF16)
    rows3 = jnp.stack([jnp.concatenate([jnp.tile(gdn_norm_g[0], GDN_HEADS), jnp.zeros((d_inner - d_v,), F32)]),
                       jnp.repeat(mb_D[0], MB_HEADDIM), mb_norm_g[0]] + [jnp.zeros((d_inner,), F32)] * 5)
    y_f, y_b = _ssd_scan(cg, sg, sgt, prow, pcol, ex, rows3, n_batch=n_batch, ncc=ncc, nlc=nlc, d_inner=d_inner,
                         col_b=col_b)

    x2 = _mixout(x1, mod3, norm_g[0, 1][None, :], w_plain, o_f, o_b, y_f, y_b, rows3,
                 w_branch_gdn[0].astype(BF16), w_branch_mb[0].astype(BF16), w_out[0].astype(BF16),
                 tm=tm, n_ctx_tiles=n_ctx_tiles, seq=seq)

    out = _ffn(None, x2, mod3, norm_g[0, 2][None, :], ffn_gu, ffn_down, final_g[None, :], k=2, row_of_tile=lambda i: 1 + (i * tm_ffn) // seq, tm=tm_ffn, final=True, name="ffn2")
    return out.reshape(n_batch, seq, d)
```

```python
import functools

import jax
import jax.numpy as jnp
import numpy as np
from jax import lax
from jax.experimental import pallas as pl
from jax.experimental.pallas import tpu as pltpu

F32 = jnp.float32
BF16 = jnp.bfloat16

EPS = 1e-6
FFN_RES = 0.5
GRID_W = 64
CONV_K = 5
GDN_HEADS = 8
GDN_D = 128
MB_HEADDIM = 64
MB_GROUPS = 2
MB_STATE = 128
N_DIR = 2

LANE = 128
SCAN_CHUNK = 128
VMEM_LIMIT = 56 * 1024 * 1024


def _const_spec(shape):
    nd = len(shape)
    return pl.BlockSpec(shape, lambda *_: (0,) * nd, pipeline_mode=pl.Buffered(1))


def _silu(x):
    return x * jax.nn.sigmoid(x)


def _softplus(x):
    return jnp.maximum(x, 0.0) + jnp.log(1.0 + jnp.exp(-jnp.abs(x)))


def _bdot(a, b):
    return jnp.dot(a.astype(BF16), b.astype(BF16), preferred_element_type=F32)


def _bdot_nt(a, b):
    return lax.dot_general(a.astype(BF16), b.astype(BF16), (((1,), (1,)), ((), ())),
                           preferred_element_type=F32)


def _modulated_norm(x, g, shift, scale):
    ms = jnp.mean(x * x, axis=-1, keepdims=True)
    return x * lax.rsqrt(ms + EPS) * g * (1.0 + scale) + shift


def _adaln_kernel(c_ref, w_ref, b_ref, o_ref):
    s = _silu(c_ref[...])
    o_ref[...] = _split_dot(s, w_ref) + b_ref[...]


def _adaln(cc, w, b, tn=1152):
    rows, d = cc.shape
    n = w.shape[1]
    return pl.pallas_call(
        _adaln_kernel,
        out_shape=jax.ShapeDtypeStruct((rows, n), F32),
        grid=(n // tn,),
        in_specs=[pl.BlockSpec((rows, d), lambda j: (0, 0)),
                  pl.BlockSpec((d, tn), lambda j: (0, j)),
                  pl.BlockSpec((1, tn), lambda j: (0, j))],
        out_specs=pl.BlockSpec((rows, tn), lambda j: (0, j)),
        compiler_params=pltpu.CompilerParams(vmem_limit_bytes=VMEM_LIMIT, dimension_semantics=("parallel",)),
        name="adaln",
    )(cc, w, b)


def _ffn_kernel(c_ref, x_ref, mod_ref, g_ref, wgu_ref, wd_ref, fg_ref, o_ref, *, k, d_ff, final, n_ctx_tiles):
    if n_ctx_tiles:
        x = jnp.where(pl.program_id(0) < n_ctx_tiles, c_ref[...], x_ref[...])
    else:
        x = x_ref[...]
    shift = mod_ref[0, 3 * k:3 * k + 1, :]
    scale = mod_ref[0, 3 * k + 1:3 * k + 2, :]
    gate = mod_ref[0, 3 * k + 2:3 * k + 3, :]
    hb = _modulated_norm(x, g_ref[...], shift, scale).astype(BF16)
    gt = jnp.dot(hb, wgu_ref[0, :, :d_ff], preferred_element_type=F32)
    up = jnp.dot(hb, wgu_ref[0, :, d_ff:], preferred_element_type=F32)
    act = (_silu(gt) * up).astype(BF16)
    y = x + FFN_RES * gate * jnp.dot(act, wd_ref[0], preferred_element_type=F32)
    if final:
        ms = jnp.mean(y * y, axis=-1, keepdims=True)
        y = y * lax.rsqrt(ms + EPS) * fg_ref[...]
    o_ref[...] = y


def _ffn(xc, x, mod3, g, wgu, wd, fg, *, k, row_of_tile, tm, final, name):
    t_lat, d = x.shape
    d_ff = wd.shape[1]
    which = k // 2
    w_spec = lambda w: pl.BlockSpec((1,) + w.shape[1:], lambda i: (which, 0, 0), pipeline_mode=pl.Buffered(1))
    if xc is None:
        nct = 0
        xc, c_spec = x, pl.BlockSpec((8, d), lambda i: (0, 0))
    else:
        nct = xc.shape[0] // tm
        c_spec = pl.BlockSpec((tm, d), lambda i: (jnp.minimum(i, nct - 1), 0))
    t = nct * tm + t_lat
    return pl.pallas_call(
        functools.partial(_ffn_kernel, k=k, d_ff=d_ff, final=final, n_ctx_tiles=nct),
        out_shape=jax.ShapeDtypeStruct((t, d), F32),
        grid=(t // tm,),
        in_specs=[c_spec, pl.BlockSpec((tm, d), lambda i: (jnp.maximum(i - nct, 0), 0)),
                  pl.BlockSpec((1,) + mod3.shape[1:], lambda i: (row_of_tile(i), 0, 0)),
                  _const_spec(g.shape), w_spec(wgu), w_spec(wd), _const_spec(fg.shape)],
        out_specs=pl.BlockSpec((tm, d), lambda i: (i, 0)),
        compiler_params=pltpu.CompilerParams(vmem_limit_bytes=VMEM_LIMIT, dimension_semantics=("parallel",)),
        name=name,
    )(xc, x, mod3, g, wgu, wd, fg)


def _regroup_kernel(offs_ref, wt_ref, *rest, n_cg_blocks, n_small_in):
    del offs_ref
    small_in = rest[:n_small_in]
    cg_ref, small_ref, plain_ref = rest[n_small_in:]
    j = pl.program_id(0)
    blk = wt_ref[...].T.astype(BF16)

    @pl.when(j < n_cg_blocks)
    def _():
        cg_ref[...] = blk

    @pl.when(j >= n_cg_blocks)
    def _():
        plain_ref[...] = blk

    @pl.when(j == 0)
    def _():
        rows = [r[...] for r in small_in]
        pad = LANE - sum(r.shape[0] for r in rows)
        small_ref[...] = jnp.concatenate(rows + [jnp.zeros((pad, rows[0].shape[1]), F32)], axis=0).T


def _regroup_w_in(wt, cg_cols, small_cols, plain_cols, bw=512):
    d = wt.shape[1]
    width = lambda cols: sum(b - a for a, b in cols)
    assert all((b - a) % bw == 0 for a, b in cg_cols + plain_cols)
    starts = lambda cols: [s for a, b in cols for s in range(a, b, bw)]
    all_starts = starts(cg_cols) + starts(plain_cols)
    unit = 8
    assert all(s % unit == 0 for s in all_starts)
    offs = jnp.asarray([s // unit for s in all_starts], jnp.int32)
    ncg = len(starts(cg_cols))
    in_specs = [pl.BlockSpec((pl.Element(bw), pl.Element(d)), lambda j, offs: (offs[j] * unit, 0))]
    in_specs += [pl.BlockSpec((pl.Element(b - a), pl.Element(d)), lambda j, offs, a=a: (a, 0))
                 for a, b in small_cols]
    out_specs = (pl.BlockSpec((d, bw), lambda j, offs: (0, jnp.minimum(j, ncg - 1))),
                 pl.BlockSpec((d, LANE), lambda j, offs: (0, 0)),
                 pl.BlockSpec((d, bw), lambda j, offs: (0, jnp.maximum(j - ncg, 0))))
    return pl.pallas_call(
        functools.partial(_regroup_kernel, n_cg_blocks=ncg, n_small_in=len(small_cols)),
        out_shape=(jax.ShapeDtypeStruct((d, width(cg_cols)), BF16), jax.ShapeDtypeStruct((d, LANE), F32),
                   jax.ShapeDtypeStruct((d, width(plain_cols)), BF16)),
        grid_spec=pltpu.PrefetchScalarGridSpec(num_scalar_prefetch=1, grid=(offs.shape[0],),
                                               in_specs=in_specs, out_specs=out_specs),
        compiler_params=pltpu.CompilerParams(vmem_limit_bytes=VMEM_LIMIT,
                                             dimension_semantics=("arbitrary",)),
        name="regroup_w_in",
    )(offs, wt, *([wt] * len(small_cols)))


def _conv_shift_matrix(tm, period):
    t = np.arange(tm)[:, None]
    s = np.arange(tm)[None, :]
    half = CONV_K // 2
    mats = [(s == t + dlt) & (t // period == (t + dlt) // period)
            for dlt in range(-half, half + 1) if dlt]
    return np.concatenate(mats, axis=1).astype(np.float32)


CONV_SPLIT = 4
CONV_HALO = 16


def _conv_shift_parts(tm, period):
    taps = CONV_K - 1
    r = tm // CONV_SPLIT
    rest = _conv_shift_matrix(tm, period).reshape(tm, taps, tm)
    blocks = []
    for s in range(CONV_SPLIT):
        rows = slice(s * r, (s + 1) * r)
        blocks.append(rest[rows, :, rows].reshape(r, taps * r).copy())
        rest[rows, :, rows] = 0
    corr = [np.zeros((2 * CONV_HALO, taps * 2 * CONV_HALO), np.float32)]
    for b in range(1, CONV_SPLIT):
        win = slice(b * r - CONV_HALO, b * r + CONV_HALO)
        corr.append(rest[win, :, win].reshape(2 * CONV_HALO, taps * 2 * CONV_HALO).copy())
        rest[win, :, win] = 0
    assert not rest.any()
    return np.stack(blocks), np.stack(corr[-max(CONV_SPLIT - 1, 1):])


def _split_dot(a, b_ref):
    b = b_ref[...]
    a1 = a.astype(BF16)
    a2 = (a - a1.astype(F32)).astype(BF16)
    b1 = b.astype(BF16)
    b2 = (b - b1.astype(F32)).astype(BF16)
    dot = functools.partial(jnp.dot, preferred_element_type=F32)
    return dot(a1, b1) + (dot(a1, b2) + dot(a2, b1))


def _inproj_kernel(x_ref, mod_ref, g_ref, w_ref, ws_ref, cp_ref, sb_ref, sc_ref, cg_ref, sg_ref, sgt_ref, *,
                   blocks):
    x = x_ref[...]
    tm = x.shape[0]
    r = tm // CONV_SPLIT
    h = _modulated_norm(x, g_ref[...], mod_ref[0, 3:4, :], mod_ref[0, 4:5, :])
    hb = h.astype(BF16)
    small = _split_dot(h, ws_ref)
    sg_ref[...] = small
    sgt_ref[...] = small.T
    half = CONV_K // 2
    for c0, c1, kind in blocks:
        u = jnp.dot(hb, w_ref[:, c0:c1], preferred_element_type=F32)
        taps = [(u * cp_ref[half + dlt:half + dlt + 1, c0:c1]).astype(BF16)
                for dlt in range(-half, half + 1) if dlt]
        conv = [jnp.dot(sb_ref[0, s], jnp.concatenate([tp[s * r:(s + 1) * r] for tp in taps], axis=0),
                        preferred_element_type=F32) for s in range(CONV_SPLIT)]
        pieces = []
        for s in range(CONV_SPLIT):
            if s > 0:
                pieces[-1] = pieces[-1] + cross[:CONV_HALO]
                pieces.append(conv[s][:CONV_HALO] + cross[CONV_HALO:])
                pieces.append(conv[s][CONV_HALO:r - CONV_HALO] if s < CONV_SPLIT - 1 else conv[s][CONV_HALO:])
            else:
                pieces.append(conv[s][:r - CONV_HALO] if CONV_SPLIT > 1 else conv[s])
            if s < CONV_SPLIT - 1:
                pieces.append(conv[s][r - CONV_HALO:])
                b = (s + 1) * r
                cross = jnp.dot(sc_ref[0, s], jnp.concatenate([tp[b - CONV_HALO:b + CONV_HALO] for tp in taps],
                                                              axis=0), preferred_element_type=F32)
        acc = u * cp_ref[half:half + 1, c0:c1] + jnp.concatenate(pieces, axis=0)
        if kind == "x":
            acc = acc + cp_ref[CONV_K:CONV_K + 1, c0:c1]
        y = _silu(acc)
        if kind in ("q", "k"):
            parts = []
            for hh in range((c1 - c0) // GDN_D):
                yh = y[:, hh * GDN_D:(hh + 1) * GDN_D]
                inv = lax.rsqrt(jnp.sum(yh * yh, axis=-1, keepdims=True) + EPS)
                if kind == "q":
                    inv = inv * (GDN_D ** -0.5)
                parts.append(yh * inv)
            y = jnp.concatenate(parts, axis=1)
        cg_ref[:, c0:c1] = y


def _inproj(x1, mod3, g, w_cg, w_small, conv_p, *, tm, n_ctx_tiles, ctx_len, seq, blocks):
    t, d = x1.shape
    ncg = w_cg.shape[1]

    def row_of_tile(i):
        return jnp.where(i < n_ctx_tiles, 0, 1 + ((i - n_ctx_tiles) * tm) // seq)

    parts = [_conv_shift_parts(tm, ctx_len), _conv_shift_parts(tm, GRID_W)]
    sblk = jnp.asarray(np.stack([p[0] for p in parts]), dtype=BF16)
    scor = jnp.asarray(np.stack([p[1] for p in parts]), dtype=BF16)
    variant = lambda i: (jnp.where(i < n_ctx_tiles, 0, 1), 0, 0, 0)
    return pl.pallas_call(
        functools.partial(_inproj_kernel, blocks=blocks),
        out_shape=(jax.ShapeDtypeStruct((t, ncg), F32), jax.ShapeDtypeStruct((t, LANE), F32),
                   jax.ShapeDtypeStruct((LANE, t), F32)),
        grid=(t // tm,),
        in_specs=[pl.BlockSpec((tm, d), lambda i: (i, 0)),
                  pl.BlockSpec((1,) + mod3.shape[1:], lambda i: (row_of_tile(i), 0, 0)),
                  _const_spec(g.shape), _const_spec(w_cg.shape), _const_spec(w_small.shape),
                  _const_spec(conv_p.shape),
                  pl.BlockSpec((1,) + sblk.shape[1:], variant), pl.BlockSpec((1,) + scor.shape[1:], variant)],
        out_specs=(pl.BlockSpec((tm, ncg), lambda i: (i, 0)), pl.BlockSpec((tm, LANE), lambda i: (i, 0)),
                   pl.BlockSpec((LANE, tm), lambda i: (0, i))),
        compiler_params=pltpu.CompilerParams(vmem_limit_bytes=VMEM_LIMIT, dimension_semantics=("parallel",)),
        name="inproj",
    )(x1, mod3, g, w_cg, w_small, conv_p, sblk, scor)


def _expand(x, e, pieces, e_left=False):
    acc = None
    r = x
    for _ in range(pieces):
        xp = r.astype(BF16)
        r = r - xp.astype(F32)
        term = jnp.dot(e, xp, preferred_element_type=F32) if e_left else jnp.dot(xp, e, preferred_element_type=F32)
        acc = term if acc is None else acc + term
    return acc


def _tri_masks(n, rev):
    ii = lax.broadcasted_iota(jnp.int32, (n, n), 0)
    jj = lax.broadcasted_iota(jnp.int32, (n, n), 1)
    if rev:
        return ii, jj, ii <= jj, ii < jj
    return ii, jj, ii >= jj, ii > jj


def _scan_block_index(b, s, rev, n_batch, ncc, nlc):
    lc = jnp.maximum(s - ncc, 0)
    if rev:
        ctx_blk = b * ncc + (ncc - 1 - jnp.minimum(s, ncc - 1))
        lat_blk = n_batch * ncc + b * nlc + (nlc - 1 - lc)
    else:
        ctx_blk = b * ncc + jnp.minimum(s, ncc - 1)
        lat_blk = n_batch * ncc + b * nlc + lc
    return jnp.where(s < ncc, ctx_blk, lat_blk)


def _scan_out_index(b, s, rev, ncc, nlc):
    lc = jnp.maximum(s - ncc, 0)
    return b * nlc + ((nlc - 1 - lc) if rev else lc)


def _gdn_kernel(qf, kf, vf, sgf, sgtf, qb, kb_, vb, sgb, sgtb, prow_ref, pcol_ref, of_ref, ob_ref, state_ref):
    @pl.when(pl.program_id(1) == 0)
    def _():
        state_ref[...] = jnp.zeros_like(state_ref)

    n = SCAN_CHUNK
    prow, pcol = prow_ref[...], pcol_ref[...]
    probs = []
    n_sub = qf.shape[0] // n
    for d, rev, refs, c in [(0, False, (qf, kf, vf, sgf, sgtf, of_ref), c) for c in range(n_sub)] + \
                           [(1, True, (qb, kb_, vb, sgb, sgtb, ob_ref), c) for c in range(n_sub)]:
        rows = slice(c * n, (c + 1) * n)
        q, k, v, sg = [r[rows, :] for r in refs[:4]]
        n_g = N_DIR * GDN_HEADS
        sgt = refs[4][0:n_g, rows]
        g_col = -jnp.exp(prow[0:1, :]) * _softplus(sg + prow[1:2, :])
        g_row = -jnp.exp(pcol[0:n_g, 0:1]) * _softplus(sgt + pcol[0:n_g, 1:2])
        beta_col = jax.nn.sigmoid(sg)
        ii, jj, incl, strict = _tri_masks(n, rev)
        gc_col = _expand(g_col, incl.astype(BF16), 3, e_left=True)
        gc_row = _expand(g_row, (ii >= jj if rev else ii <= jj).astype(BF16), 3)
        last = 0 if rev else n - 1
        for hh in range(GDN_HEADS):
            ci = d * GDN_HEADS + hh
            lo, hi = hh * GDN_D, (hh + 1) * GDN_D
            qh, kh, vh = q[:, lo:hi], k[:, lo:hi], v[:, lo:hi]
            gcol = gc_col[:, ci:ci + 1]
            grow = gc_row[ci:ci + 1, :]
            bcol = beta_col[:, N_DIR * GDN_HEADS + ci:N_DIR * GDN_HEADS + ci + 1]
            eg = jnp.exp(gcol)
            decay = jnp.exp(jnp.where(incl, gcol - grow, -jnp.inf))
            kb = kh * bcol
            g_last = gcol[last:last + 1, :]
            probs.append(dict(d=d, hh=hh, lo=lo, hi=hi, ii=ii, jj=jj, strict=strict, decay=decay, o_ref=refs[5],
                              rows=rows, order=(n_sub - 1 - c) if rev else c,
                              lhs=jnp.concatenate([kb, qh], axis=0).astype(BF16), kh=kh.astype(BF16),
                              rhs=jnp.concatenate([vh * bcol, kb * eg], axis=1).astype(BF16),
                              q_in=(qh * eg).astype(BF16), k_out=(kh * jnp.exp(g_last - gcol)).astype(BF16),
                              g_tot=jnp.exp(g_last)))
    kk = [_bdot_nt(p["lhs"], p["kh"]) for p in probs]
    for p, kkp in zip(probs, kk):
        p["a"] = jnp.where(p["strict"], kkp[:n] * p["decay"], 0.0)
        p["qk"] = (kkp[n:] * p["decay"]).astype(BF16)
    lg = 3
    for p in probs:
        blk = (p["ii"] >> lg) == (p["jj"] >> lg)
        p["a8"] = jnp.where(blk, p["a"], 0.0).astype(BF16)
        p["t"] = (p["ii"] == p["jj"]).astype(F32) - jnp.where(blk, p["a"], 0.0)
    x2 = [_bdot(p["a8"], p["a8"]).astype(BF16) for p in probs]
    x4 = [_bdot(x, x).astype(BF16) for x in x2]
    t1 = [p["t"] + _bdot(p["t"], x) for p, x in zip(probs, x2)]
    ts = [t + _bdot(t, x) for t, x in zip(t1, x4)]
    while (1 << lg) < n:
        s_blk = 1 << lg

        def pick(x, rev):
            return jnp.concatenate([x[b0 + (0 if rev else s_blk):b0 + (s_blk if rev else 2 * s_blk)]
                                    for b0 in range(0, n, 2 * s_blk)], axis=0)

        def merge(full, part, rev):
            rows = []
            for idx, b0 in enumerate(range(0, n, 2 * s_blk)):
                new = part[idx * s_blk:(idx + 1) * s_blk]
                old = (jnp.zeros((s_blk, n), F32) if full is None else
                       full[b0 + (s_blk if rev else 0):b0 + (2 * s_blk if rev else s_blk)])
                rows += [new, old] if rev else [old, new]
            return jnp.concatenate(rows, axis=0)

        bs = []
        for p in probs:
            rev = p["d"] == 1
            r = lax.broadcasted_iota(jnp.int32, (n // 2, n), 0)
            pi = ((r >> lg) << (lg + 1)) + (r & (s_blk - 1)) + (0 if rev else s_blk)
            pj = lax.broadcasted_iota(jnp.int32, (n // 2, n), 1)
            off = ((pi >> (lg + 1)) == (pj >> (lg + 1))) & ((pi >> lg) != (pj >> lg))
            bs.append(jnp.where(off, pick(p["a"], rev), 0.0).astype(BF16))
        tb = [t.astype(BF16) for t in ts]
        bt = [_bdot(b, t) for b, t in zip(bs, tb)]
        bt_full = [merge(None, x, p["d"] == 1).astype(BF16) for x, p in zip(bt, probs)]
        upd = [_bdot(pick(t, p["d"] == 1), x) for t, x, p in zip(ts, bt_full, probs)]
        ts = [merge(t, pick(t, p["d"] == 1) - u, p["d"] == 1) for t, u, p in zip(ts, upd, probs)]
        lg += 1
    sol = [_bdot(t, p["rhs"]) for t, p in zip(ts, probs)]
    state = {(d, hh): state_ref[d, hh] for d in range(N_DIR) for hh in range(GDN_HEADS)}
    for order in range(n_sub):
        cur = [(p, s) for p, s in zip(probs, sol) if p["order"] == order]
        stb = [state[p["d"], p["hh"]].astype(BF16) for p, _ in cur]
        ws = [jnp.dot(s[:, GDN_D:].astype(BF16), sb, preferred_element_type=F32) for (_, s), sb in zip(cur, stb)]
        v_new = [(s[:, :GDN_D] - w).astype(BF16) for (_, s), w in zip(cur, ws)]
        for (p, _), sb, vn in zip(cur, stb, v_new):
            p["o_ref"][p["rows"], p["lo"]:p["hi"]] = jnp.dot(
                jnp.concatenate([p["q_in"], p["qk"]], axis=1), jnp.concatenate([sb, vn], axis=0),
                preferred_element_type=F32)
            state[p["d"], p["hh"]] = state[p["d"], p["hh"]] * p["g_tot"] + lax.dot_general(
                p["k_out"], vn, (((0,), (0,)), ((), ())), preferred_element_type=F32)
    for (d, hh), sta in state.items():
        state_ref[d, hh] = sta


GDN_CHUNKS_PER_STEP = 2


def _gdn_scan(cg, sg, sgt, prow, pcol, *, n_batch, ncc, nlc, col_q):
    assert ncc % GDN_CHUNKS_PER_STEP == 0 and nlc % GDN_CHUNKS_PER_STEP == 0
    n = SCAN_CHUNK * GDN_CHUNKS_PER_STEP
    ncc, nlc = ncc // GDN_CHUNKS_PER_STEP, nlc // GDN_CHUNKS_PER_STEP
    dq = GDN_HEADS * GDN_D
    cq = col_q // dq

    def tok(rev, col):
        return pl.BlockSpec((n, dq), lambda b, s: (_scan_block_index(b, s, rev, n_batch, ncc, nlc), col))

    def small(rev):
        return pl.BlockSpec((n, LANE), lambda b, s: (_scan_block_index(b, s, rev, n_batch, ncc, nlc), 0))

    def small_t(rev):
        return pl.BlockSpec((LANE, n), lambda b, s: (0, _scan_block_index(b, s, rev, n_batch, ncc, nlc)))

    def out(rev):
        return pl.BlockSpec((n, dq), lambda b, s: (_scan_out_index(b, s, rev, ncc, nlc), 0))

    in_specs = []
    for rev in (False, True):
        in_specs += [tok(rev, cq), tok(rev, cq + 1), tok(rev, cq + 2), small(rev), small_t(rev)]
    in_specs += [_const_spec(prow.shape), _const_spec(pcol.shape)]
    t_lat = n_batch * nlc * n
    return pl.pallas_call(
        _gdn_kernel,
        out_shape=(jax.ShapeDtypeStruct((t_lat, dq), F32),) * 2,
        grid=(n_batch, ncc + nlc),
        in_specs=in_specs,
        out_specs=(out(False), out(True)),
        scratch_shapes=[pltpu.VMEM((N_DIR, GDN_HEADS, GDN_D, GDN_D), F32)],
        compiler_params=pltpu.CompilerParams(vmem_limit_bytes=VMEM_LIMIT,
                                             dimension_semantics=("arbitrary", "arbitrary")),
        name="gdn_scan",
    )(cg, cg, cg, sg, sgt, cg, cg, cg, sg, sgt, prow, pcol)


def _ssd_steps(xs_ref, bm_ref, cm_ref, sg_ref, sgt_ref, prow, pcol, ex_ref, state_ref, y_ref, d, rev, rows, order,
               skip_ref=None):
    n = SCAN_CHUNK
    bm, cm, sg, sgt = bm_ref[rows, :], cm_ref[rows, :], sg_ref[rows, :], sgt_ref[:, rows]
    n_heads = xs_ref.shape[1] // MB_HEADDIM
    hpg = n_heads // MB_GROUPS
    gw = hpg * MB_HEADDIM
    pw = 2 * MB_HEADDIM
    ii, jj, incl, _ = _tri_masks(n, rev)
    f0 = 2 * N_DIR * GDN_HEADS + d * n_heads
    heads = lambda a: a[:, f0:f0 + n_heads]
    dt_col = _softplus(sg + prow[3:4, :])
    dt_row = _softplus(sgt[f0:f0 + n_heads, :] + pcol[f0:f0 + n_heads, 3:4])
    acs_col = _expand(dt_col * -jnp.exp(prow[2:3, :]), incl.astype(BF16), 3, e_left=True)
    acs_row = _expand(dt_row * -jnp.exp(pcol[f0:f0 + n_heads, 2:3]),
                      (ii >= jj if rev else ii <= jj).astype(BF16), 3)
    last = 0 if rev else n - 1
    tot = acs_col[last:last + 1, :]
    w_in = heads(dt_col * jnp.exp(tot - acs_col))
    w_out = heads(jnp.exp(acs_col))
    w_tot = heads(jnp.broadcast_to(jnp.exp(tot), (8, LANE)))
    yield
    lane = lax.broadcasted_iota(jnp.int32, (n, pw), 1)
    left = lane < MB_HEADDIM
    grp = []
    for g in range(MB_GROUPS):
        bg = bm[:, g * MB_STATE:(g + 1) * MB_STATE]
        cgm = cm[:, g * MB_STATE:(g + 1) * MB_STATE].astype(BF16)
        grp.append((cgm, _bdot_nt(cgm, bg), bg.T.astype(BF16)))
    pairs = [(g, pr) for g in range(MB_GROUPS) for pr in range(hpg // 2)]
    lo_of = lambda g, pr: (g * hpg + 2 * pr) * MB_HEADDIM
    spread = [ex_ref[:, lo_of(g, pr):lo_of(g, pr) + pw] for g, pr in pairs]
    e_in = [_expand(w_in, sp, 1) for sp in spread]
    e_out = [_expand(w_out, sp, 1) for sp in spread]
    e_tot = [_expand(w_tot, sp, 3)[0:1, :] for sp in spread]
    yield
    xps = [xs_ref[rows, lo_of(g, pr):lo_of(g, pr) + pw] for g, pr in pairs]
    for _ in range(2 * order):
        yield
    sts = [state_ref[d, g, :, lo_of(g, pr) - g * gw:lo_of(g, pr) - g * gw + pw] for g, pr in pairs]
    y_off = [jnp.dot(grp[g][0], st.astype(BF16), preferred_element_type=F32) * eo
             for (g, pr), st, eo in zip(pairs, sts, e_out)]
    yield
    upd = [jnp.dot(grp[g][2], (xp * ei).astype(BF16), preferred_element_type=F32)
           for (g, pr), xp, ei in zip(pairs, xps, e_in)]
    for (g, pr), st, et, u in zip(pairs, sts, e_tot, upd):
        sl = lo_of(g, pr) - g * gw
        state_ref[d, g, :, sl:sl + pw] = st * et + u
    yield
    m2s = []
    for g, pr in pairs:
        ms = []
        for hd in (g * hpg + 2 * pr, g * hpg + 2 * pr + 1):
            ccol = acs_col[:, f0 + hd:f0 + hd + 1]
            crow = acs_row[hd:hd + 1, :]
            seg = jnp.exp(jnp.where(incl, ccol - crow, -jnp.inf))
            ms.append(grp[g][1] * seg * dt_row[hd:hd + 1, :])
        m2s.append(jnp.concatenate(ms, axis=1).astype(BF16))
    xbds = [jnp.concatenate([jnp.where(left, xp, 0.0), jnp.where(left, 0.0, xp)], axis=0).astype(BF16)
            for xp in xps]
    yield
    for (g, pr), m2, xbd, yo, xp in zip(pairs, m2s, xbds, y_off, xps):
        lo = lo_of(g, pr)
        y = jnp.dot(m2, xbd, preferred_element_type=F32) + yo
        if skip_ref is not None:
            y = y + skip_ref[1:2, lo:lo + pw] * xp
        y_ref[rows, lo:lo + pw] = y


def _ssd_kernel(xf, bf, cf, sgf, sgtf, xb, bb, cb, sgb, sgtb, prow_ref, pcol_ref, ex_ref, rows_ref, yf_ref, yb_ref,
                state_ref):
    @pl.when(pl.program_id(1) == 0)
    def _():
        state_ref[...] = jnp.zeros_like(state_ref)

    prow, pcol = prow_ref[...], pcol_ref[...]
    n_sub = xf.shape[0] // SCAN_CHUNK
    live = []
    for c in range(n_sub):
        rows = slice(c * SCAN_CHUNK, (c + 1) * SCAN_CHUNK)
        live.append(_ssd_steps(xf, bf, cf, sgf, sgtf, prow, pcol, ex_ref, state_ref, yf_ref, 0, False, rows, c,
                               skip_ref=rows_ref))
        live.append(_ssd_steps(xb, bb, cb, sgb, sgtb, prow, pcol, ex_ref, state_ref, yb_ref, 1, True, rows,
                               n_sub - 1 - c))
    while live:
        for steps in list(live):
            if next(steps, "done") == "done":
                live.remove(steps)


SSD_CHUNKS_PER_STEP = 2


def _ssd_scan(cg, sg, sgt, prow, pcol, ex, rows, *, n_batch, ncc, nlc, d_inner, col_b):
    assert ncc % SSD_CHUNKS_PER_STEP == 0 and nlc % SSD_CHUNKS_PER_STEP == 0
    n = SCAN_CHUNK * SSD_CHUNKS_PER_STEP
    ncc, nlc = ncc // SSD_CHUNKS_PER_STEP, nlc // SSD_CHUNKS_PER_STEP
    bw = MB_GROUPS * MB_STATE
    cb_ = col_b // bw

    def blk(rev, width, col):
        return pl.BlockSpec((n, width), lambda b, s: (_scan_block_index(b, s, rev, n_batch, ncc, nlc), col))

    def small_t(rev):
        return pl.BlockSpec((LANE, n), lambda b, s: (0, _scan_block_index(b, s, rev, n_batch, ncc, nlc)))

    def out(rev):
        return pl.BlockSpec((n, d_inner), lambda b, s: (_scan_out_index(b, s, rev, ncc, nlc), 0))

    in_specs = []
    for rev in (False, True):
        in_specs += [blk(rev, d_inner, 0), blk(rev, bw, cb_), blk(rev, bw, cb_ + 1), blk(rev, LANE, 0), small_t(rev)]
    in_specs += [_const_spec(prow.shape), _const_spec(pcol.shape), _const_spec(ex.shape), _const_spec(rows.shape)]
    t_lat = n_batch * nlc * n
    return pl.pallas_call(
        _ssd_kernel,
        out_shape=(jax.ShapeDtypeStruct((t_lat, d_inner), F32),) * 2,
        grid=(n_batch, ncc + nlc),
        in_specs=in_specs,
        out_specs=(out(False), out(True)),
        scratch_shapes=[pltpu.VMEM((N_DIR, MB_GROUPS, MB_STATE, d_inner // MB_GROUPS), F32)],
        compiler_params=pltpu.CompilerParams(vmem_limit_bytes=VMEM_LIMIT,
                                             dimension_semantics=("arbitrary", "arbitrary")),
        name="ssd_scan",
    )(cg, cg, cg, sg, sgt, cg, cg, cg, sg, sgt, prow, pcol, ex, rows)


def _mixout_kernel(x_ref, mod_ref, g_ref, wp_ref, of_ref, ob_ref, yf_ref, yb_ref,
                   rows_ref, wbg_ref, wbm_ref, wo_ref, o_ref, *, d_model, d_v, d_inner):
    x = x_ref[...]
    hb = _modulated_norm(x, g_ref[...], mod_ref[0, 3:4, :], mod_ref[0, 4:5, :]).astype(BF16)
    gate = mod_ref[0, 5:6, :]
    z = jnp.dot(hb, wp_ref[...], preferred_element_type=F32)
    za = z[:, :d_v]
    zb = z[:, d_v:d_v + d_inner]
    ga = z[:, d_v + d_inner:d_v + d_inner + d_model]
    gb = z[:, d_v + d_inner + d_model:]

    o = of_ref[...] + ob_ref[...]
    parts = []
    for hh in range(d_v // GDN_D):
        oh = o[:, hh * GDN_D:(hh + 1) * GDN_D]
        parts.append(oh * lax.rsqrt(jnp.mean(oh * oh, axis=-1, keepdims=True) + EPS))
    ya = jnp.concatenate(parts, axis=1) * rows_ref[0:1, :d_v] * _silu(za)

    yb = (yf_ref[...] + yb_ref[...]) * _silu(zb)
    gw = d_inner // MB_GROUPS
    parts = []
    for g in range(MB_GROUPS):
        yg = yb[:, g * gw:(g + 1) * gw]
        parts.append(yg * lax.rsqrt(jnp.mean(yg * yg, axis=-1, keepdims=True) + EPS))
    yb = jnp.concatenate(parts, axis=1) * rows_ref[2:3, :]

    merged = (jax.nn.sigmoid(ga) * _bdot(ya, wbg_ref[...]) + jax.nn.sigmoid(gb) * _bdot(yb, wbm_ref[...]))
    o_ref[...] = x + gate * _bdot(merged, wo_ref[...])


def _mixout(x1, mod3, g, w_plain, o_f, o_b, y_f, y_b, rows, wbg, wbm, wo, *, tm, n_ctx_tiles, seq):
    t_lat, d_v = o_f.shape
    d_inner = y_f.shape[1]
    d = x1.shape[1]
    return pl.pallas_call(
        functools.partial(_mixout_kernel, d_model=d, d_v=d_v, d_inner=d_inner),
        out_shape=jax.ShapeDtypeStruct((t_lat, d), F32),
        grid=(t_lat // tm,),
        in_specs=[pl.BlockSpec((tm, d), lambda i: (i + n_ctx_tiles, 0)),
                  pl.BlockSpec((1,) + mod3.shape[1:], lambda i: (1 + (i * tm) // seq, 0, 0)),
                  _const_spec(g.shape), _const_spec(w_plain.shape),
                  pl.BlockSpec((tm, d_v), lambda i: (i, 0)), pl.BlockSpec((tm, d_v), lambda i: (i, 0)),
                  pl.BlockSpec((tm, d_inner), lambda i: (i, 0)), pl.BlockSpec((tm, d_inner), lambda i: (i, 0)),
                  _const_spec(rows.shape), _const_spec(wbg.shape), _const_spec(wbm.shape), _const_spec(wo.shape)],
        out_specs=pl.BlockSpec((tm, d), lambda i: (i, 0)),
        compiler_params=pltpu.CompilerParams(vmem_limit_bytes=VMEM_LIMIT, dimension_semantics=("parallel",)),
        name="mixout",
    )(x1, mod3, g, w_plain, o_f, o_b, y_f, y_b, rows, wbg, wbm, wo)


def kernel(x, c, ctx, c_ctx, w_ada, b_ada, norm_g, ffn_w_gu, ffn_w_down, w_in, gdn_conv_w, gdn_A_log, gdn_dt_bias, gdn_norm_g, mb_conv_w, mb_conv_b, mb_A_log, mb_dt_bias, mb_D, mb_norm_g, w_branch_gdn, w_branch_mb, w_out, final_g):
    n_batch, seq, d = x.shape
    ctx_len = ctx.shape[1]
    assert w_ada.shape[0] == 1, "single-layer operation"
    d_qk = GDN_HEADS * GDN_D
    d_v = GDN_HEADS * GDN_D
    d_inner = mb_norm_g.shape[1]
    n_mb_heads = d_inner // MB_HEADDIM
    d_bc = MB_GROUPS * MB_STATE
    tm = 256
    t_ctx, t_lat = n_batch * ctx_len, n_batch * seq
    assert t_ctx % tm == 0 and seq % tm == 0
    assert ctx_len % SCAN_CHUNK == 0 and seq % SCAN_CHUNK == 0
    assert N_DIR * (2 * GDN_HEADS + n_mb_heads) <= LANE
    n_ctx_tiles = t_ctx // tm

    rows = 16
    cc = jnp.concatenate([c_ctx[None, :], c, jnp.zeros((rows - 1 - n_batch, d), F32)], axis=0)
    mod3 = _adaln(cc, w_ada[0], b_ada[0][None, :]).reshape(rows, 9, d)

    tm_ffn = 2 * tm
    assert t_ctx % tm_ffn == 0 and seq % tm_ffn == 0 and tm_ffn % ctx_len == 0 and tm_ffn % GRID_W == 0

    def row_ffn(i):
        return jnp.where(i < t_ctx // tm_ffn, 0, 1 + ((i - t_ctx // tm_ffn) * tm_ffn) // seq)

    ffn_gu, ffn_down = ffn_w_gu[0].astype(BF16), ffn_w_down[0].astype(BF16)
    x1 = _ffn(ctx.reshape(t_ctx, d), x.reshape(t_lat, d), mod3, norm_g[0, 0][None, :], ffn_gu, ffn_down,
              final_g[None, :], k=0, row_of_tile=row_ffn, tm=tm_ffn, final=False, name="ffn1")

    sizes = (2 * d_qk + d_v, d_v, N_DIR * GDN_HEADS, N_DIR * GDN_HEADS, d_inner, d_inner + 2 * d_bc,
             N_DIR * n_mb_heads)
    o_qkv, o_za, o_a, o_beta, o_zb, o_xbc, o_dt, o_gates = [sum(sizes[:j]) for j in range(len(sizes) + 1)]

    def regroup(a, off_qkv, off_xbc):
        return jnp.concatenate([a[..., off_xbc:off_xbc + d_inner], a[..., off_qkv:off_qkv + 2 * d_qk + d_v],
                                a[..., off_xbc + d_inner:off_xbc + d_inner + 2 * d_bc]], axis=-1)

    n_small = N_DIR * (2 * GDN_HEADS + n_mb_heads)
    w_cg, w_small, w_plain = _regroup_w_in(
        w_in[0].T,
        cg_cols=((o_xbc, o_xbc + d_inner), (o_qkv, o_za), (o_xbc + d_inner, o_dt)),
        small_cols=((o_a, o_zb), (o_dt, o_gates)),
        plain_cols=((o_za, o_a), (o_zb, o_xbc), (o_gates, o_gates + 2 * d)))
    conv_all = jnp.concatenate([gdn_conv_w[0], mb_conv_w[0]], axis=1)
    bias_all = jnp.concatenate([jnp.zeros((1, 2 * d_qk + d_v), F32), mb_conv_b[0][None, :]], axis=1)
    conv_p = regroup(jnp.concatenate([conv_all, bias_all, jnp.zeros((2, conv_all.shape[1]), F32)], axis=0),
                     0, 2 * d_qk + d_v)
    col_q = d_inner
    col_b = d_inner + 2 * d_qk + d_v
    blocks = ((0, d_inner // 2, "x"), (d_inner // 2, d_inner, "x"), (col_q, col_q + d_qk, "q"),
              (col_q + d_qk, col_q + 2 * d_qk, "k"), (col_q + 2 * d_qk, col_b, "v"), (col_b, col_b + 2 * d_bc, "x"))
    cg, sg, sgt = _inproj(x1, mod3, norm_g[0, 1][None, :], w_cg, w_small, conv_p, tm=tm_ffn,
                          n_ctx_tiles=t_ctx // tm_ffn,
                          ctx_len=ctx_len, seq=seq, blocks=blocks)

    pad = jnp.zeros((LANE - n_small,), F32)
    zeros_g = jnp.zeros((N_DIR * GDN_HEADS,), F32)
    zeros_m = jnp.zeros((N_DIR * n_mb_heads,), F32)
    prow = jnp.stack([jnp.concatenate([gdn_A_log[0].reshape(-1), zeros_g, zeros_m, pad]),
                      jnp.concatenate([gdn_dt_bias[0].reshape(-1), zeros_g, zeros_m, pad]),
                      jnp.concatenate([zeros_g, zeros_g, mb_A_log[0].reshape(-1), pad]),
                      jnp.concatenate([zeros_g, zeros_g, mb_dt_bias[0].reshape(-1), pad])]
                     + [jnp.zeros((LANE,), F32)] * 4)
    pcol = prow.T
    ncc, nlc = ctx_len // SCAN_CHUNK, seq // SCAN_CHUNK
    o_f, o_b = _gdn_scan(cg, sg, sgt, prow, pcol, n_batch=n_batch, ncc=ncc, nlc=nlc, col_q=col_q)

    ex = (jnp.arange(n_mb_heads)[:, None] == (jnp.arange(d_inner) // MB_HEADDIM)[None, :]).astype(BF16)
    rows3 = jnp.stack([jnp.concatenate([jnp.tile(gdn_norm_g[0], GDN_HEADS), jnp.zeros((d_inner - d_v,), F32)]),
                       jnp.repeat(mb_D[0], MB_HEADDIM), mb_norm_g[0]] + [jnp.zeros((d_inner,), F32)] * 5)
    y_f, y_b = _ssd_scan(cg, sg, sgt, prow, pcol, ex, rows3, n_batch=n_batch, ncc=ncc, nlc=nlc, d_inner=d_inner,
                         col_b=col_b)

    x2 = _mixout(x1, mod3, norm_g[0, 1][None, :], w_plain, o_f, o_b, y_f, y_b, rows3,
                 w_branch_gdn[0].astype(BF16), w_branch_mb[0].astype(BF16), w_out[0].astype(BF16),
                 tm=tm, n_ctx_tiles=n_ctx_tiles, seq=seq)

    out = _ffn(None, x2, mod3, norm_g[0, 2][None, :], ffn_gu, ffn_down, final_g[None, :], k=2, row_of_tile=lambda i: 1 + (i * tm_ffn) // seq, tm=tm_ffn, final=True, name="ffn2")
    return out.reshape(n_batch, seq, d)
```
